```python
import jax
import jax.numpy as jnp
from jax import lax
import numpy as np

D_MODEL = 1024
BATCH = 2
SEQ = 8192
DEPTH = 4

GRID_W = 64
CTX_LEN = 256
EPS = 1e-6

POOL_WIDTH = 512
POOL_WINDOWS = (2, 4, 8, 16)
POOL_GROUP = POOL_WIDTH // len(POOL_WINDOWS)

HY_WIDTH = 512
HY_EMB = 33
HY_BANDS = (HY_EMB - 1) // 2
HY_FILTER_HIDDEN = 64

MLA_HEADS = 8
QK_NOPE = 64
QK_ROPE = 32
QK_DIM = QK_NOPE + QK_ROPE
V_HEAD = 64
Q_LORA = 384
KV_LORA = 256
MLA_SCALE = QK_DIM ** -0.5
ROPE_BASE = 10000.0
Q_BLOCK = 128

D_FF = 4 * D_MODEL
N_BRANCH = 3
N_MOD = 6

OFF_POOL = 0
OFF_HY = OFF_POOL + POOL_WIDTH
OFF_Q = OFF_HY + 3 * HY_WIDTH
OFF_KV = OFF_Q + Q_LORA
OFF_GATE = OFF_KV + KV_LORA + QK_ROPE
D_IN = OFF_GATE + N_BRANCH * D_MODEL

kernel_name = 'hybrid_pool_hyena_mla_prefix_dit'


def rmsnorm(x, g):
    xf = x.astype(jnp.float32)
    y = xf * lax.rsqrt(jnp.mean(xf * xf, axis=-1, keepdims=True) + EPS)
    return (y * g.astype(jnp.float32)).astype(x.dtype)


def axial_rope_tables(n_rows):
    f32 = jnp.float32
    rows = jnp.repeat(jnp.arange(n_rows, dtype=f32), GRID_W)
    cols = jnp.tile(jnp.arange(GRID_W, dtype=f32), n_rows)
    half = QK_ROPE // 2
    inv = ROPE_BASE ** (-jnp.arange(0, half, 2, dtype=f32) / half)
    ar = rows[:, None] * inv
    ac = cols[:, None] * inv
    ang = jnp.concatenate([ar, ar, ac, ac], axis=-1)
    return jnp.cos(ang), jnp.sin(ang)


def apply_axial_rope(x, cos, sin):
    q = QK_ROPE // 4
    rot = jnp.concatenate([-x[..., q:2 * q], x[..., :q], -x[..., 3 * q:], x[..., 2 * q:3 * q]], axis=-1)
    return x * cos.astype(x.dtype) + rot * sin.astype(x.dtype)


def pool_mixer(p, w_group, scale):
    f32 = jnp.float32
    b_, L, _ = p.shape
    pf = p.astype(f32)
    cs = jnp.concatenate([jnp.zeros((b_, 1, POOL_WIDTH), f32), jnp.cumsum(pf, axis=1)], axis=1)
    t = jnp.arange(L)
    outs = []
    for gi, w in enumerate(POOL_WINDOWS):
        lo = jnp.clip(t - w // 2, 0, L)
        hi = jnp.clip(t + w // 2, 0, L)
        sl = slice(gi * POOL_GROUP, (gi + 1) * POOL_GROUP)
        csg = cs[:, :, sl]
        mean = (jnp.take(csg, hi, axis=1) - jnp.take(csg, lo, axis=1)) / (hi - lo).astype(f32)[None, :, None]
        outs.append(mean - pf[:, :, sl])
    u = jnp.stack(outs, axis=2).astype(p.dtype)
    u = jnp.einsum('blgc,gcd->blgd', u, w_group).reshape(b_, L, POOL_WIDTH)
    return u * scale


def short_conv(u, w, b):
    up = jnp.pad(u, ((0, 0), (1, 1), (0, 0)))
    return up[:, :-2] * w[0] + up[:, 1:-1] * w[1] + up[:, 2:] * w[2] + b


def hyena_filter(L, w1, b1, freq1, w2, b2, freq2, w3, decay):
    f32 = jnp.float32
    t = jnp.linspace(0.0, 1.0, L, dtype=f32)[:, None]
    omega = 2.0 * np.pi * jnp.arange(L, dtype=f32)[:, None] / L
    bands = jnp.linspace(1e-4, HY_BANDS - 1, HY_BANDS, dtype=f32)[None, :]
    z = jnp.concatenate([t, jnp.cos(omega * bands), -jnp.sin(omega * bands)], axis=-1)
    hid = jnp.sin(freq1.astype(f32) * (z @ w1.astype(f32) + b1.astype(f32)))
    hid = jnp.sin(freq2.astype(f32) * (hid @ w2.astype(f32) + b2.astype(f32)))
    h = (hid @ w3.astype(f32)).reshape(L, 2, HY_WIDTH)
    h = h * jnp.exp(-t[:, :, None] * jnp.abs(decay.astype(f32)))
    h_fwd, h_bwd = h[:, 0], h[1:, 1]
    l1 = jnp.sum(jnp.abs(h_fwd), axis=0) + jnp.sum(jnp.abs(h_bwd), axis=0)
    filt = jnp.concatenate([h_fwd, jnp.zeros((1, HY_WIDTH), f32), h_bwd[::-1]], axis=0)
    return filt / l1


def long_conv(u, filt):
    L = u.shape[1]
    u_f = jnp.fft.rfft(u.astype(jnp.float32), n=2 * L, axis=1)
    f_f = jnp.fft.rfft(filt, n=2 * L, axis=0)
    y = jnp.fft.irfft(u_f * f_f[None], n=2 * L, axis=1)[:, :L]
    return y.astype(u.dtype)


def hyena_mixer(u, lp):
    L = u.shape[1]
    x0, x1, v = jnp.split(short_conv(u, lp['hy_conv_w'], lp['hy_conv_b']), 3, axis=-1)
    filt = hyena_filter(L, lp['hy_f_w1'], lp['hy_f_b1'], lp['hy_f_freq1'], lp['hy_f_w2'],
                        lp['hy_f_b2'], lp['hy_f_freq2'], lp['hy_f_w3'], lp['hy_decay'])
    zv = v * x1
    return x0 * (long_conv(zv, filt) + zv * lp['hy_bias'])


def mla_query(c_q, q_norm_g, w_uq, cos, sin):
    b_, L, _ = c_q.shape
    q = (rmsnorm(c_q, q_norm_g) @ w_uq).reshape(b_, L, MLA_HEADS, QK_DIM)
    if cos is None:
        return q
    q_pe = apply_axial_rope(q[..., QK_NOPE:], cos[None, :, None, :], sin[None, :, None, :])
    return jnp.concatenate([q[..., :QK_NOPE], q_pe], axis=-1)


def mla_kv(c_kv, kv_norm_g, w_ukv, cos, sin):
    b_, L, _ = c_kv.shape
    kv = (rmsnorm(c_kv[..., :KV_LORA], kv_norm_g) @ w_ukv).reshape(b_, L, MLA_HEADS, QK_NOPE + V_HEAD)
    k_nope, v = kv[..., :QK_NOPE], kv[..., QK_NOPE:]
    k_pe = c_kv[..., KV_LORA:]
    if cos is not None:
        k_pe = apply_axial_rope(k_pe, cos[None], sin[None])
    k_pe = jnp.broadcast_to(k_pe[:, :, None, :], (b_, L, MLA_HEADS, QK_ROPE))
    return jnp.concatenate([k_nope, k_pe], axis=-1), v


def attend(q, k, v):
    s = jnp.einsum('bqhd,bkhd->bhqk', q, k).astype(jnp.float32) * MLA_SCALE
    p = jax.nn.softmax(s, axis=-1).astype(v.dtype)
    return jnp.einsum('bhqk,bkhd->bqhd', p, v)


def blocked_attend(q, k, v):
    b_, L, H, dk = q.shape
    qb = q.reshape(b_, L // Q_BLOCK, Q_BLOCK, H, dk).transpose(1, 0, 2, 3, 4)
    ob = lax.map(lambda qi: attend(qi, k, v), qb)
    return ob.transpose(1, 0, 2, 3, 4).reshape(b_, L, H, V_HEAD)


def token_mix(proj, k_ctx, v_ctx, lp, cos, sin):
    b_, L, _ = proj.shape
    ga, gb, gc = jnp.split(jax.nn.sigmoid(proj[..., OFF_GATE:]), N_BRANCH, axis=-1)
    a = pool_mixer(proj[..., OFF_POOL:OFF_HY], lp['pool_w'], lp['pool_scale']) @ lp['pool_out']
    b = hyena_mixer(proj[..., OFF_HY:OFF_Q], lp) @ lp['hy_out']
    q = mla_query(proj[..., OFF_Q:OFF_KV], lp['q_norm_g'], lp['w_uq'], cos, sin)
    if cos is None:
        o = attend(q, k_ctx, v_ctx)
    else:
        k_lat, v_lat = mla_kv(proj[..., OFF_KV:OFF_GATE], lp['kv_norm_g'], lp['w_ukv'], cos, sin)
        o = blocked_attend(q, jnp.concatenate([k_ctx, k_lat], axis=1), jnp.concatenate([v_ctx, v_lat], axis=1))
    cm = o.reshape(b_, L, MLA_HEADS * V_HEAD) @ lp['w_o']
    return (ga * a + gb * b + gc * cm) @ lp['w_out']


def sq_relu_mlp(x, w1, w2):
    return jnp.square(jax.nn.relu(x @ w1)) @ w2


def setup_inputs(seed: int = 0) -> dict:
    key = jax.random.key(seed)
    ks = iter(jax.random.split(key, 40))
    f32 = jnp.float32

    def nrm(shape, s):
        return jax.random.normal(next(ks), shape, f32) * s

    def gain(shape):
        return 1.0 + nrm(shape, 0.02)

    L_ = DEPTH
    return {
        'x': nrm((BATCH, SEQ, D_MODEL), 1.0),
        'c': nrm((BATCH, D_MODEL), 1.0),
        'ctx': nrm((BATCH, CTX_LEN, D_MODEL), 1.0),
        'c_ctx': nrm((D_MODEL,), 1.0),
        'w_mod': nrm((L_, D_MODEL, N_MOD * D_MODEL), 0.5 * D_MODEL ** -0.5),
        'b_mod': nrm((L_, N_MOD * D_MODEL), 0.02),
        'norm1_g': gain((L_, D_MODEL)),
        'norm2_g': gain((L_, D_MODEL)),
        'w_in': nrm((L_, D_MODEL, D_IN), D_MODEL ** -0.5),
        'pool_w': nrm((L_, len(POOL_WINDOWS), POOL_GROUP, POOL_GROUP), POOL_GROUP ** -0.5),
        'pool_scale': 1.0 + nrm((L_, POOL_WIDTH), 0.1),
        'pool_out': nrm((L_, POOL_WIDTH, D_MODEL), POOL_WIDTH ** -0.5),
        'hy_conv_w': nrm((L_, 3, 3 * HY_WIDTH), 3 ** -0.5),
        'hy_conv_b': nrm((L_, 3 * HY_WIDTH), 0.02),
        'hy_f_w1': nrm((L_, HY_EMB, HY_FILTER_HIDDEN), HY_EMB ** -0.5),
        'hy_f_b1': nrm((L_, HY_FILTER_HIDDEN), 0.1),
        'hy_f_freq1': 1.0 + nrm((L_, HY_FILTER_HIDDEN), 0.1),
        'hy_f_w2': nrm((L_, HY_FILTER_HIDDEN, HY_FILTER_HIDDEN), HY_FILTER_HIDDEN ** -0.5),
        'hy_f_b2': nrm((L_, HY_FILTER_HIDDEN), 0.1),
        'hy_f_freq2': 1.0 + nrm((L_, HY_FILTER_HIDDEN), 0.1),
        'hy_f_w3': nrm((L_, HY_FILTER_HIDDEN, 2 * HY_WIDTH), HY_FILTER_HIDDEN ** -0.5),
        'hy_decay': jax.random.uniform(next(ks), (L_, 2, HY_WIDTH), f32, 3.0, 15.0),
        'hy_bias': nrm((L_, HY_WIDTH), 0.5),
        'hy_out': nrm((L_, HY_WIDTH, D_MODEL), HY_WIDTH ** -0.5),
        'q_norm_g': gain((L_, Q_LORA)),
        'w_uq': nrm((L_, Q_LORA, MLA_HEADS * QK_DIM), Q_LORA ** -0.5),
        'kv_norm_g': gain((L_, KV_LORA)),
        'w_ukv': nrm((L_, KV_LORA, MLA_HEADS * (QK_NOPE + V_HEAD)), KV_LORA ** -0.5),
        'w_o': nrm((L_, MLA_HEADS * V_HEAD, D_MODEL), (MLA_HEADS * V_HEAD) ** -0.5),
        'w_out': nrm((L_, D_MODEL, D_MODEL), D_MODEL ** -0.5),
        'w_ff1': nrm((L_, D_MODEL, D_FF), D_MODEL ** -0.5),
        'w_ff2': nrm((L_, D_FF, D_MODEL), D_FF ** -0.5),
        'final_g': gain((D_MODEL,)),
    }


def reference(x, c, ctx, c_ctx, w_mod, b_mod, norm1_g, norm2_g, w_in, pool_w, pool_scale, pool_out,
              hy_conv_w, hy_conv_b, hy_f_w1, hy_f_b1, hy_f_freq1, hy_f_w2, hy_f_b2, hy_f_freq2, hy_f_w3,
              hy_decay, hy_bias, hy_out, q_norm_g, w_uq, kv_norm_g, w_ukv, w_o, w_out, w_ff1, w_ff2,
              final_g):
    ROWS = x.shape[1] // GRID_W
    cos, sin = axial_rope_tables(ROWS)
    s_lat = jax.nn.silu(c)
    s_ctx = jax.nn.silu(c_ctx)
    h, hc = x, ctx
    for l in range(DEPTH):
        lp = {
            'pool_w': pool_w[l], 'pool_scale': pool_scale[l], 'pool_out': pool_out[l],
            'hy_conv_w': hy_conv_w[l], 'hy_conv_b': hy_conv_b[l],
            'hy_f_w1': hy_f_w1[l], 'hy_f_b1': hy_f_b1[l], 'hy_f_freq1': hy_f_freq1[l],
            'hy_f_w2': hy_f_w2[l], 'hy_f_b2': hy_f_b2[l], 'hy_f_freq2': hy_f_freq2[l],
            'hy_f_w3': hy_f_w3[l], 'hy_decay': hy_decay[l], 'hy_bias': hy_bias[l], 'hy_out': hy_out[l],
            'q_norm_g': q_norm_g[l], 'w_uq': w_uq[l], 'kv_norm_g': kv_norm_g[l], 'w_ukv': w_ukv[l],
            'w_o': w_o[l], 'w_out': w_out[l],
        }
        last = l == DEPTH - 1
        sh1, sc1, g1, sh2, sc2, g2 = jnp.split((s_lat @ w_mod[l] + b_mod[l])[:, None, :], N_MOD, axis=-1)
        mod_c = s_ctx @ w_mod[l] + b_mod[l]
        sh1c, sc1c = mod_c[:D_MODEL], mod_c[D_MODEL:2 * D_MODEL]
        xn = rmsnorm(h, norm1_g[l]) * (1 + sc1) + sh1
        xnc = rmsnorm(hc, norm1_g[l]) * (1 + sc1c) + sh1c
        if last:
            proj_c = None
            ckv_c = xnc @ w_in[l][:, OFF_KV:OFF_GATE]
        else:
            proj_c = xnc @ w_in[l]
            ckv_c = proj_c[..., OFF_KV:OFF_GATE]
        k_c, v_c = mla_kv(ckv_c, kv_norm_g[l], w_ukv[l], None, None)
        h = h + g1 * token_mix(xn @ w_in[l], k_c, v_c, lp, cos, sin)
        h = h + g2 * sq_relu_mlp(rmsnorm(h, norm2_g[l]) * (1 + sc2) + sh2, w_ff1[l], w_ff2[l])
        if not last:
            g1c, sh2c, sc2c, g2c = jnp.split(mod_c[2 * D_MODEL:], 4)
            hc = hc + g1c * token_mix(proj_c, k_c, v_c, lp, None, None)
            hc = hc + g2c * sq_relu_mlp(rmsnorm(hc, norm2_g[l]) * (1 + sc2c) + sh2c, w_ff1[l], w_ff2[l])
    return rmsnorm(h, final_g)
```

```python
import functools
import math

import jax
import jax.numpy as jnp
import numpy as np
from jax import lax
from jax.experimental import pallas as pl
from jax.experimental.pallas import tpu as pltpu

F32 = jnp.float32
BF16 = jnp.bfloat16

D_MODEL = 1024
DEPTH = 4
GRID_W = 64
EPS = 1e-6

POOL_WIDTH = 512
POOL_WINDOWS = (2, 4, 8, 16)
POOL_GROUP = 128
HY_WIDTH = 512
HY_EMB = 33
HY_BANDS = 16
MLA_HEADS = 8
QK_NOPE = 64
QK_ROPE = 32
QK_DIM = 96
V_HEAD = 64
Q_LORA = 384
KV_LORA = 256
MLA_SCALE = QK_DIM ** -0.5
ROPE_BASE = 10000.0
D_FF = 4 * D_MODEL
N_MOD = 6

LANE = 128
HALO = 16
HEAD_PAD = 128

COL_POOL = 0
COL_HY = 512
COL_QKV = 2048
COL_GATE = 3072
N_PROJ = 6144

VMEM_LIMIT = 50 * 1024 * 1024


def _cparams(sem):
    return pltpu.CompilerParams(dimension_semantics=sem, vmem_limit_bytes=VMEM_LIMIT)


def _mm(a, b):
    return jnp.dot(a, b, preferred_element_type=F32)


def _split(x):
    hi = x.astype(BF16)
    lo = (x - hi.astype(F32)).astype(BF16)
    return hi, lo


def _dot3(a, b):
    ah, al = _split(a)
    bh, bl = _split(b)
    return _mm(ah, bh) + _mm(al, bh) + _mm(ah, bl)


def _dot3c(a, bh, bl):
    ah, al = _split(a)
    return _mm(ah, bh) + _mm(al, bh) + _mm(ah, bl)


def _cdot3(ch, cl, b):
    bh, bl = _split(b)
    n = b.shape[1]
    r = _mm(ch, jnp.concatenate([bh, bl], axis=1))
    return r[:, :n] + r[:, n:] + _mm(cl, bh)


def _rot_half(x):
    q = QK_ROPE // 4
    return jnp.concatenate([-x[..., q:2 * q], x[..., :q], -x[..., 3 * q:], x[..., 2 * q:3 * q]], axis=-1)


def _mod_kernel(c_ref, w_ref, b_ref, o_ref):
    c = c_ref[...]
    s = c * jax.nn.sigmoid(c)
    o_ref[0] = _dot3(s, w_ref[0]) + b_ref[0]


def _modulation(c_all, w_mod, b_mod):
    tn = 1536
    n = N_MOD * D_MODEL
    return pl.pallas_call(
        _mod_kernel,
        grid=(DEPTH, n // tn),
        in_specs=[pl.BlockSpec((8, D_MODEL), lambda l, j: (0, 0)),
                  pl.BlockSpec((1, D_MODEL, tn), lambda l, j: (l, 0, j)),
                  pl.BlockSpec((1, 1, tn), lambda l, j: (l, 0, j))],
        out_specs=pl.BlockSpec((1, 8, tn), lambda l, j: (l, 0, j)),
        out_shape=jax.ShapeDtypeStruct((DEPTH, 8, n), F32),
        compiler_params=_cparams(("arbitrary", "arbitrary")),
        name="modulation",
    )(c_all, w_mod, b_mod.reshape(DEPTH, 1, n))


def _mod_norm(x, g, sc, sh):
    ms = jnp.mean(x * x, axis=-1, keepdims=True)
    return (x * lax.rsqrt(ms + EPS) * g) * (1.0 + sc) + sh


def _inproj_kernel(x_ref, sh_ref, sc_ref, g_ref, w_ref, o_ref):
    xn = _mod_norm(x_ref[0], g_ref[...], sc_ref[0], sh_ref[0])
    o_ref[0] = _mm(xn.astype(BF16), w_ref[...]).astype(o_ref.dtype)


def _inproj(h, mod, g, w, tt):
    b_, L, _ = h.shape
    tn = N_PROJ // 2
    return pl.pallas_call(
        _inproj_kernel,
        grid=(N_PROJ // tn, b_, L // tt),
        in_specs=[pl.BlockSpec((1, tt, D_MODEL), lambda j, b, i: (b, i, 0)),
                  pl.BlockSpec((1, 1, D_MODEL), lambda j, b, i: (b, 0, 0)),
                  pl.BlockSpec((1, 1, D_MODEL), lambda j, b, i: (b, 0, 1)),
                  pl.BlockSpec((1, D_MODEL), lambda j, b, i: (0, 0)),
                  pl.BlockSpec((D_MODEL, tn), lambda j, b, i: (0, j))],
        out_specs=pl.BlockSpec((1, tt, tn), lambda j, b, i: (b, i, j)),
        out_shape=jax.ShapeDtypeStruct((b_, L, N_PROJ), BF16),
        compiler_params=_cparams(("arbitrary", "arbitrary", "arbitrary")),
        name="inproj",
    )(h, mod, mod, g, w)


def _seqlocal_kernel(cur_ref, prev_ref, next_ref, pw_ref, ps_ref, cw_ref, cb_ref,
                     a_ref, x0_ref, zvt_ref, *, tt, L):
    i = pl.program_id(1)
    nt = pl.num_programs(1)
    keep_prev = (i > 0).astype(F32)
    keep_next = (i < nt - 1).astype(F32)
    n = tt + 2 * HALO

    def ext(lo, hi):
        return jnp.concatenate([prev_ref[0, :, lo:hi].astype(F32) * keep_prev,
                                cur_ref[0, :, lo:hi].astype(F32),
                                next_ref[0, :, lo:hi].astype(F32) * keep_next], axis=0)

    def roll(x, s):
        return pltpu.roll(x, s % n, 0)

    t = i * tt + lax.broadcasted_iota(jnp.int32, (tt, POOL_GROUP), 0)
    for gi, w in enumerate(POOL_WINDOWS):
        lo = COL_POOL + gi * POOL_GROUP
        xg = ext(lo, lo + POOL_GROUP)
        s = xg + roll(xg, 1)
        if w >= 4:
            s = roll(s, 1) + roll(s, -1)
        if w >= 8:
            s = roll(s, 2) + roll(s, -2)
        if w >= 16:
            s = roll(s, 4) + roll(s, -4)
        cnt = (jnp.minimum(t + w // 2, L) - jnp.maximum(t - w // 2, 0)).astype(F32)
        u = s[HALO:HALO + tt] / cnt - xg[HALO:HALO + tt]
        ug = _mm(u.astype(BF16), pw_ref[gi]) * ps_ref[:, gi * POOL_GROUP:(gi + 1) * POOL_GROUP]
        a_ref[0, :, gi * POOL_GROUP:(gi + 1) * POOL_GROUP] = ug.astype(a_ref.dtype)

    def conv(part, j):
        c0 = part * HY_WIDTH + j * LANE
        xe = ext(COL_HY + c0, COL_HY + c0 + LANE)
        y = (roll(xe, 1) * cw_ref[0:1, c0:c0 + LANE] + xe * cw_ref[1:2, c0:c0 + LANE]
             + roll(xe, -1) * cw_ref[2:3, c0:c0 + LANE] + cb_ref[:, c0:c0 + LANE])
        return y[HALO:HALO + tt]

    for j in range(HY_WIDTH // LANE):
        x0_ref[0, :, j * LANE:(j + 1) * LANE] = conv(0, j).astype(x0_ref.dtype)
        zv = conv(2, j) * conv(1, j)
        zvt_ref[0, j * LANE:(j + 1) * LANE, :] = zv.T


def _seqlocal(proj, pool_w, pool_scale, conv_w, conv_b, tt):
    b_, L, _ = proj.shape
    hb = tt // HALO
    nh = L // HALO
    wide = COL_QKV
    return pl.pallas_call(
        functools.partial(_seqlocal_kernel, tt=tt, L=L),
        grid=(b_, L // tt),
        in_specs=[pl.BlockSpec((1, tt, wide), lambda b, i: (b, i, 0)),
                  pl.BlockSpec((1, HALO, wide), lambda b, i: (b, jnp.maximum(i * hb - 1, 0), 0)),
                  pl.BlockSpec((1, HALO, wide), lambda b, i: (b, jnp.minimum((i + 1) * hb, nh - 1), 0)),
                  pl.BlockSpec((len(POOL_WINDOWS), POOL_GROUP, POOL_GROUP), lambda b, i: (0, 0, 0)),
                  pl.BlockSpec((1, POOL_WIDTH), lambda b, i: (0, 0)),
                  pl.BlockSpec((3, 3 * HY_WIDTH), lambda b, i: (0, 0)),
                  pl.BlockSpec((1, 3 * HY_WIDTH), lambda b, i: (0, 0))],
        out_specs=[pl.BlockSpec((1, tt, POOL_WIDTH), lambda b, i: (b, i, 0)),
                   pl.BlockSpec((1, tt, HY_WIDTH), lambda b, i: (b, i, 0)),
                   pl.BlockSpec((1, HY_WIDTH, tt), lambda b, i: (b, 0, i))],
        out_shape=[jax.ShapeDtypeStruct((b_, L, POOL_WIDTH), BF16),
                   jax.ShapeDtypeStruct((b_, L, HY_WIDTH), BF16),
                   jax.ShapeDtypeStruct((b_, HY_WIDTH, L), F32)],
        compiler_params=_cparams(("arbitrary", "arbitrary")),
        name="seqlocal",
    )(proj, proj, proj, pool_w, pool_scale, conv_w, conv_b)


def _filter_kernel(z_ref, w1_ref, b1_ref, f1_ref, w2_ref, b2_ref, f2_ref, w3_ref, dec_ref,
                   filt_ref, l1_ref, *, tr, L):
    i = pl.program_id(0)
    z = z_ref[...]
    hid = jnp.sin(f1_ref[...] * (_dot3(z, w1_ref[...]) + b1_ref[...]))
    hid = jnp.sin(f2_ref[...] * (_dot3(hid, w2_ref[...]) + b2_ref[...]))
    h = _dot3(hid, w3_ref[0])
    h = h * jnp.exp(-z[:, 0:1] * jnp.abs(dec_ref[0]))
    row = i * tr + lax.broadcasted_iota(jnp.int32, h.shape, 0)
    h = jnp.where(row == L, 0.0, h)
    filt_ref[...] = h

    @pl.when(i == 0)
    def _():
        l1_ref[...] = jnp.zeros_like(l1_ref)

    l1_ref[...] += jnp.sum(jnp.abs(h), axis=0, keepdims=True)


def _hyena_filter(zfull, w1, b1, f1, w2, b2, f2, w3, dec, L, tr):
    n = 2 * L
    nhalf = L // tr
    full = lambda i: (0, 0)
    return pl.pallas_call(
        functools.partial(_filter_kernel, tr=tr, L=L),
        grid=(n // tr,),
        in_specs=[pl.BlockSpec((tr, LANE), lambda i: (i, 0)),
                  pl.BlockSpec((LANE, LANE), full), pl.BlockSpec((1, LANE), full), pl.BlockSpec((1, LANE), full),
                  pl.BlockSpec((LANE, LANE), full), pl.BlockSpec((1, LANE), full), pl.BlockSpec((1, LANE), full),
                  pl.BlockSpec((1, LANE, HY_WIDTH), lambda i: (i // nhalf, 0, 0)),
                  pl.BlockSpec((1, 1, HY_WIDTH), lambda i: (i // nhalf, 0, 0))],
        out_specs=[pl.BlockSpec((tr, HY_WIDTH), lambda i: (i, 0)),
                   pl.BlockSpec((1, HY_WIDTH), full)],
        out_shape=[jax.ShapeDtypeStruct((n, HY_WIDTH), F32),
                   jax.ShapeDtypeStruct((1, HY_WIDTH), F32)],
        compiler_params=_cparams(("arbitrary",)),
        name="hyena_filter",
    )(zfull, w1, b1, f1, w2, b2, f2, w3, dec)


def _filt_fin_kernel(f_ref, l1_ref, o_ref):
    o_ref[...] = (f_ref[...] / l1_ref[...]).T


def _filter_finalize(filt, l1, tr):
    n = filt.shape[0]
    return pl.pallas_call(
        _filt_fin_kernel,
        grid=(n // tr,),
        in_specs=[pl.BlockSpec((tr, HY_WIDTH), lambda i: (i, 0)),
                  pl.BlockSpec((1, HY_WIDTH), lambda i: (0, 0))],
        out_specs=pl.BlockSpec((HY_WIDTH, tr), lambda i: (0, i)),
        out_shape=jax.ShapeDtypeStruct((HY_WIDTH, n), F32),
        compiler_params=_cparams(("arbitrary",)),
        name="filter_finalize",
    )(filt, l1)


FFT_N1 = 128
FFT_N2 = 128
CONV_CB = 8


def _lconv_kernel(bias_ref, z_ref, f_ref, f1d_h, f1d_l, f1f_h, f1f_l, tr_ref, ti_ref,
                  f2_h, f2_l, g2_h, g2_l, e1_h, e1_l, y_ref, bs_ref, bf_ref, cm_ref):
    blk = pl.program_id(0)
    tw_r = tr_ref[...]
    tw_i = ti_ref[...]
    n2 = FFT_N2
    for c in range(CONV_CB):
        rows = slice(c * FFT_N1, (c + 1) * FFT_N1)
        zc = jnp.concatenate([z_ref[0, c], z_ref[1, c]], axis=0)
        a = _cdot3(f1d_h[...], f1d_l[...], zc)
        ar, ai = a[:FFT_N1], a[FFT_N1:]
        bs_ref[rows, :n2] = ar * tw_r - ai * tw_i
        bs_ref[rows, n2:] = ar * tw_i + ai * tw_r
        a = _cdot3(f1f_h[...], f1f_l[...], f_ref[c])
        ar, ai = a[:FFT_N1], a[FFT_N1:]
        bf_ref[rows, :n2] = ar * tw_r - ai * tw_i
        bf_ref[rows, n2:] = ar * tw_i + ai * tw_r
    x = _dot3c(bs_ref[...], f2_h[...], f2_l[...])
    hf = _dot3c(bf_ref[...], f2_h[...], f2_l[...])
    xr, xi = x[:, :n2], x[:, n2:]
    hr, hi = hf[:, :n2], hf[:, n2:]
    y = jnp.concatenate([xr * hr - xi * hi, xr * hi + xi * hr], axis=1)
    cm_ref[...] = _dot3c(y, g2_h[...], g2_l[...])
    half = FFT_N1 // 2
    for c in range(CONV_CB):
        rows = slice(c * FFT_N1, (c + 1) * FFT_N1)
        cr = cm_ref[rows, :n2]
        ci = cm_ref[rows, n2:]
        dc = jnp.concatenate([cr * tw_r + ci * tw_i, ci * tw_r - cr * tw_i], axis=0)
        yc = _cdot3(e1_h[...], e1_l[...], dc)
        bias = bias_ref[blk * CONV_CB + c]
        y_ref[0, c] = yc[:half] + z_ref[0, c] * bias
        y_ref[1, c] = yc[half:] + z_ref[1, c] * bias


def _lconv_consts():
    n1, n2 = FFT_N1, FFT_N2
    n = n1 * n2
    idx = np.arange(128)
    ang1 = 2.0 * np.pi * ((idx[:, None] * idx[None, :]) % 128) / 128.0
    fr, fi = np.cos(ang1), -np.sin(ang1)
    angt = 2.0 * np.pi * (idx[:, None] * idx[None, :]) / n
    tw_r, tw_i = np.cos(angt), -np.sin(angt)
    h = n1 // 2
    f1d = np.block([[fr[:, :h], -fi[:, :h]], [fi[:, :h], fr[:, :h]]])
    f1f = np.concatenate([fr, fi], axis=0)
    f2 = np.block([[fr, fi], [-fi, fr]])
    g2 = np.block([[fr, -fi], [fi, fr]])
    er, ei = fr[:h] / n, -fi[:h] / n
    e1 = np.block([[er, -ei], [ei, er]])

    def hl(m):
        m32 = jnp.asarray(m, F32)
        hi = m32.astype(BF16)
        lo = (m32 - hi.astype(F32)).astype(BF16)
        return hi, lo

    return (*hl(f1d), *hl(f1f), jnp.asarray(tw_r, F32), jnp.asarray(tw_i, F32), *hl(f2), *hl(g2), *hl(e1))


def _long_conv_lat(zvt, filt_t, bias):
    b_, C, L = zvt.shape
    assert b_ == 2 and L == FFT_N1 * FFT_N2 // 2
    z4 = zvt.reshape(b_, C, FFT_N1 // 2, FFT_N2)
    f3 = filt_t.reshape(C, FFT_N1, FFT_N2)
    consts = _lconv_consts()
    cb = CONV_CB

    def cspec(a):
        return pl.BlockSpec(a.shape, lambda i: (0,) * a.ndim)

    y = pl.pallas_call(
        _lconv_kernel,
        grid=(C // cb,),
        in_specs=[pl.BlockSpec(memory_space=pltpu.SMEM),
                  pl.BlockSpec((b_, cb, FFT_N1 // 2, FFT_N2), lambda i: (0, i, 0, 0)),
                  pl.BlockSpec((cb, FFT_N1, FFT_N2), lambda i: (i, 0, 0))] + [cspec(a) for a in consts],
        out_specs=pl.BlockSpec((b_, cb, FFT_N1 // 2, FFT_N2), lambda i: (0, i, 0, 0)),
        out_shape=jax.ShapeDtypeStruct(z4.shape, F32),
        scratch_shapes=[pltpu.VMEM((cb * FFT_N1, 2 * FFT_N2), F32),
                        pltpu.VMEM((cb * FFT_N1, 2 * FFT_N2), F32),
                        pltpu.VMEM((cb * FFT_N1, 2 * FFT_N2), F32)],
        compiler_params=_cparams(("arbitrary",)),
        name="long_conv",
    )(bias, z4, f3, *consts)
    return y.reshape(b_, C, L)


def _cconv_kernel(z_ref, f_ref, bias_ref, fd_h, fd_l, ff_h, ff_l, fi_h, fi_l, y_ref, *, L):
    zr, zi = z_ref[0], z_ref[1]
    x = _dot3c(jnp.concatenate([zr, zi], axis=1), fd_h[...], fd_l[...])
    hf = _dot3c(f_ref[...], ff_h[...], ff_l[...])
    n = 2 * L
    xr, xi = x[:, :n], x[:, n:]
    hr, hi = hf[:, :n], hf[:, n:]
    y = jnp.concatenate([xr * hr - xi * hi, xr * hi + xi * hr], axis=1)
    out = _dot3c(y, fi_h[...], fi_l[...])
    bias = bias_ref[...]
    y_ref[0] = out[:, :L] + zr * bias
    y_ref[1] = out[:, L:] + zi * bias


def _cconv_consts(L):
    n = 2 * L
    idx = np.arange(n)
    ang = 2.0 * np.pi * ((idx[:, None] * idx[None, :]) % n) / n
    fr, fi = np.cos(ang), -np.sin(ang)
    fd = np.block([[fr[:L], fi[:L]], [-fi[:L], fr[:L]]])
    ff = np.concatenate([fr, fi], axis=1)
    er, ei = fr[:, :L] / n, -fi[:, :L] / n
    finv = np.block([[er, ei], [-ei, er]])

    def hl(m):
        m32 = jnp.asarray(m, F32)
        hi = m32.astype(BF16)
        lo = (m32 - hi.astype(F32)).astype(BF16)
        return hi, lo

    return (*hl(fd), *hl(ff), *hl(finv))


def _long_conv_ctx(zvt, filt_t, bias):
    b_, C, L = zvt.shape
    consts = _cconv_consts(L)
    bias_b = jnp.broadcast_to(bias[:, None], (C, L))

    def cspec(a):
        return pl.BlockSpec(a.shape, lambda i: (0,) * a.ndim)

    return pl.pallas_call(
        functools.partial(_cconv_kernel, L=L),
        grid=(1,),
        in_specs=[cspec(zvt), cspec(filt_t), cspec(bias_b)] + [cspec(a) for a in consts],
        out_specs=cspec(zvt),
        out_shape=jax.ShapeDtypeStruct(zvt.shape, F32),
        compiler_params=_cparams(("arbitrary",)),
        name="long_conv_ctx",
    )(zvt, filt_t, bias_b, *consts)


def _rms(x, g):
    ms = jnp.mean(x * x, axis=-1, keepdims=True)
    return x * lax.rsqrt(ms + EPS) * g


def _qkv_kernel(p_ref, gq_ref, gkv_ref, wq_ref, wqr_ref, wk_ref, wv_ref,
                cq_ref, sq_ref, ck_ref, sk_ref, q_ref, k_ref, v_ref):
    blk = p_ref[0]
    cq = blk[:, :Q_LORA].astype(F32)
    ckv = blk[:, Q_LORA:Q_LORA + KV_LORA].astype(F32)
    o = Q_LORA + KV_LORA
    kpe = blk[:, o:o + LANE].astype(F32)
    kpe_rot = blk[:, o + LANE:o + 2 * LANE].astype(F32)
    qn = _rms(cq, gq_ref[...]).astype(BF16)
    q = _mm(qn, wq_ref[...])
    qr = _mm(qn, wqr_ref[...])
    kvn = _rms(ckv, gkv_ref[...]).astype(BF16)
    k = _mm(kvn, wk_ref[...])
    v_ref[0] = _mm(kvn, wv_ref[...]).astype(v_ref.dtype)
    cq_t, sq_t = cq_ref[...], sq_ref[...]
    kpe_r = kpe * ck_ref[...] + kpe_rot * sk_ref[...]
    for h in range(MLA_HEADS):
        sl = slice(h * HEAD_PAD, (h + 1) * HEAD_PAD)
        q_ref[0, h] = (q[:, sl] * cq_t + qr[:, sl] * sq_t).astype(q_ref.dtype)
        k_ref[0, h] = (k[:, sl] + kpe_r).astype(k_ref.dtype)


def _qkv(proj, gq, gkv, wq, wqr, wk, wv, cq, sq, ck, sk, tt):
    b_, L, _ = proj.shape
    full = lambda b, i: (0, 0)
    hw = MLA_HEADS * HEAD_PAD
    vw = MLA_HEADS * V_HEAD
    return pl.pallas_call(
        _qkv_kernel,
        grid=(b_, L // tt),
        in_specs=[pl.BlockSpec((1, tt, 1024), lambda b, i: (b, i, COL_QKV // 1024)),
                  pl.BlockSpec((1, Q_LORA), full), pl.BlockSpec((1, KV_LORA), full),
                  pl.BlockSpec((Q_LORA, hw), full), pl.BlockSpec((Q_LORA, hw), full),
                  pl.BlockSpec((KV_LORA, hw), full), pl.BlockSpec((KV_LORA, vw), full),
                  pl.BlockSpec((tt, LANE), lambda b, i: (i, 0)), pl.BlockSpec((tt, LANE), lambda b, i: (i, 0)),
                  pl.BlockSpec((tt, LANE), lambda b, i: (i, 0)), pl.BlockSpec((tt, LANE), lambda b, i: (i, 0))],
        out_specs=[pl.BlockSpec((1, MLA_HEADS, tt, HEAD_PAD), lambda b, i: (b, 0, i, 0)),
                   pl.BlockSpec((1, MLA_HEADS, tt, HEAD_PAD), lambda b, i: (b, 0, i, 0)),
                   pl.BlockSpec((1, tt, vw), lambda b, i: (b, i, 0))],
        out_shape=[jax.ShapeDtypeStruct((b_, MLA_HEADS, L, HEAD_PAD), BF16),
                   jax.ShapeDtypeStruct((b_, MLA_HEADS, L, HEAD_PAD), BF16),
                   jax.ShapeDtypeStruct((b_, L, vw), BF16)],
        compiler_params=_cparams(("arbitrary", "arbitrary")),
        name="qkv",
    )(proj, gq, gkv, wq, wqr, wk, wv, cq, sq, ck, sk)


def _attn_kernel(q_ref, k_ref, v_ref, o_ref, *, tk, nk):
    tq = q_ref.shape[2]
    outs = []
    for hh in range(2):
        q = q_ref[0, hh]

        def body(j, carry):
            m, l, acc = carry
            start = pl.multiple_of(j * tk, tk)
            k = k_ref[0, hh, pl.ds(start, tk), :]
            v = v_ref[0, pl.ds(start, tk), :]
            s = lax.dot_general(q, k, (((1,), (1,)), ((), ())), preferred_element_type=F32)
            m_new = jnp.maximum(m, jnp.max(s, axis=-1, keepdims=True))
            p = jnp.exp(s - m_new)
            alpha = jnp.exp(m - m_new)
            l = alpha * l + jnp.sum(p, axis=-1, keepdims=True)
            acc = alpha * acc + _mm(p.astype(BF16), v)
            return m_new, l, acc

        init = (jnp.full((tq, 1), -1e30, F32), jnp.zeros((tq, 1), F32), jnp.zeros((tq, LANE), F32))
        m, l, acc = lax.fori_loop(0, nk, body, init)
        outs.append(acc / l)
    lane = lax.broadcasted_iota(jnp.int32, (tq, LANE), 1)
    o_ref[0] = jnp.where(lane < V_HEAD, outs[0], outs[1]).astype(o_ref.dtype)


def _attention(q, k, v, tq, tk):
    b_, H, L, _ = q.shape
    S = k.shape[2]
    return pl.pallas_call(
        functools.partial(_attn_kernel, tk=tk, nk=S // tk),
        grid=(b_, H // 2, L // tq),
        in_specs=[pl.BlockSpec((1, 2, tq, HEAD_PAD), lambda b, hp, i: (b, hp, i, 0)),
                  pl.BlockSpec((1, 2, S, HEAD_PAD), lambda b, hp, i: (b, hp, 0, 0)),
                  pl.BlockSpec((1, S, LANE), lambda b, hp, i: (b, 0, hp))],
        out_specs=pl.BlockSpec((1, tq, LANE), lambda b, hp, i: (b, i, hp)),
        out_shape=jax.ShapeDtypeStruct((b_, L, H * V_HEAD), BF16),
        compiler_params=_cparams(("arbitrary", "arbitrary", "arbitrary")),
        name="attention",
    )(q, k, v)


def _mixout_kernel(a_ref, x0_ref, yt_ref, o_ref, ga_ref, gb_ref, gc_ref, h_ref, g1_ref,
                   wp_ref, wh_ref, wo_ref, wout_ref, out_ref):
    hy = (x0_ref[0].astype(F32) * yt_ref[0].T).astype(BF16)
    a = _mm(a_ref[0], wp_ref[...])
    b = _mm(hy, wh_ref[...])
    cm = _mm(o_ref[0], wo_ref[...])
    m = (jax.nn.sigmoid(ga_ref[0].astype(F32)) * a + jax.nn.sigmoid(gb_ref[0].astype(F32)) * b
         + jax.nn.sigmoid(gc_ref[0].astype(F32)) * cm)
    out_ref[0] = h_ref[0] + g1_ref[0] * _mm(m.astype(BF16), wout_ref[...])


def _mixout(a_pre, x0c, yt, o, proj, h, mod, wp, wh, wo, wout, tt):
    b_, L, _ = h.shape
    full = lambda b, i: (0, 0)
    row = lambda b, i: (b, i, 0)
    gcol = COL_GATE // D_MODEL
    return pl.pallas_call(
        _mixout_kernel,
        grid=(b_, L // tt),
        in_specs=[pl.BlockSpec((1, tt, POOL_WIDTH), row),
                  pl.BlockSpec((1, tt, HY_WIDTH), row),
                  pl.BlockSpec((1, HY_WIDTH, tt), lambda b, i: (b, 0, i)),
                  pl.BlockSpec((1, tt, MLA_HEADS * V_HEAD), row),
                  pl.BlockSpec((1, tt, D_MODEL), lambda b, i: (b, i, gcol)),
                  pl.BlockSpec((1, tt, D_MODEL), lambda b, i: (b, i, gcol + 1)),
                  pl.BlockSpec((1, tt, D_MODEL), lambda b, i: (b, i, gcol + 2)),
                  pl.BlockSpec((1, tt, D_MODEL), row),
                  pl.BlockSpec((1, 1, D_MODEL), lambda b, i: (b, 0, 2)),
                  pl.BlockSpec((POOL_WIDTH, D_MODEL), full), pl.BlockSpec((HY_WIDTH, D_MODEL), full),
                  pl.BlockSpec((MLA_HEADS * V_HEAD, D_MODEL), full), pl.BlockSpec((D_MODEL, D_MODEL), full)],
        out_specs=pl.BlockSpec((1, tt, D_MODEL), row),
        out_shape=jax.ShapeDtypeStruct(h.shape, F32),
        compiler_params=_cparams(("arbitrary", "arbitrary")),
        name="mixout",
    )(a_pre, x0c, yt, o, proj, proj, proj, h, mod, wp, wh, wo, wout)


def _ffn_kernel(x_ref, sh_ref, sc_ref, g2_ref, gn_ref, w1_ref, w2_ref, fg_ref, o_ref, xn_ref, acc_ref, *, final):
    k = pl.program_id(2)
    nk = pl.num_programs(2)

    @pl.when(k == 0)
    def _():
        xn_ref[...] = _mod_norm(x_ref[0], gn_ref[...], sc_ref[0], sh_ref[0]).astype(BF16)
        acc_ref[...] = jnp.zeros_like(acc_ref)

    u = jnp.maximum(_mm(xn_ref[...], w1_ref[...]), 0.0)
    acc_ref[...] += _mm((u * u).astype(BF16), w2_ref[...])

    @pl.when(k == nk - 1)
    def _():
        y = x_ref[0] + g2_ref[0] * acc_ref[...]
        if final:
            y = _rms(y, fg_ref[...])
        o_ref[0] = y


def _ffn(h, mod, gn, w1, w2, final_g, tt, final):
    b_, L, _ = h.shape
    tf = D_FF // 2
    full = lambda b, i, k: (0, 0)
    row = lambda b, i, k: (b, i, 0)
    return pl.pallas_call(
        functools.partial(_ffn_kernel, final=final),
        grid=(b_, L // tt, D_FF // tf),
        in_specs=[pl.BlockSpec((1, tt, D_MODEL), row),
                  pl.BlockSpec((1, 1, D_MODEL), lambda b, i, k: (b, 0, 3)),
                  pl.BlockSpec((1, 1, D_MODEL), lambda b, i, k: (b, 0, 4)),
                  pl.BlockSpec((1, 1, D_MODEL), lambda b, i, k: (b, 0, 5)),
                  pl.BlockSpec((1, D_MODEL), full),
                  pl.BlockSpec((D_MODEL, tf), lambda b, i, k: (0, k)),
                  pl.BlockSpec((tf, D_MODEL), lambda b, i, k: (k, 0)),
                  pl.BlockSpec((1, D_MODEL), full)],
        out_specs=pl.BlockSpec((1, tt, D_MODEL), row),
        out_shape=jax.ShapeDtypeStruct(h.shape, F32),
        scratch_shapes=[pltpu.VMEM((tt, D_MODEL), BF16), pltpu.VMEM((tt, D_MODEL), F32)],
        compiler_params=_cparams(("arbitrary", "arbitrary", "arbitrary")),
        name="ffn",
    )(h, mod, mod, mod, gn, w1, w2, final_g)


def _rope_tables(n_rows):
    rows = jnp.repeat(jnp.arange(n_rows, dtype=F32), GRID_W)
    cols = jnp.tile(jnp.arange(GRID_W, dtype=F32), n_rows)
    half = QK_ROPE // 2
    inv = ROPE_BASE ** (-jnp.arange(0, half, 2, dtype=F32) / half)
    ar = rows[:, None] * inv
    ac = cols[:, None] * inv
    ang = jnp.concatenate([ar, ar, ac, ac], axis=-1)
    return jnp.cos(ang), jnp.sin(ang)


def _head_tables(cos, sin, L):
    ones = jnp.ones((L, QK_NOPE), F32)
    z64 = jnp.zeros((L, QK_NOPE), F32)
    z32 = jnp.zeros((L, HEAD_PAD - QK_DIM), F32)
    if cos is None:
        cos = jnp.ones((L, QK_ROPE), F32)
        sin = jnp.zeros((L, QK_ROPE), F32)
    cq = jnp.concatenate([ones, cos, z32], axis=1) * MLA_SCALE
    sq = jnp.concatenate([z64, sin, z32], axis=1) * MLA_SCALE
    ck = jnp.concatenate([z64, cos, z32], axis=1)
    sk = jnp.concatenate([z64, sin, z32], axis=1)
    return cq, sq, ck, sk


def _hy_embed(L):
    t = jnp.linspace(0.0, 1.0, L, dtype=F32)[:, None]
    omega = 2.0 * np.pi * jnp.arange(L, dtype=F32)[:, None] / L
    bands = jnp.linspace(1e-4, HY_BANDS - 1, HY_BANDS, dtype=F32)[None, :]
    z = jnp.concatenate([t, jnp.cos(omega * bands), -jnp.sin(omega * bands)], axis=-1)
    zfull = jnp.concatenate([z, jnp.zeros((1, HY_EMB), F32), z[:0:-1]], axis=0)
    return jnp.pad(zfull, ((0, 0), (0, LANE - HY_EMB)))


def _pack_w_in(w):
    z = lambda n: jnp.zeros((D_MODEL, n), F32)
    o_q = 512 + 3 * HY_WIDTH
    o_kv = o_q + Q_LORA
    o_pe = o_kv + KV_LORA
    o_gate = o_pe + QK_ROPE
    kpe = w[:, o_pe:o_gate]
    return jnp.concatenate([
        w[:, :o_pe],
        z(QK_NOPE), kpe, z(HEAD_PAD - QK_DIM),
        z(QK_NOPE), _rot_half(kpe), z(HEAD_PAD - QK_DIM),
        z(LANE),
        w[:, o_gate:]], axis=1).astype(BF16)


def _pack_heads(w_uq, w_ukv):
    wq3 = w_uq.reshape(Q_LORA, MLA_HEADS, QK_DIM)
    pad = jnp.zeros((Q_LORA, MLA_HEADS, HEAD_PAD - QK_DIM), F32)
    wq = jnp.concatenate([wq3, pad], axis=-1).reshape(Q_LORA, -1)
    wqr = jnp.concatenate([jnp.zeros((Q_LORA, MLA_HEADS, QK_NOPE), F32), _rot_half(wq3[..., QK_NOPE:]), pad],
                          axis=-1).reshape(Q_LORA, -1)
    wkv3 = w_ukv.reshape(KV_LORA, MLA_HEADS, QK_NOPE + V_HEAD)
    wk = jnp.concatenate([wkv3[..., :QK_NOPE], jnp.zeros((KV_LORA, MLA_HEADS, HEAD_PAD - QK_NOPE), F32)],
                         axis=-1).reshape(KV_LORA, -1)
    wv = wkv3[..., QK_NOPE:].reshape(KV_LORA, -1)
    return wq.astype(BF16), wqr.astype(BF16), wk.astype(BF16), wv.astype(BF16)


def _pad2(a, r, c):
    return jnp.pad(a, ((0, r - a.shape[0]), (0, c - a.shape[1])))


TT_LAT = 512
TQ = 256
TK = 768


def kernel(x, c, ctx, c_ctx, w_mod, b_mod, norm1_g, norm2_g, w_in, pool_w, pool_scale, pool_out, hy_conv_w, hy_conv_b, hy_f_w1, hy_f_b1, hy_f_freq1, hy_f_w2, hy_f_b2, hy_f_freq2, hy_f_w3, hy_decay, hy_bias, hy_out, q_norm_g, w_uq, kv_norm_g, w_ukv, w_o, w_out, w_ff1, w_ff2, final_g):
    B, L, _ = x.shape
    Lc = ctx.shape[1]
    cos, sin = _rope_tables(L // GRID_W)
    tabs_lat = _head_tables(cos, sin, L)
    tabs_ctx = _head_tables(None, None, Lc)
    z_lat = _hy_embed(L)
    z_ctx = _hy_embed(Lc)

    c_all = jnp.concatenate([c, c_ctx[None], jnp.zeros((8 - B - 1, D_MODEL), F32)], axis=0)
    mod_all = _modulation(c_all, w_mod, b_mod)

    h, hc = x, ctx
    fg = final_g[None]
    for l in range(DEPTH):
        last = l == DEPTH - 1
        mod_lat = mod_all[l, :B][:, None, :]
        mod_ctx = jnp.broadcast_to(mod_all[l, B][None, None, :], (B, 1, N_MOD * D_MODEL))
        g1n, g2n = norm1_g[l][None], norm2_g[l][None]
        w_in_p = _pack_w_in(w_in[l])
        wq, wqr, wk, wv = _pack_heads(w_uq[l], w_ukv[l])
        pw = pool_w[l].astype(BF16)
        ps = pool_scale[l][None]
        cw, cb = hy_conv_w[l], hy_conv_b[l][None]
        wp, wh, wo, wout = (pool_out[l].astype(BF16), hy_out[l].astype(BF16), w_o[l].astype(BF16),
                            w_out[l].astype(BF16))
        wf1, wf2 = w_ff1[l].astype(BF16), w_ff2[l].astype(BF16)
        gq, gkv = q_norm_g[l][None], kv_norm_g[l][None]
        fw1 = _pad2(hy_f_w1[l], LANE, LANE)
        fb1 = _pad2(hy_f_b1[l][None], 1, LANE)
        ff1 = _pad2(hy_f_freq1[l][None], 1, LANE)
        fw2 = _pad2(hy_f_w2[l], LANE, LANE)
        fb2 = _pad2(hy_f_b2[l][None], 1, LANE)
        ff2 = _pad2(hy_f_freq2[l][None], 1, LANE)
        fw3 = jnp.pad(hy_f_w3[l].reshape(-1, 2, HY_WIDTH).transpose(1, 0, 2),
                      ((0, 0), (0, LANE - hy_f_w3.shape[1]), (0, 0)))
        dec = hy_decay[l][:, None, :]

        def stream_pre(hs, mod, tabs, tt):
            proj = _inproj(hs, mod, g1n, w_in_p, tt)
            q, k, v = _qkv(proj, gq, gkv, wq, wqr, wk, wv, *tabs, tt)
            return proj, q, k, v

        def stream_mix(hs, mod, proj, q, k, v, zemb, tt, tq, tk, tr):
            Ls = hs.shape[1]
            a_pre, x0c, zvt = _seqlocal(proj, pw, ps, cw, cb, tt)
            filt, l1 = _hyena_filter(zemb, fw1, fb1, ff1, fw2, fb2, ff2, fw3, dec, Ls, tr)
            filt_t = _filter_finalize(filt, l1, tr)
            if Ls == FFT_N1 * FFT_N2 // 2:
                yt = _long_conv_lat(zvt, filt_t, hy_bias[l])
            else:
                yt = _long_conv_ctx(zvt, filt_t, hy_bias[l])
            o = _attention(q, k, v, tq, tk)
            return _mixout(a_pre, x0c, yt, o, proj, hs, mod, wp, wh, wo, wout, tt)

        proj_c, q_c, k_c, v_c = stream_pre(hc, mod_ctx, tabs_ctx, Lc)
        proj_l, q_l, k_l, v_l = stream_pre(h, mod_lat, tabs_lat, TT_LAT)
        k_all = jnp.concatenate([k_c, k_l], axis=2)
        v_all = jnp.concatenate([v_c, v_l], axis=1)
        h = stream_mix(h, mod_lat, proj_l, q_l, k_all, v_all, z_lat, TT_LAT, TQ, TK, 1024)
        h = _ffn(h, mod_lat, g2n, wf1, wf2, fg, TT_LAT, last)
        if not last:
            hc = stream_mix(hc, mod_ctx, proj_c, q_c, k_c, v_c, z_ctx, Lc, Lc, Lc, Lc)
            hc = _ffn(hc, mod_ctx, g2n, wf1, wf2, fg, Lc, False)
    return h
```

```python
import functools
import math

import jax
import jax.numpy as jnp
import numpy as np
from jax import lax
from jax.experimental import pallas as pl
from jax.experimental.pallas import tpu as pltpu

F32 = jnp.float32
BF16 = jnp.bfloat16

D_MODEL = 1024
DEPTH = 4
GRID_W = 64
EPS = 1e-6

POOL_WIDTH = 512
POOL_WINDOWS = (2, 4, 8, 16)
POOL_GROUP = 128
HY_WIDTH = 512
HY_EMB = 33
HY_BANDS = 16
MLA_HEADS = 8
QK_NOPE = 64
QK_ROPE = 32
QK_DIM = 96
V_HEAD = 64
Q_LORA = 384
KV_LORA = 256
MLA_SCALE = QK_DIM ** -0.5
LOG2E = math.log2(math.e)
ROPE_BASE = 10000.0
D_FF = 4 * D_MODEL
N_MOD = 6

LANE = 128
HALO = 16
HEAD_PAD = 128

COL_POOL = 0
COL_HY = 512
COL_QKV = 2048
COL_GATE = 3072
N_PROJ = 6144

VMEM_LIMIT = 50 * 1024 * 1024


def _cparams(sem):
    return pltpu.CompilerParams(dimension_semantics=sem, vmem_limit_bytes=VMEM_LIMIT)


def _mm(a, b):
    return jnp.dot(a, b, preferred_element_type=F32)


def _split(x):
    hi = x.astype(BF16)
    lo = (x - hi.astype(F32)).astype(BF16)
    return hi, lo


def _dot3(a, b):
    ah, al = _split(a)
    bh, bl = _split(b)
    return _mm(ah, bh) + _mm(al, bh) + _mm(ah, bl)


def _dot3c(a, bh, bl):
    ah, al = _split(a)
    return _mm(ah, bh) + _mm(al, bh) + _mm(ah, bl)


def _cdot3(ch, cl, b):
    bh, bl = _split(b)
    n = b.shape[1]
    r = _mm(ch, jnp.concatenate([bh, bl], axis=1))
    return r[:, :n] + r[:, n:] + _mm(cl, bh)


def _rot_half(x):
    q = QK_ROPE // 4
    return jnp.concatenate([-x[..., q:2 * q], x[..., :q], -x[..., 3 * q:], x[..., 2 * q:3 * q]], axis=-1)


def _mod_kernel(c_ref, w_ref, b_ref, o_ref):
    c = c_ref[...]
    s = c * jax.nn.sigmoid(c)
    o_ref[0] = _dot3(s, w_ref[0]) + b_ref[0]


def _modulation(c_all, w_mod, b_mod):
    tn = 1536
    n = N_MOD * D_MODEL
    return pl.pallas_call(
        _mod_kernel,
        grid=(DEPTH, n // tn),
        in_specs=[pl.BlockSpec((8, D_MODEL), lambda l, j: (0, 0)),
                  pl.BlockSpec((1, D_MODEL, tn), lambda l, j: (l, 0, j)),
                  pl.BlockSpec((1, 1, tn), lambda l, j: (l, 0, j))],
        out_specs=pl.BlockSpec((1, 8, tn), lambda l, j: (l, 0, j)),
        out_shape=jax.ShapeDtypeStruct((DEPTH, 8, n), F32),
        compiler_params=_cparams(("arbitrary", "arbitrary")),
        name="modulation",
    )(c_all, w_mod, b_mod.reshape(DEPTH, 1, n))


def _mod_norm(x, g, sc, sh):
    ms = jnp.mean(x * x, axis=-1, keepdims=True)
    return (x * lax.rsqrt(ms + EPS) * g) * (1.0 + sc) + sh


def _inproj_kernel(x_ref, sh_ref, sc_ref, g_ref, w_ref, o_ref):
    xn = _mod_norm(x_ref[0], g_ref[...], sc_ref[0], sh_ref[0])
    o_ref[0] = _mm(xn.astype(BF16), w_ref[...]).astype(o_ref.dtype)


def _inproj(h, mod, g, w, tt):
    b_, L, _ = h.shape
    tn = N_PROJ // 2
    return pl.pallas_call(
        _inproj_kernel,
        grid=(N_PROJ // tn, b_, L // tt),
        in_specs=[pl.BlockSpec((1, tt, D_MODEL), lambda j, b, i: (b, i, 0)),
                  pl.BlockSpec((1, 1, D_MODEL), lambda j, b, i: (b, 0, 0)),
                  pl.BlockSpec((1, 1, D_MODEL), lambda j, b, i: (b, 0, 1)),
                  pl.BlockSpec((1, D_MODEL), lambda j, b, i: (0, 0)),
                  pl.BlockSpec((D_MODEL, tn), lambda j, b, i: (0, j))],
        out_specs=pl.BlockSpec((1, tt, tn), lambda j, b, i: (b, i, j)),
        out_shape=jax.ShapeDtypeStruct((b_, L, N_PROJ), BF16),
        compiler_params=_cparams(("arbitrary", "arbitrary", "arbitrary")),
        name="inproj",
    )(h, mod, mod, g, w)


def _seqlocal_kernel(cur_ref, prev_ref, next_ref, pw_ref, ps_ref, cw_ref, cb_ref,
                     a_ref, x0_ref, zvt_ref, *, tt, L):
    i = pl.program_id(1)
    nt = pl.num_programs(1)
    keep_prev = (i > 0).astype(F32)
    keep_next = (i < nt - 1).astype(F32)
    n = tt + 2 * HALO

    def ext(lo, hi):
        return jnp.concatenate([prev_ref[0, :, lo:hi].astype(F32) * keep_prev,
                                cur_ref[0, :, lo:hi].astype(F32),
                                next_ref[0, :, lo:hi].astype(F32) * keep_next], axis=0)

    def roll(x, s):
        return pltpu.roll(x, s % n, 0)

    t = i * tt + lax.broadcasted_iota(jnp.int32, (tt, POOL_GROUP), 0)
    for gi, w in enumerate(POOL_WINDOWS):
        lo = COL_POOL + gi * POOL_GROUP
        xg = ext(lo, lo + POOL_GROUP)
        s = xg + roll(xg, 1)
        if w >= 4:
            s = roll(s, 1) + roll(s, -1)
        if w >= 8:
            s = roll(s, 2) + roll(s, -2)
        if w >= 16:
            s = roll(s, 4) + roll(s, -4)
        cnt = (jnp.minimum(t + w // 2, L) - jnp.maximum(t - w // 2, 0)).astype(F32)
        u = s[HALO:HALO + tt] / cnt - xg[HALO:HALO + tt]
        ug = _mm(u.astype(BF16), pw_ref[gi]) * ps_ref[:, gi * POOL_GROUP:(gi + 1) * POOL_GROUP]
        a_ref[0, :, gi * POOL_GROUP:(gi + 1) * POOL_GROUP] = ug.astype(a_ref.dtype)

    def conv(part, j):
        c0 = part * HY_WIDTH + j * LANE
        xe = ext(COL_HY + c0, COL_HY + c0 + LANE)
        y = (roll(xe, 1) * cw_ref[0:1, c0:c0 + LANE] + xe * cw_ref[1:2, c0:c0 + LANE]
             + roll(xe, -1) * cw_ref[2:3, c0:c0 + LANE] + cb_ref[:, c0:c0 + LANE])
        return y[HALO:HALO + tt]

    for j in range(HY_WIDTH // LANE):
        x0_ref[0, :, j * LANE:(j + 1) * LANE] = conv(0, j).astype(x0_ref.dtype)
        zv = conv(2, j) * conv(1, j)
        zvt_ref[0, j * LANE:(j + 1) * LANE, :] = zv.T


def _seqlocal(proj, pool_w, pool_scale, conv_w, conv_b, tt):
    b_, L, _ = proj.shape
    hb = tt // HALO
    nh = L // HALO
    wide = COL_QKV
    return pl.pallas_call(
        functools.partial(_seqlocal_kernel, tt=tt, L=L),
        grid=(b_, L // tt),
        in_specs=[pl.BlockSpec((1, tt, wide), lambda b, i: (b, i, 0)),
                  pl.BlockSpec((1, HALO, wide), lambda b, i: (b, jnp.maximum(i * hb - 1, 0), 0)),
                  pl.BlockSpec((1, HALO, wide), lambda b, i: (b, jnp.minimum((i + 1) * hb, nh - 1), 0)),
                  pl.BlockSpec((len(POOL_WINDOWS), POOL_GROUP, POOL_GROUP), lambda b, i: (0, 0, 0)),
                  pl.BlockSpec((1, POOL_WIDTH), lambda b, i: (0, 0)),
                  pl.BlockSpec((3, 3 * HY_WIDTH), lambda b, i: (0, 0)),
                  pl.BlockSpec((1, 3 * HY_WIDTH), lambda b, i: (0, 0))],
        out_specs=[pl.BlockSpec((1, tt, POOL_WIDTH), lambda b, i: (b, i, 0)),
                   pl.BlockSpec((1, tt, HY_WIDTH), lambda b, i: (b, i, 0)),
                   pl.BlockSpec((1, HY_WIDTH, tt), lambda b, i: (b, 0, i))],
        out_shape=[jax.ShapeDtypeStruct((b_, L, POOL_WIDTH), BF16),
                   jax.ShapeDtypeStruct((b_, L, HY_WIDTH), BF16),
                   jax.ShapeDtypeStruct((b_, HY_WIDTH, L), F32)],
        compiler_params=_cparams(("arbitrary", "arbitrary")),
        name="seqlocal",
    )(proj, proj, proj, pool_w, pool_scale, conv_w, conv_b)


def _filter_kernel(z_ref, w1_ref, b1_ref, f1_ref, w2_ref, b2_ref, f2_ref, w3_ref, dec_ref,
                   filt_ref, l1_ref, *, tr, L):
    i = pl.program_id(0)
    z = z_ref[...]
    hid = jnp.sin(f1_ref[...] * (_dot3(z, w1_ref[...]) + b1_ref[...]))
    hid = jnp.sin(f2_ref[...] * (_dot3(hid, w2_ref[...]) + b2_ref[...]))
    h = _dot3(hid, w3_ref[0])
    h = h * jnp.exp(-z[:, 0:1] * jnp.abs(dec_ref[0]))
    row = i * tr + lax.broadcasted_iota(jnp.int32, h.shape, 0)
    h = jnp.where(row == L, 0.0, h)
    filt_ref[...] = h

    @pl.when(i == 0)
    def _():
        l1_ref[...] = jnp.zeros_like(l1_ref)

    l1_ref[...] += jnp.sum(jnp.abs(h), axis=0, keepdims=True)


def _hyena_filter(zfull, w1, b1, f1, w2, b2, f2, w3, dec, L, tr):
    n = 2 * L
    nhalf = L // tr
    full = lambda i: (0, 0)
    return pl.pallas_call(
        functools.partial(_filter_kernel, tr=tr, L=L),
        grid=(n // tr,),
        in_specs=[pl.BlockSpec((tr, LANE), lambda i: (i, 0)),
                  pl.BlockSpec((LANE, LANE), full), pl.BlockSpec((1, LANE), full), pl.BlockSpec((1, LANE), full),
                  pl.BlockSpec((LANE, LANE), full), pl.BlockSpec((1, LANE), full), pl.BlockSpec((1, LANE), full),
                  pl.BlockSpec((1, LANE, HY_WIDTH), lambda i: (i // nhalf, 0, 0)),
                  pl.BlockSpec((1, 1, HY_WIDTH), lambda i: (i // nhalf, 0, 0))],
        out_specs=[pl.BlockSpec((tr, HY_WIDTH), lambda i: (i, 0)),
                   pl.BlockSpec((1, HY_WIDTH), full)],
        out_shape=[jax.ShapeDtypeStruct((n, HY_WIDTH), F32),
                   jax.ShapeDtypeStruct((1, HY_WIDTH), F32)],
        compiler_params=_cparams(("arbitrary",)),
        name="hyena_filter",
    )(zfull, w1, b1, f1, w2, b2, f2, w3, dec)


def _filt_fin_kernel(f_ref, l1_ref, o_ref):
    o_ref[...] = (f_ref[...] / l1_ref[...]).T


def _filter_finalize(filt, l1, tr):
    n = filt.shape[0]
    return pl.pallas_call(
        _filt_fin_kernel,
        grid=(n // tr,),
        in_specs=[pl.BlockSpec((tr, HY_WIDTH), lambda i: (i, 0)),
                  pl.BlockSpec((1, HY_WIDTH), lambda i: (0, 0))],
        out_specs=pl.BlockSpec((HY_WIDTH, tr), lambda i: (0, i)),
        out_shape=jax.ShapeDtypeStruct((HY_WIDTH, n), F32),
        compiler_params=_cparams(("arbitrary",)),
        name="filter_finalize",
    )(filt, l1)


FFT_N1 = 128
FFT_N2 = 128
CONV_CB = 8


def _lconv_kernel(bias_ref, z_ref, f_ref, f1d_h, f1d_l, f1f_h, f1f_l, tr_ref, ti_ref,
                  f2_h, f2_l, g2_h, g2_l, e1_h, e1_l, y_ref, bs_ref, bf_ref, cm_ref):
    blk = pl.program_id(0)
    tw_r = tr_ref[...]
    tw_i = ti_ref[...]
    n2 = FFT_N2
    for c in range(CONV_CB):
        rows = slice(c * FFT_N1, (c + 1) * FFT_N1)
        zc = jnp.concatenate([z_ref[0, c], z_ref[1, c]], axis=0)
        a = _cdot3(f1d_h[...], f1d_l[...], zc)
        ar, ai = a[:FFT_N1], a[FFT_N1:]
        bs_ref[rows, :n2] = ar * tw_r - ai * tw_i
        bs_ref[rows, n2:] = ar * tw_i + ai * tw_r
        a = _cdot3(f1f_h[...], f1f_l[...], f_ref[c])
        ar, ai = a[:FFT_N1], a[FFT_N1:]
        bf_ref[rows, :n2] = ar * tw_r - ai * tw_i
        bf_ref[rows, n2:] = ar * tw_i + ai * tw_r
    x = _dot3c(bs_ref[...], f2_h[...], f2_l[...])
    hf = _dot3c(bf_ref[...], f2_h[...], f2_l[...])
    xr, xi = x[:, :n2], x[:, n2:]
    hr, hi = hf[:, :n2], hf[:, n2:]
    y = jnp.concatenate([xr * hr - xi * hi, xr * hi + xi * hr], axis=1)
    cm_ref[...] = _dot3c(y, g2_h[...], g2_l[...])
    half = FFT_N1 // 2
    for c in range(CONV_CB):
        rows = slice(c * FFT_N1, (c + 1) * FFT_N1)
        cr = cm_ref[rows, :n2]
        ci = cm_ref[rows, n2:]
        dc = jnp.concatenate([cr * tw_r + ci * tw_i, ci * tw_r - cr * tw_i], axis=0)
        yc = _cdot3(e1_h[...], e1_l[...], dc)
        bias = bias_ref[blk * CONV_CB + c]
        y_ref[0, c] = yc[:half] + z_ref[0, c] * bias
        y_ref[1, c] = yc[half:] + z_ref[1, c] * bias


def _lconv_consts():
    n1, n2 = FFT_N1, FFT_N2
    n = n1 * n2
    idx = np.arange(128)
    ang1 = 2.0 * np.pi * ((idx[:, None] * idx[None, :]) % 128) / 128.0
    fr, fi = np.cos(ang1), -np.sin(ang1)
    angt = 2.0 * np.pi * (idx[:, None] * idx[None, :]) / n
    tw_r, tw_i = np.cos(angt), -np.sin(angt)
    h = n1 // 2
    f1d = np.block([[fr[:, :h], -fi[:, :h]], [fi[:, :h], fr[:, :h]]])
    f1f = np.concatenate([fr, fi], axis=0)
    f2 = np.block([[fr, fi], [-fi, fr]])
    g2 = np.block([[fr, -fi], [fi, fr]])
    er, ei = fr[:h] / n, -fi[:h] / n
    e1 = np.block([[er, -ei], [ei, er]])

    def hl(m):
        m32 = jnp.asarray(m, F32)
        hi = m32.astype(BF16)
        lo = (m32 - hi.astype(F32)).astype(BF16)
        return hi, lo

    return (*hl(f1d), *hl(f1f), jnp.asarray(tw_r, F32), jnp.asarray(tw_i, F32), *hl(f2), *hl(g2), *hl(e1))


def _long_conv_lat(zvt, filt_t, bias):
    b_, C, L = zvt.shape
    assert b_ == 2 and L == FFT_N1 * FFT_N2 // 2
    z4 = zvt.reshape(b_, C, FFT_N1 // 2, FFT_N2)
    f3 = filt_t.reshape(C, FFT_N1, FFT_N2)
    consts = _lconv_consts()
    cb = CONV_CB

    def cspec(a):
        return pl.BlockSpec(a.shape, lambda i: (0,) * a.ndim)

    y = pl.pallas_call(
        _lconv_kernel,
        grid=(C // cb,),
        in_specs=[pl.BlockSpec(memory_space=pltpu.SMEM),
                  pl.BlockSpec((b_, cb, FFT_N1 // 2, FFT_N2), lambda i: (0, i, 0, 0)),
                  pl.BlockSpec((cb, FFT_N1, FFT_N2), lambda i: (i, 0, 0))] + [cspec(a) for a in consts],
        out_specs=pl.BlockSpec((b_, cb, FFT_N1 // 2, FFT_N2), lambda i: (0, i, 0, 0)),
        out_shape=jax.ShapeDtypeStruct(z4.shape, F32),
        scratch_shapes=[pltpu.VMEM((cb * FFT_N1, 2 * FFT_N2), F32),
                        pltpu.VMEM((cb * FFT_N1, 2 * FFT_N2), F32),
                        pltpu.VMEM((cb * FFT_N1, 2 * FFT_N2), F32)],
        compiler_params=_cparams(("arbitrary",)),
        name="long_conv",
    )(bias, z4, f3, *consts)
    return y.reshape(b_, C, L)


def _cconv_kernel(z_ref, f_ref, bias_ref, fd_h, fd_l, ff_h, ff_l, fi_h, fi_l, y_ref, *, L):
    zr, zi = z_ref[0], z_ref[1]
    x = _dot3c(jnp.concatenate([zr, zi], axis=1), fd_h[...], fd_l[...])
    hf = _dot3c(f_ref[...], ff_h[...], ff_l[...])
    n = 2 * L
    xr, xi = x[:, :n], x[:, n:]
    hr, hi = hf[:, :n], hf[:, n:]
    y = jnp.concatenate([xr * hr - xi * hi, xr * hi + xi * hr], axis=1)
    out = _dot3c(y, fi_h[...], fi_l[...])
    bias = bias_ref[...]
    y_ref[0] = out[:, :L] + zr * bias
    y_ref[1] = out[:, L:] + zi * bias


def _cconv_consts(L):
    n = 2 * L
    idx = np.arange(n)
    ang = 2.0 * np.pi * ((idx[:, None] * idx[None, :]) % n) / n
    fr, fi = np.cos(ang), -np.sin(ang)
    fd = np.block([[fr[:L], fi[:L]], [-fi[:L], fr[:L]]])
    ff = np.concatenate([fr, fi], axis=1)
    er, ei = fr[:, :L] / n, -fi[:, :L] / n
    finv = np.block([[er, ei], [-ei, er]])

    def hl(m):
        m32 = jnp.asarray(m, F32)
        hi = m32.astype(BF16)
        lo = (m32 - hi.astype(F32)).astype(BF16)
        return hi, lo

    return (*hl(fd), *hl(ff), *hl(finv))


def _long_conv_ctx(zvt, filt_t, bias):
    b_, C, L = zvt.shape
    consts = _cconv_consts(L)
    bias_b = jnp.broadcast_to(bias[:, None], (C, L))

    def cspec(a):
        return pl.BlockSpec(a.shape, lambda i: (0,) * a.ndim)

    return pl.pallas_call(
        functools.partial(_cconv_kernel, L=L),
        grid=(1,),
        in_specs=[cspec(zvt), cspec(filt_t), cspec(bias_b)] + [cspec(a) for a in consts],
        out_specs=cspec(zvt),
        out_shape=jax.ShapeDtypeStruct(zvt.shape, F32),
        compiler_params=_cparams(("arbitrary",)),
        name="long_conv_ctx",
    )(zvt, filt_t, bias_b, *consts)


def _rms(x, g):
    ms = jnp.mean(x * x, axis=-1, keepdims=True)
    return x * lax.rsqrt(ms + EPS) * g


def _qkv_kernel(p_ref, gq_ref, gkv_ref, wq_ref, wqr_ref, wk_ref, wv_ref,
                cq_ref, sq_ref, ck_ref, sk_ref, q_ref, k_ref, v_ref):
    blk = p_ref[0]
    cq = blk[:, :Q_LORA].astype(F32)
    ckv = blk[:, Q_LORA:Q_LORA + KV_LORA].astype(F32)
    o = Q_LORA + KV_LORA
    kpe = blk[:, o:o + LANE].astype(F32)
    kpe_rot = blk[:, o + LANE:o + 2 * LANE].astype(F32)
    qn = _rms(cq, gq_ref[...]).astype(BF16)
    q = _mm(qn, wq_ref[...])
    qr = _mm(qn, wqr_ref[...])
    kvn = _rms(ckv, gkv_ref[...]).astype(BF16)
    k = _mm(kvn, wk_ref[...])
    v = _mm(kvn, wv_ref[...])
    cq_t, sq_t = cq_ref[...], sq_ref[...]
    kpe_r = kpe * ck_ref[...] + kpe_rot * sk_ref[...]
    lane = lax.broadcasted_iota(jnp.int32, (1, HEAD_PAD), 1)
    for h in range(MLA_HEADS):
        sl = slice(h * HEAD_PAD, (h + 1) * HEAD_PAD)
        q_ref[0, h] = (q[:, sl] * cq_t + qr[:, sl] * sq_t).astype(q_ref.dtype)
        k_ref[0, h] = (k[:, sl] + kpe_r).astype(k_ref.dtype)
        ones_col = (lane == (V_HEAD if h % 2 == 0 else 0)).astype(F32)
        v_ref[0, h] = (v[:, sl] + ones_col).astype(v_ref.dtype)


def _qkv(proj, gq, gkv, wq, wqr, wk, wv, cq, sq, ck, sk, tt):
    b_, L, _ = proj.shape
    full = lambda b, i: (0, 0)
    hw = MLA_HEADS * HEAD_PAD
    return pl.pallas_call(
        _qkv_kernel,
        grid=(b_, L // tt),
        in_specs=[pl.BlockSpec((1, tt, 1024), lambda b, i: (b, i, COL_QKV // 1024)),
                  pl.BlockSpec((1, Q_LORA), full), pl.BlockSpec((1, KV_LORA), full),
                  pl.BlockSpec((Q_LORA, hw), full), pl.BlockSpec((Q_LORA, hw), full),
                  pl.BlockSpec((KV_LORA, hw), full), pl.BlockSpec((KV_LORA, hw), full),
                  pl.BlockSpec((tt, LANE), lambda b, i: (i, 0)), pl.BlockSpec((tt, LANE), lambda b, i: (i, 0)),
                  pl.BlockSpec((tt, LANE), lambda b, i: (i, 0)), pl.BlockSpec((tt, LANE), lambda b, i: (i, 0))],
        out_specs=[pl.BlockSpec((1, MLA_HEADS, tt, HEAD_PAD), lambda b, i: (b, 0, i, 0)),
                   pl.BlockSpec((1, MLA_HEADS, tt, HEAD_PAD), lambda b, i: (b, 0, i, 0)),
                   pl.BlockSpec((1, MLA_HEADS, tt, HEAD_PAD), lambda b, i: (b, 0, i, 0))],
        out_shape=[jax.ShapeDtypeStruct((b_, MLA_HEADS, L, HEAD_PAD), BF16)] * 3,
        compiler_params=_cparams(("arbitrary", "arbitrary")),
        name="qkv",
    )(proj, gq, gkv, wq, wqr, wk, wv, cq, sq, ck, sk)


def _attn_kernel(q_ref, *refs, chunks):
    o_ref = refs[-1]
    tq = q_ref.shape[2]
    q = [q_ref[0, hh] for hh in range(2)]
    m = [jnp.full((tq, 1), -1e30, F32) for _ in range(2)]
    acc = [jnp.zeros((tq, LANE), F32) for _ in range(2)]
    work = [(src, start, size, hh) for (src, start, size) in chunks for hh in range(2)]

    def scores(n):
        src, start, size, hh = work[n]
        k = refs[2 * src][0, hh, start:start + size, :]
        return lax.dot_general(q[hh], k, (((1,), (1,)), ((), ())), preferred_element_type=F32)

    s_next = scores(0)
    for n, (src, start, size, hh) in enumerate(work):
        s = s_next
        if n + 1 < len(work):
            s_next = scores(n + 1)
        v = refs[2 * src + 1][0, hh, start:start + size, :]
        m_new = jnp.maximum(m[hh], jnp.max(s, axis=-1, keepdims=True))
        p = jnp.exp2(s - m_new)
        alpha = jnp.exp2(m[hh] - m_new)
        acc[hh] = alpha * acc[hh] + _mm(p.astype(BF16), v)
        m[hh] = m_new
    lane = lax.broadcasted_iota(jnp.int32, (tq, LANE), 1)
    o0 = acc[0] / acc[0][:, V_HEAD:V_HEAD + 1]
    o1 = acc[1] / acc[1][:, 0:1]
    o_ref[0] = jnp.where(lane < V_HEAD, o0, o1).astype(o_ref.dtype)


def _attention(q, kv_sources, tq, tk):
    b_, H, L, _ = q.shape
    chunks = []
    in_specs = [pl.BlockSpec((1, 2, tq, HEAD_PAD), lambda b, hp, i: (b, hp, i, 0))]
    args = [q]
    for src, (k, v) in enumerate(kv_sources):
        S = k.shape[2]
        step = min(tk, S)
        chunks += [(src, start, step) for start in range(0, S, step)]
        in_specs += [pl.BlockSpec((1, 2, S, HEAD_PAD), lambda b, hp, i: (b, hp, 0, 0))] * 2
        args += [k, v]
    return pl.pallas_call(
        functools.partial(_attn_kernel, chunks=tuple(chunks)),
        grid=(b_, H // 2, L // tq),
        in_specs=in_specs,
        out_specs=pl.BlockSpec((1, tq, LANE), lambda b, hp, i: (b, i, hp)),
        out_shape=jax.ShapeDtypeStruct((b_, L, H * V_HEAD), BF16),
        compiler_params=_cparams(("arbitrary", "arbitrary", "arbitrary")),
        name="attention",
    )(*args)


def _mixout_kernel(a_ref, x0_ref, yt_ref, o_ref, ga_ref, gb_ref, gc_ref, h_ref, g1_ref,
                   wp_ref, wh_ref, wo_ref, wout_ref, out_ref):
    hy = (x0_ref[0].astype(F32) * yt_ref[0].T).astype(BF16)
    a = _mm(a_ref[0], wp_ref[...])
    b = _mm(hy, wh_ref[...])
    cm = _mm(o_ref[0], wo_ref[...])
    m = (jax.nn.sigmoid(ga_ref[0].astype(F32)) * a + jax.nn.sigmoid(gb_ref[0].astype(F32)) * b
         + jax.nn.sigmoid(gc_ref[0].astype(F32)) * cm)
    out_ref[0] = h_ref[0] + g1_ref[0] * _mm(m.astype(BF16), wout_ref[...])


def _mixout(a_pre, x0c, yt, o, proj, h, mod, wp, wh, wo, wout, tt):
    b_, L, _ = h.shape
    full = lambda b, i: (0, 0)
    row = lambda b, i: (b, i, 0)
    gcol = COL_GATE // D_MODEL
    return pl.pallas_call(
        _mixout_kernel,
        grid=(b_, L // tt),
        in_specs=[pl.BlockSpec((1, tt, POOL_WIDTH), row),
                  pl.BlockSpec((1, tt, HY_WIDTH), row),
                  pl.BlockSpec((1, HY_WIDTH, tt), lambda b, i: (b, 0, i)),
                  pl.BlockSpec((1, tt, MLA_HEADS * V_HEAD), row),
                  pl.BlockSpec((1, tt, D_MODEL), lambda b, i: (b, i, gcol)),
                  pl.BlockSpec((1, tt, D_MODEL), lambda b, i: (b, i, gcol + 1)),
                  pl.BlockSpec((1, tt, D_MODEL), lambda b, i: (b, i, gcol + 2)),
                  pl.BlockSpec((1, tt, D_MODEL), row),
                  pl.BlockSpec((1, 1, D_MODEL), lambda b, i: (b, 0, 2)),
                  pl.BlockSpec((POOL_WIDTH, D_MODEL), full), pl.BlockSpec((HY_WIDTH, D_MODEL), full),
                  pl.BlockSpec((MLA_HEADS * V_HEAD, D_MODEL), full), pl.BlockSpec((D_MODEL, D_MODEL), full)],
        out_specs=pl.BlockSpec((1, tt, D_MODEL), row),
        out_shape=jax.ShapeDtypeStruct(h.shape, F32),
        compiler_params=_cparams(("arbitrary", "arbitrary")),
        name="mixout",
    )(a_pre, x0c, yt, o, proj, proj, proj, h, mod, wp, wh, wo, wout)


def _ffn_kernel(x_ref, sh_ref, sc_ref, g2_ref, gn_ref, w1_ref, w2_ref, fg_ref, o_ref, xn_ref, acc_ref, *, final):
    k = pl.program_id(2)
    nk = pl.num_programs(2)

    @pl.when(k == 0)
    def _():
        xn_ref[...] = _mod_norm(x_ref[0], gn_ref[...], sc_ref[0], sh_ref[0]).astype(BF16)
        acc_ref[...] = jnp.zeros_like(acc_ref)

    u = jnp.maximum(_mm(xn_ref[...], w1_ref[...]), 0.0)
    acc_ref[...] += _mm((u * u).astype(BF16), w2_ref[...])

    @pl.when(k == nk - 1)
    def _():
        y = x_ref[0] + g2_ref[0] * acc_ref[...]
        if final:
            y = _rms(y, fg_ref[...])
        o_ref[0] = y


def _ffn(h, mod, gn, w1, w2, final_g, tt, final):
    b_, L, _ = h.shape
    tf = D_FF // 2
    full = lambda b, i, k: (0, 0)
    row = lambda b, i, k: (b, i, 0)
    return pl.pallas_call(
        functools.partial(_ffn_kernel, final=final),
        grid=(b_, L // tt, D_FF // tf),
        in_specs=[pl.BlockSpec((1, tt, D_MODEL), row),
                  pl.BlockSpec((1, 1, D_MODEL), lambda b, i, k: (b, 0, 3)),
                  pl.BlockSpec((1, 1, D_MODEL), lambda b, i, k: (b, 0, 4)),
                  pl.BlockSpec((1, 1, D_MODEL), lambda b, i, k: (b, 0, 5)),
                  pl.BlockSpec((1, D_MODEL), full),
                  pl.BlockSpec((D_MODEL, tf), lambda b, i, k: (0, k)),
                  pl.BlockSpec((tf, D_MODEL), lambda b, i, k: (k, 0)),
                  pl.BlockSpec((1, D_MODEL), full)],
        out_specs=pl.BlockSpec((1, tt, D_MODEL), row),
        out_shape=jax.ShapeDtypeStruct(h.shape, F32),
        scratch_shapes=[pltpu.VMEM((tt, D_MODEL), BF16), pltpu.VMEM((tt, D_MODEL), F32)],
        compiler_params=_cparams(("arbitrary", "arbitrary", "arbitrary")),
        name="ffn",
    )(h, mod, mod, mod, gn, w1, w2, final_g)


def _rope_tables(n_rows):
    rows = jnp.repeat(jnp.arange(n_rows, dtype=F32), GRID_W)
    cols = jnp.tile(jnp.arange(GRID_W, dtype=F32), n_rows)
    half = QK_ROPE // 2
    inv = ROPE_BASE ** (-jnp.arange(0, half, 2, dtype=F32) / half)
    ar = rows[:, None] * inv
    ac = cols[:, None] * inv
    ang = jnp.concatenate([ar, ar, ac, ac], axis=-1)
    return jnp.cos(ang), jnp.sin(ang)


def _head_tables(cos, sin, L):
    ones = jnp.ones((L, QK_NOPE), F32)
    z64 = jnp.zeros((L, QK_NOPE), F32)
    z32 = jnp.zeros((L, HEAD_PAD - QK_DIM), F32)
    if cos is None:
        cos = jnp.ones((L, QK_ROPE), F32)
        sin = jnp.zeros((L, QK_ROPE), F32)
    cq = jnp.concatenate([ones, cos, z32], axis=1) * (MLA_SCALE * LOG2E)
    sq = jnp.concatenate([z64, sin, z32], axis=1) * (MLA_SCALE * LOG2E)
    ck = jnp.concatenate([z64, cos, z32], axis=1)
    sk = jnp.concatenate([z64, sin, z32], axis=1)
    return cq, sq, ck, sk


def _hy_embed(L):
    t = jnp.linspace(0.0, 1.0, L, dtype=F32)[:, None]
    omega = 2.0 * np.pi * jnp.arange(L, dtype=F32)[:, None] / L
    bands = jnp.linspace(1e-4, HY_BANDS - 1, HY_BANDS, dtype=F32)[None, :]
    z = jnp.concatenate([t, jnp.cos(omega * bands), -jnp.sin(omega * bands)], axis=-1)
    zfull = jnp.concatenate([z, jnp.zeros((1, HY_EMB), F32), z[:0:-1]], axis=0)
    return jnp.pad(zfull, ((0, 0), (0, LANE - HY_EMB)))


def _pack_w_in(w):
    z = lambda n: jnp.zeros((D_MODEL, n), F32)
    o_q = 512 + 3 * HY_WIDTH
    o_kv = o_q + Q_LORA
    o_pe = o_kv + KV_LORA
    o_gate = o_pe + QK_ROPE
    kpe = w[:, o_pe:o_gate]
    return jnp.concatenate([
        w[:, :o_pe],
        z(QK_NOPE), kpe, z(HEAD_PAD - QK_DIM),
        z(QK_NOPE), _rot_half(kpe), z(HEAD_PAD - QK_DIM),
        z(LANE),
        w[:, o_gate:]], axis=1).astype(BF16)


def _pack_heads(w_uq, w_ukv):
    wq3 = w_uq.reshape(Q_LORA, MLA_HEADS, QK_DIM)
    pad = jnp.zeros((Q_LORA, MLA_HEADS, HEAD_PAD - QK_DIM), F32)
    wq = jnp.concatenate([wq3, pad], axis=-1).reshape(Q_LORA, -1)
    wqr = jnp.concatenate([jnp.zeros((Q_LORA, MLA_HEADS, QK_NOPE), F32), _rot_half(wq3[..., QK_NOPE:]), pad],
                          axis=-1).reshape(Q_LORA, -1)
    wkv3 = w_ukv.reshape(KV_LORA, MLA_HEADS, QK_NOPE + V_HEAD)
    wk = jnp.concatenate([wkv3[..., :QK_NOPE], jnp.zeros((KV_LORA, MLA_HEADS, HEAD_PAD - QK_NOPE), F32)],
                         axis=-1).reshape(KV_LORA, -1)
    zv = jnp.zeros((KV_LORA, MLA_HEADS // 2, V_HEAD), F32)
    v3 = wkv3[..., QK_NOPE:]
    wv = jnp.stack([jnp.concatenate([v3[:, 0::2], zv], axis=-1),
                    jnp.concatenate([zv, v3[:, 1::2]], axis=-1)], axis=2).reshape(KV_LORA, -1)
    return wq.astype(BF16), wqr.astype(BF16), wk.astype(BF16), wv.astype(BF16)


def _pad2(a, r, c):
    return jnp.pad(a, ((0, r - a.shape[0]), (0, c - a.shape[1])))


TT_LAT = 512
TQ = 512
TK = 1024


def kernel(x, c, ctx, c_ctx, w_mod, b_mod, norm1_g, norm2_g, w_in, pool_w, pool_scale, pool_out, hy_conv_w, hy_conv_b, hy_f_w1, hy_f_b1, hy_f_freq1, hy_f_w2, hy_f_b2, hy_f_freq2, hy_f_w3, hy_decay, hy_bias, hy_out, q_norm_g, w_uq, kv_norm_g, w_ukv, w_o, w_out, w_ff1, w_ff2, final_g):
    B, L, _ = x.shape
    Lc = ctx.shape[1]
    cos, sin = _rope_tables(L // GRID_W)
    tabs_lat = _head_tables(cos, sin, L)
    tabs_ctx = _head_tables(None, None, Lc)
    z_lat = _hy_embed(L)
    z_ctx = _hy_embed(Lc)

    c_all = jnp.concatenate([c, c_ctx[None], jnp.zeros((8 - B - 1, D_MODEL), F32)], axis=0)
    mod_all = _modulation(c_all, w_mod, b_mod)

    h, hc = x, ctx
    fg = final_g[None]
    for l in range(DEPTH):
        last = l == DEPTH - 1
        mod_lat = mod_all[l, :B][:, None, :]
        mod_ctx = jnp.broadcast_to(mod_all[l, B][None, None, :], (B, 1, N_MOD * D_MODEL))
        g1n, g2n = norm1_g[l][None], norm2_g[l][None]
        w_in_p = _pack_w_in(w_in[l])
        wq, wqr, wk, wv = _pack_heads(w_uq[l], w_ukv[l])
        pw = pool_w[l].astype(BF16)
        ps = pool_scale[l][None]
        cw, cb = hy_conv_w[l], hy_conv_b[l][None]
        wp, wh, wo, wout = (pool_out[l].astype(BF16), hy_out[l].astype(BF16), w_o[l].astype(BF16),
                            w_out[l].astype(BF16))
        wf1, wf2 = w_ff1[l].astype(BF16), w_ff2[l].astype(BF16)
        gq, gkv = q_norm_g[l][None], kv_norm_g[l][None]
        fw1 = _pad2(hy_f_w1[l], LANE, LANE)
        fb1 = _pad2(hy_f_b1[l][None], 1, LANE)
        ff1 = _pad2(hy_f_freq1[l][None], 1, LANE)
        fw2 = _pad2(hy_f_w2[l], LANE, LANE)
        fb2 = _pad2(hy_f_b2[l][None], 1, LANE)
        ff2 = _pad2(hy_f_freq2[l][None], 1, LANE)
        fw3 = jnp.pad(hy_f_w3[l].reshape(-1, 2, HY_WIDTH).transpose(1, 0, 2),
                      ((0, 0), (0, LANE - hy_f_w3.shape[1]), (0, 0)))
        dec = hy_decay[l][:, None, :]

        def stream_pre(hs, mod, tabs, tt):
            proj = _inproj(hs, mod, g1n, w_in_p, tt)
            q, k, v = _qkv(proj, gq, gkv, wq, wqr, wk, wv, *tabs, tt)
            return proj, q, k, v

        def stream_mix(hs, mod, proj, q, kv_sources, zemb, tt, tq, tk, tr):
            Ls = hs.shape[1]
            a_pre, x0c, zvt = _seqlocal(proj, pw, ps, cw, cb, tt)
            filt, l1 = _hyena_filter(zemb, fw1, fb1, ff1, fw2, fb2, ff2, fw3, dec, Ls, tr)
            filt_t = _filter_finalize(filt, l1, tr)
            if Ls == FFT_N1 * FFT_N2 // 2:
                yt = _long_conv_lat(zvt, filt_t, hy_bias[l])
            else:
                yt = _long_conv_ctx(zvt, filt_t, hy_bias[l])
            o = _attention(q, kv_sources, tq, tk)
            return _mixout(a_pre, x0c, yt, o, proj, hs, mod, wp, wh, wo, wout, tt)

        proj_c, q_c, k_c, v_c = stream_pre(hc, mod_ctx, tabs_ctx, Lc)
        proj_l, q_l, k_l, v_l = stream_pre(h, mod_lat, tabs_lat, TT_LAT)
        h = stream_mix(h, mod_lat, proj_l, q_l, [(k_c, v_c), (k_l, v_l)], z_lat, TT_LAT, TQ, TK, 1024)
        h = _ffn(h, mod_lat, g2n, wf1, wf2, fg, TT_LAT, last)
        if not last:
            hc = stream_mix(hc, mod_ctx, proj_c, q_c, [(k_c, v_c)], z_ctx, Lc, Lc, Lc, Lc)
            hc = _ffn(hc, mod_ctx, g2n, wf1, wf2, fg, Lc, False)
    return h
```

```python
import functools
import math

import jax
import jax.numpy as jnp
import numpy as np
from jax import lax
from jax.experimental import pallas as pl
from jax.experimental.pallas import tpu as pltpu

F32 = jnp.float32
BF16 = jnp.bfloat16

D_MODEL = 1024
DEPTH = 4
GRID_W = 64
EPS = 1e-6

POOL_WIDTH = 512
POOL_WINDOWS = (2, 4, 8, 16)
POOL_GROUP = 128
HY_WIDTH = 512
HY_EMB = 33
HY_BANDS = 16
HY_HALF = 64
MLA_HEADS = 8
QK_NOPE = 64
QK_ROPE = 32
QK_DIM = 96
V_HEAD = 64
Q_LORA = 384
KV_LORA = 256
MLA_SCALE = QK_DIM ** -0.5
LOG2E = math.log2(math.e)
ROPE_BASE = 10000.0
D_FF = 4 * D_MODEL
N_MOD = 6

LANE = 128
HALO = 16
HEAD_PAD = 128

COL_POOL = 0
COL_HY = 512
COL_QKV = 2048
COL_GATE = 3072
N_PROJ = 6144

VMEM_LIMIT = 50 * 1024 * 1024


def _cparams(sem):
    return pltpu.CompilerParams(dimension_semantics=sem, vmem_limit_bytes=VMEM_LIMIT)


def _mm(a, b):
    return jnp.dot(a, b, preferred_element_type=F32)


def _split(x):
    hi = x.astype(BF16)
    lo = (x - hi.astype(F32)).astype(BF16)
    return hi, lo


def _dot3(a, b):
    ah, al = _split(a)
    bh, bl = _split(b)
    return _mm(ah, bh) + _mm(al, bh) + _mm(ah, bl)


def _dot3c(a, bh, bl):
    ah, al = _split(a)
    return _mm(ah, bh) + _mm(al, bh) + _mm(ah, bl)


def _cdot3(ch, cl, b):
    bh, bl = _split(b)
    n = b.shape[1]
    r = _mm(ch, jnp.concatenate([bh, bl], axis=1))
    return r[:, :n] + r[:, n:] + _mm(cl, bh)


def _hl(m):
    m32 = np.asarray(m, np.float32)
    hi = m32.astype(BF16)
    lo = (m32 - hi.astype(np.float32)).astype(BF16)
    return hi, lo


def _rot_half(x):
    q = QK_ROPE // 4
    return jnp.concatenate([-x[..., q:2 * q], x[..., :q], -x[..., 3 * q:], x[..., 2 * q:3 * q]], axis=-1)


def _mod_kernel(c_ref, w_ref, b_ref, o_ref):
    c = c_ref[...]
    s = c * jax.nn.sigmoid(c)
    o_ref[0] = _dot3(s, w_ref[0]) + b_ref[0]


def _modulation(c_all, w_mod, b_mod):
    tn = 1536
    n = N_MOD * D_MODEL
    return pl.pallas_call(
        _mod_kernel,
        grid=(DEPTH, n // tn),
        in_specs=[pl.BlockSpec((8, D_MODEL), lambda l, j: (0, 0)),
                  pl.BlockSpec((1, D_MODEL, tn), lambda l, j: (l, 0, j)),
                  pl.BlockSpec((1, 1, tn), lambda l, j: (l, 0, j))],
        out_specs=pl.BlockSpec((1, 8, tn), lambda l, j: (l, 0, j)),
        out_shape=jax.ShapeDtypeStruct((DEPTH, 8, n), F32),
        compiler_params=_cparams(("arbitrary", "arbitrary")),
        name="modulation",
    )(c_all, w_mod, b_mod.reshape(DEPTH, 1, n))


def _mod_norm(x, g, sc, sh):
    ms = jnp.mean(x * x, axis=-1, keepdims=True)
    return (x * lax.rsqrt(ms + EPS) * g) * (1.0 + sc) + sh


def _inproj_kernel(x_ref, sh_ref, sc_ref, g_ref, w_ref, o_ref, *, nsplit):
    xn = _mod_norm(x_ref[0], g_ref[...], sc_ref[0], sh_ref[0]).astype(BF16)
    tn = N_PROJ // nsplit
    for j in range(nsplit):
        o_ref[0, :, j * tn:(j + 1) * tn] = _mm(xn, w_ref[:, j * tn:(j + 1) * tn]).astype(o_ref.dtype)


def _inproj(h, mod, g, w, tt):
    b_, L, _ = h.shape
    return pl.pallas_call(
        functools.partial(_inproj_kernel, nsplit=3),
        grid=(b_, L // tt),
        in_specs=[pl.BlockSpec((1, tt, D_MODEL), lambda b, i: (b, i, 0)),
                  pl.BlockSpec((1, 1, D_MODEL), lambda b, i: (b, 0, 0)),
                  pl.BlockSpec((1, 1, D_MODEL), lambda b, i: (b, 0, 1)),
                  pl.BlockSpec((1, D_MODEL), lambda b, i: (0, 0)),
                  pl.BlockSpec((D_MODEL, N_PROJ), lambda b, i: (0, 0), pipeline_mode=pl.Buffered(1))],
        out_specs=pl.BlockSpec((1, tt, N_PROJ), lambda b, i: (b, i, 0)),
        out_shape=jax.ShapeDtypeStruct((b_, L, N_PROJ), BF16),
        compiler_params=_cparams(("arbitrary", "arbitrary")),
        name="inproj",
    )(h, mod, mod, g, w)


def _seqlocal_kernel(cur_ref, prev_ref, next_ref, pw_ref, ps_ref, cw_ref, cb_ref,
                     a_ref, x0_ref, zvt_ref, *, tt, L):
    i = pl.program_id(1)
    nt = pl.num_programs(1)
    keep_prev = (i > 0).astype(F32)
    keep_next = (i < nt - 1).astype(F32)
    n = tt + 2 * HALO

    def ext(lo, hi):
        return jnp.concatenate([prev_ref[0, :, lo:hi].astype(F32) * keep_prev,
                                cur_ref[0, :, lo:hi].astype(F32),
                                next_ref[0, :, lo:hi].astype(F32) * keep_next], axis=0)

    def roll(x, s):
        return pltpu.roll(x, s % n, 0)

    t = i * tt + lax.broadcasted_iota(jnp.int32, (tt, POOL_GROUP), 0)
    for gi, w in enumerate(POOL_WINDOWS):
        lo = COL_POOL + gi * POOL_GROUP
        xg = ext(lo, lo + POOL_GROUP)
        s = xg + roll(xg, 1)
        if w >= 4:
            s = roll(s, 1) + roll(s, -1)
        if w >= 8:
            s = roll(s, 2) + roll(s, -2)
        if w >= 16:
            s = roll(s, 4) + roll(s, -4)
        cnt = (jnp.minimum(t + w // 2, L) - jnp.maximum(t - w // 2, 0)).astype(F32)
        u = s[HALO:HALO + tt] / cnt - xg[HALO:HALO + tt]
        ug = _mm(u.astype(BF16), pw_ref[gi]) * ps_ref[:, gi * POOL_GROUP:(gi + 1) * POOL_GROUP]
        a_ref[0, :, gi * POOL_GROUP:(gi + 1) * POOL_GROUP] = ug.astype(a_ref.dtype)

    def conv(part, j):
        c0 = part * HY_WIDTH + j * LANE
        xe = ext(COL_HY + c0, COL_HY + c0 + LANE)
        y = (roll(xe, 1) * cw_ref[0:1, c0:c0 + LANE] + xe * cw_ref[1:2, c0:c0 + LANE]
             + roll(xe, -1) * cw_ref[2:3, c0:c0 + LANE] + cb_ref[:, c0:c0 + LANE])
        return y[HALO:HALO + tt]

    for j in range(HY_WIDTH // LANE):
        x0_ref[0, :, j * LANE:(j + 1) * LANE] = conv(0, j).astype(x0_ref.dtype)
        zv = conv(2, j) * conv(1, j)
        zvt_ref[0, j * LANE:(j + 1) * LANE, :] = zv.T


def _seqlocal(proj, pool_w, pool_scale, conv_w, conv_b, tt):
    b_, L, _ = proj.shape
    hb = tt // HALO
    nh = L // HALO
    wide = COL_QKV
    return pl.pallas_call(
        functools.partial(_seqlocal_kernel, tt=tt, L=L),
        grid=(b_, L // tt),
        in_specs=[pl.BlockSpec((1, tt, wide), lambda b, i: (b, i, 0)),
                  pl.BlockSpec((1, HALO, wide), lambda b, i: (b, jnp.maximum(i * hb - 1, 0), 0)),
                  pl.BlockSpec((1, HALO, wide), lambda b, i: (b, jnp.minimum((i + 1) * hb, nh - 1), 0)),
                  pl.BlockSpec((len(POOL_WINDOWS), POOL_GROUP, POOL_GROUP), lambda b, i: (0, 0, 0)),
                  pl.BlockSpec((1, POOL_WIDTH), lambda b, i: (0, 0)),
                  pl.BlockSpec((3, 3 * HY_WIDTH), lambda b, i: (0, 0)),
                  pl.BlockSpec((1, 3 * HY_WIDTH), lambda b, i: (0, 0))],
        out_specs=[pl.BlockSpec((1, tt, POOL_WIDTH), lambda b, i: (b, i, 0)),
                   pl.BlockSpec((1, tt, HY_WIDTH), lambda b, i: (b, i, 0)),
                   pl.BlockSpec((1, HY_WIDTH, tt), lambda b, i: (b, 0, i))],
        out_shape=[jax.ShapeDtypeStruct((b_, L, POOL_WIDTH), BF16),
                   jax.ShapeDtypeStruct((b_, L, HY_WIDTH), BF16),
                   jax.ShapeDtypeStruct((b_, HY_WIDTH, L), F32)],
        compiler_params=_cparams(("arbitrary", "arbitrary")),
        name="seqlocal",
    )(proj, proj, proj, pool_w, pool_scale, conv_w, conv_b)


def _filter_kernel(z_ref, w1_ref, b1_ref, f1_ref, w2_ref, b2_ref, f2_ref, w3_ref, dec_ref,
                   filt_ref, l1_ref, *, tr, L):
    i = pl.program_id(0)
    half = tr // 2
    z = z_ref[...]
    hid = jnp.sin(f1_ref[...] * (_dot3(z, w1_ref[...]) + b1_ref[...]))
    hid = jnp.sin(f2_ref[...] * (_dot3(hid, w2_ref[...]) + b2_ref[...]))
    decay = jnp.abs(dec_ref[0])
    total = None
    for s in range(2):
        h = _dot3(hid, w3_ref[0, s]) * jnp.exp(-z[:, s * HY_HALF:s * HY_HALF + 1] * decay)
        row = i * tr + s * half + lax.broadcasted_iota(jnp.int32, h.shape, 0)
        h = jnp.where(row == L, 0.0, h)
        filt_ref[:, s * half:(s + 1) * half] = h.T
        part = jnp.sum(jnp.abs(h), axis=0, keepdims=True)
        total = part if total is None else total + part

    @pl.when(i == 0)
    def _():
        l1_ref[...] = jnp.zeros_like(l1_ref)

    l1_ref[...] += total


def _hyena_filter(zpacked, w1, b1, f1, w2, b2, f2, w3, dec, L, tr):
    n = 2 * L
    nhalf = L // tr
    full = lambda i: (0, 0)
    return pl.pallas_call(
        functools.partial(_filter_kernel, tr=tr, L=L),
        grid=(n // tr,),
        in_specs=[pl.BlockSpec((tr // 2, LANE), lambda i: (i, 0)),
                  pl.BlockSpec((LANE, LANE), full), pl.BlockSpec((1, LANE), full), pl.BlockSpec((1, LANE), full),
                  pl.BlockSpec((LANE, LANE), full), pl.BlockSpec((1, LANE), full), pl.BlockSpec((1, LANE), full),
                  pl.BlockSpec((1, 2, LANE, HY_WIDTH), lambda i: (i // nhalf, 0, 0, 0)),
                  pl.BlockSpec((1, 1, HY_WIDTH), lambda i: (i // nhalf, 0, 0))],
        out_specs=[pl.BlockSpec((HY_WIDTH, tr), lambda i: (0, i)),
                   pl.BlockSpec((1, HY_WIDTH), full)],
        out_shape=[jax.ShapeDtypeStruct((HY_WIDTH, n), F32),
                   jax.ShapeDtypeStruct((1, HY_WIDTH), F32)],
        compiler_params=_cparams(("arbitrary",)),
        name="hyena_filter",
    )(zpacked, w1, b1, f1, w2, b2, f2, w3, dec)


FFT_N1 = 128
FFT_N2 = 128
CONV_CB = 16


def _lconv_kernel(bias_ref, l1_ref, z_ref, f_ref, f1d_h, f1d_l, f1f_h, f1f_l, tr_ref, ti_ref,
                  f2_h, f2_l, g2_h, g2_l, e1_h, e1_l, y_ref, bs_ref, bf_ref, cm_ref):
    blk = pl.program_id(0)
    tw_r = tr_ref[...]
    tw_i = ti_ref[...]
    n2 = FFT_N2
    for c in range(CONV_CB):
        rows = slice(c * FFT_N1, (c + 1) * FFT_N1)
        zc = jnp.concatenate([z_ref[0, c], z_ref[1, c]], axis=0)
        a = _cdot3(f1d_h[...], f1d_l[...], zc)
        ar, ai = a[:FFT_N1], a[FFT_N1:]
        bs_ref[rows, :n2] = ar * tw_r - ai * tw_i
        bs_ref[rows, n2:] = ar * tw_i + ai * tw_r
        inv_l1 = 1.0 / jnp.full((1, n2), l1_ref[blk * CONV_CB + c], F32)
        a = _cdot3(f1f_h[...], f1f_l[...], f_ref[c] * inv_l1)
        ar, ai = a[:FFT_N1], a[FFT_N1:]
        bf_ref[rows, :n2] = ar * tw_r - ai * tw_i
        bf_ref[rows, n2:] = ar * tw_i + ai * tw_r
    x = _dot3c(bs_ref[...], f2_h[...], f2_l[...])
    hf = _dot3c(bf_ref[...], f2_h[...], f2_l[...])
    xr, xi = x[:, :n2], x[:, n2:]
    hr, hi = hf[:, :n2], hf[:, n2:]
    y = jnp.concatenate([xr * hr - xi * hi, xr * hi + xi * hr], axis=1)
    cm_ref[...] = _dot3c(y, g2_h[...], g2_l[...])
    half = FFT_N1 // 2
    for c in range(CONV_CB):
        rows = slice(c * FFT_N1, (c + 1) * FFT_N1)
        cr = cm_ref[rows, :n2]
        ci = cm_ref[rows, n2:]
        dc = jnp.concatenate([cr * tw_r + ci * tw_i, ci * tw_r - cr * tw_i], axis=0)
        yc = _cdot3(e1_h[...], e1_l[...], dc)
        bias = bias_ref[blk * CONV_CB + c]
        y_ref[0, c] = yc[:half] + z_ref[0, c] * bias
        y_ref[1, c] = yc[half:] + z_ref[1, c] * bias


def _lconv_consts():
    n1, n2 = FFT_N1, FFT_N2
    n = n1 * n2
    idx = np.arange(128)
    ang1 = 2.0 * np.pi * ((idx[:, None] * idx[None, :]) % 128) / 128.0
    fr, fi = np.cos(ang1), -np.sin(ang1)
    angt = 2.0 * np.pi * (idx[:, None] * idx[None, :]) / n
    tw_r, tw_i = np.cos(angt), -np.sin(angt)
    h = n1 // 2
    f1d = np.block([[fr[:, :h], -fi[:, :h]], [fi[:, :h], fr[:, :h]]])
    f1f = np.concatenate([fr, fi], axis=0)
    f2 = np.block([[fr, fi], [-fi, fr]])
    g2 = np.block([[fr, -fi], [fi, fr]])
    er, ei = fr[:h] / n, -fi[:h] / n
    e1 = np.block([[er, -ei], [ei, er]])

    return (*_hl(f1d), *_hl(f1f), np.asarray(tw_r, np.float32), np.asarray(tw_i, np.float32),
            *_hl(f2), *_hl(g2), *_hl(e1))


def _long_conv_lat(zvt, filt_t, l1, bias):
    b_, C, L = zvt.shape
    assert b_ == 2 and L == FFT_N1 * FFT_N2 // 2
    z4 = zvt.reshape(b_, C, FFT_N1 // 2, FFT_N2)
    f3 = filt_t.reshape(C, FFT_N1, FFT_N2)
    consts = _lconv_consts()
    cb = CONV_CB

    def cspec(a):
        return pl.BlockSpec(a.shape, lambda i: (0,) * a.ndim)

    y = pl.pallas_call(
        _lconv_kernel,
        grid=(C // cb,),
        in_specs=[pl.BlockSpec(memory_space=pltpu.SMEM), pl.BlockSpec(memory_space=pltpu.SMEM),
                  pl.BlockSpec((b_, cb, FFT_N1 // 2, FFT_N2), lambda i: (0, i, 0, 0)),
                  pl.BlockSpec((cb, FFT_N1, FFT_N2), lambda i: (i, 0, 0))] + [cspec(a) for a in consts],
        out_specs=pl.BlockSpec((b_, cb, FFT_N1 // 2, FFT_N2), lambda i: (0, i, 0, 0)),
        out_shape=jax.ShapeDtypeStruct(z4.shape, F32),
        scratch_shapes=[pltpu.VMEM((cb * FFT_N1, 2 * FFT_N2), F32),
                        pltpu.VMEM((cb * FFT_N1, 2 * FFT_N2), F32),
                        pltpu.VMEM((cb * FFT_N1, 2 * FFT_N2), F32)],
        compiler_params=_cparams(("arbitrary",)),
        name="long_conv",
    )(bias, l1, z4, f3, *consts)
    return y.reshape(b_, C, L)


def _cconv_kernel(z_ref, f_ref, l1_ref, bias_ref, fd_h, fd_l, ff_h, ff_l, fi_h, fi_l, y_ref, *, L):
    zr, zi = z_ref[0], z_ref[1]
    x = _dot3c(jnp.concatenate([zr, zi], axis=1), fd_h[...], fd_l[...])
    hf = _dot3c(f_ref[...] * (1.0 / l1_ref[...]), ff_h[...], ff_l[...])
    n = 2 * L
    xr, xi = x[:, :n], x[:, n:]
    hr, hi = hf[:, :n], hf[:, n:]
    y = jnp.concatenate([xr * hr - xi * hi, xr * hi + xi * hr], axis=1)
    out = _dot3c(y, fi_h[...], fi_l[...])
    bias = bias_ref[...]
    y_ref[0] = out[:, :L] + zr * bias
    y_ref[1] = out[:, L:] + zi * bias


def _cconv_consts(L):
    n = 2 * L
    idx = np.arange(n)
    ang = 2.0 * np.pi * ((idx[:, None] * idx[None, :]) % n) / n
    fr, fi = np.cos(ang), -np.sin(ang)
    fd = np.block([[fr[:L], fi[:L]], [-fi[:L], fr[:L]]])
    ff = np.concatenate([fr, fi], axis=1)
    er, ei = fr[:, :L] / n, -fi[:, :L] / n
    finv = np.block([[er, ei], [-ei, er]])

    return (*_hl(fd), *_hl(ff), *_hl(finv))


def _long_conv_ctx(zvt, filt_t, l1, bias):
    b_, C, L = zvt.shape
    consts = _cconv_consts(L)
    bias_b = jnp.broadcast_to(bias[:, None], (C, L))
    l1_col = l1[:, None]

    def cspec(a):
        return pl.BlockSpec(a.shape, lambda i: (0,) * a.ndim)

    return pl.pallas_call(
        functools.partial(_cconv_kernel, L=L),
        grid=(1,),
        in_specs=[cspec(zvt), cspec(filt_t), cspec(l1_col), cspec(bias_b)] + [cspec(a) for a in consts],
        out_specs=cspec(zvt),
        out_shape=jax.ShapeDtypeStruct(zvt.shape, F32),
        compiler_params=_cparams(("arbitrary",)),
        name="long_conv_ctx",
    )(zvt, filt_t, l1_col, bias_b, *consts)


def _rms(x, g):
    ms = jnp.mean(x * x, axis=-1, keepdims=True)
    return x * lax.rsqrt(ms + EPS) * g


def _qkv_kernel(p_ref, gq_ref, gkv_ref, wq_ref, wqr_ref, wk_ref, wv_ref,
                cq_ref, sq_ref, ck_ref, sk_ref, q_ref, k_ref, v_ref):
    blk = p_ref[0]
    cq = blk[:, :Q_LORA].astype(F32)
    ckv = blk[:, Q_LORA:Q_LORA + KV_LORA].astype(F32)
    o = Q_LORA + KV_LORA
    kpe = blk[:, o:o + LANE].astype(F32)
    kpe_rot = blk[:, o + LANE:o + 2 * LANE].astype(F32)
    qn = _rms(cq, gq_ref[...]).astype(BF16)
    q = _mm(qn, wq_ref[...])
    qr = _mm(qn, wqr_ref[...])
    kvn = _rms(ckv, gkv_ref[...]).astype(BF16)
    k = _mm(kvn, wk_ref[...])
    v = _mm(kvn, wv_ref[...])
    cq_t, sq_t = cq_ref[...], sq_ref[...]
    kpe_r = kpe * ck_ref[...] + kpe_rot * sk_ref[...]
    lane = lax.broadcasted_iota(jnp.int32, (1, HEAD_PAD), 1)
    for h in range(MLA_HEADS):
        sl = slice(h * HEAD_PAD, (h + 1) * HEAD_PAD)
        q_ref[0, h] = (q[:, sl] * cq_t + qr[:, sl] * sq_t).astype(q_ref.dtype)
        k_ref[0, h] = (k[:, sl] + kpe_r).astype(k_ref.dtype)
        ones_col = (lane == (V_HEAD if h % 2 == 0 else 0)).astype(F32)
        v_ref[0, h] = (v[:, sl] + ones_col).astype(v_ref.dtype)


def _qkv(proj, gq, gkv, wq, wqr, wk, wv, cq, sq, ck, sk, tt):
    b_, L, _ = proj.shape
    full = lambda b, i: (0, 0)
    hw = MLA_HEADS * HEAD_PAD
    return pl.pallas_call(
        _qkv_kernel,
        grid=(b_, L // tt),
        in_specs=[pl.BlockSpec((1, tt, 1024), lambda b, i: (b, i, COL_QKV // 1024)),
                  pl.BlockSpec((1, Q_LORA), full), pl.BlockSpec((1, KV_LORA), full),
                  pl.BlockSpec((Q_LORA, hw), full), pl.BlockSpec((Q_LORA, hw), full),
                  pl.BlockSpec((KV_LORA, hw), full), pl.BlockSpec((KV_LORA, hw), full),
                  pl.BlockSpec((tt, LANE), lambda b, i: (i, 0)), pl.BlockSpec((tt, LANE), lambda b, i: (i, 0)),
                  pl.BlockSpec((tt, LANE), lambda b, i: (i, 0)), pl.BlockSpec((tt, LANE), lambda b, i: (i, 0))],
        out_specs=[pl.BlockSpec((1, MLA_HEADS, tt, HEAD_PAD), lambda b, i: (b, 0, i, 0)),
                   pl.BlockSpec((1, MLA_HEADS, tt, HEAD_PAD), lambda b, i: (b, 0, i, 0)),
                   pl.BlockSpec((1, MLA_HEADS, tt, HEAD_PAD), lambda b, i: (b, 0, i, 0))],
        out_shape=[jax.ShapeDtypeStruct((b_, MLA_HEADS, L, HEAD_PAD), BF16)] * 3,
        compiler_params=_cparams(("arbitrary", "arbitrary")),
        name="qkv",
    )(proj, gq, gkv, wq, wqr, wk, wv, cq, sq, ck, sk)


def _attn_kernel(q_ref, *refs, chunks):
    o_ref = refs[-1]
    tq = q_ref.shape[2]
    q = [q_ref[0, hh] for hh in range(2)]
    m = [jnp.full((tq, 1), -1e30, F32) for _ in range(2)]
    acc = [jnp.zeros((tq, LANE), F32) for _ in range(2)]
    work = [(src, start, size, hh) for (src, start, size) in chunks for hh in range(2)]

    def scores(n):
        src, start, size, hh = work[n]
        k = refs[2 * src][0, hh, start:start + size, :]
        return lax.dot_general(q[hh], k, (((1,), (1,)), ((), ())), preferred_element_type=F32)

    s_next = scores(0)
    for n, (src, start, size, hh) in enumerate(work):
        s = s_next
        if n + 1 < len(work):
            s_next = scores(n + 1)
        v = refs[2 * src + 1][0, hh, start:start + size, :]
        m_new = jnp.maximum(m[hh], jnp.max(s, axis=-1, keepdims=True))
        p = jnp.exp2(s - m_new)
        alpha = jnp.exp2(m[hh] - m_new)
        acc[hh] = alpha * acc[hh] + _mm(p.astype(BF16), v)
        m[hh] = m_new
    lane = lax.broadcasted_iota(jnp.int32, (tq, LANE), 1)
    o0 = acc[0] / acc[0][:, V_HEAD:V_HEAD + 1]
    o1 = acc[1] / acc[1][:, 0:1]
    o_ref[0] = jnp.where(lane < V_HEAD, o0, o1).astype(o_ref.dtype)


def _attention(q, kv_sources, tq, tk):
    b_, H, L, _ = q.shape
    chunks = []
    in_specs = [pl.BlockSpec((1, 2, tq, HEAD_PAD), lambda b, hp, i: (b, hp, i, 0))]
    args = [q]
    for src, (k, v) in enumerate(kv_sources):
        S = k.shape[2]
        step = min(tk, S)
        chunks += [(src, start, step) for start in range(0, S, step)]
        in_specs += [pl.BlockSpec((1, 2, S, HEAD_PAD), lambda b, hp, i: (b, hp, 0, 0))] * 2
        args += [k, v]
    return pl.pallas_call(
        functools.partial(_attn_kernel, chunks=tuple(chunks)),
        grid=(b_, H // 2, L // tq),
        in_specs=in_specs,
        out_specs=pl.BlockSpec((1, tq, LANE), lambda b, hp, i: (b, i, hp)),
        out_shape=jax.ShapeDtypeStruct((b_, L, H * V_HEAD), BF16),
        compiler_params=_cparams(("arbitrary", "arbitrary", "arbitrary")),
        name="attention",
    )(*args)


def _mixout_kernel(a_ref, x0_ref, yt_ref, o_ref, ga_ref, gb_ref, gc_ref, h_ref, g1_ref,
                   wp_ref, wh_ref, wo_ref, wout_ref, out_ref):
    hy = (x0_ref[0].astype(F32) * yt_ref[0].T).astype(BF16)
    a = _mm(a_ref[0], wp_ref[...])
    b = _mm(hy, wh_ref[...])
    cm = _mm(o_ref[0], wo_ref[...])
    m = (jax.nn.sigmoid(ga_ref[0].astype(F32)) * a + jax.nn.sigmoid(gb_ref[0].astype(F32)) * b
         + jax.nn.sigmoid(gc_ref[0].astype(F32)) * cm)
    out_ref[0] = h_ref[0] + g1_ref[0] * _mm(m.astype(BF16), wout_ref[...])


def _mixout(a_pre, x0c, yt, o, proj, h, mod, wp, wh, wo, wout, tt):
    b_, L, _ = h.shape
    full = lambda b, i: (0, 0)
    row = lambda b, i: (b, i, 0)
    gcol = COL_GATE // D_MODEL
    return pl.pallas_call(
        _mixout_kernel,
        grid=(b_, L // tt),
        in_specs=[pl.BlockSpec((1, tt, POOL_WIDTH), row),
                  pl.BlockSpec((1, tt, HY_WIDTH), row),
                  pl.BlockSpec((1, HY_WIDTH, tt), lambda b, i: (b, 0, i)),
                  pl.BlockSpec((1, tt, MLA_HEADS * V_HEAD), row),
                  pl.BlockSpec((1, tt, D_MODEL), lambda b, i: (b, i, gcol)),
                  pl.BlockSpec((1, tt, D_MODEL), lambda b, i: (b, i, gcol + 1)),
                  pl.BlockSpec((1, tt, D_MODEL), lambda b, i: (b, i, gcol + 2)),
                  pl.BlockSpec((1, tt, D_MODEL), row),
                  pl.BlockSpec((1, 1, D_MODEL), lambda b, i: (b, 0, 2)),
                  pl.BlockSpec((POOL_WIDTH, D_MODEL), full), pl.BlockSpec((HY_WIDTH, D_MODEL), full),
                  pl.BlockSpec((MLA_HEADS * V_HEAD, D_MODEL), full), pl.BlockSpec((D_MODEL, D_MODEL), full)],
        out_specs=pl.BlockSpec((1, tt, D_MODEL), row),
        out_shape=jax.ShapeDtypeStruct(h.shape, F32),
        compiler_params=_cparams(("arbitrary", "arbitrary")),
        name="mixout",
    )(a_pre, x0c, yt, o, proj, proj, proj, h, mod, wp, wh, wo, wout)


def _ffn_kernel(x_ref, sh_ref, sc_ref, g2_ref, gn_ref, w1_ref, w2_ref, fg_ref, o_ref, *, final, nsplit):
    x = x_ref[0]
    xn = _mod_norm(x, gn_ref[...], sc_ref[0], sh_ref[0]).astype(BF16)
    tf = D_FF // nsplit
    acc = None
    for k in range(nsplit):
        u = jnp.maximum(_mm(xn, w1_ref[:, k * tf:(k + 1) * tf]), 0.0)
        part = _mm((u * u).astype(BF16), w2_ref[k * tf:(k + 1) * tf, :])
        acc = part if acc is None else acc + part
    y = x + g2_ref[0] * acc
    if final:
        y = _rms(y, fg_ref[...])
    o_ref[0] = y


def _ffn(h, mod, gn, w1, w2, final_g, tt, final):
    b_, L, _ = h.shape
    full = lambda b, i: (0, 0)
    row = lambda b, i: (b, i, 0)
    resident = dict(pipeline_mode=pl.Buffered(1))
    return pl.pallas_call(
        functools.partial(_ffn_kernel, final=final, nsplit=2),
        grid=(b_, L // tt),
        in_specs=[pl.BlockSpec((1, tt, D_MODEL), row),
                  pl.BlockSpec((1, 1, D_MODEL), lambda b, i: (b, 0, 3)),
                  pl.BlockSpec((1, 1, D_MODEL), lambda b, i: (b, 0, 4)),
                  pl.BlockSpec((1, 1, D_MODEL), lambda b, i: (b, 0, 5)),
                  pl.BlockSpec((1, D_MODEL), full),
                  pl.BlockSpec((D_MODEL, D_FF), full, **resident),
                  pl.BlockSpec((D_FF, D_MODEL), full, **resident),
                  pl.BlockSpec((1, D_MODEL), full)],
        out_specs=pl.BlockSpec((1, tt, D_MODEL), row),
        out_shape=jax.ShapeDtypeStruct(h.shape, F32),
        compiler_params=_cparams(("arbitrary", "arbitrary")),
        name="ffn",
    )(h, mod, mod, mod, gn, w1, w2, final_g)


def _rope_tables(n_rows):
    rows = np.repeat(np.arange(n_rows, dtype=np.float64), GRID_W)
    cols = np.tile(np.arange(GRID_W, dtype=np.float64), n_rows)
    half = QK_ROPE // 2
    inv = ROPE_BASE ** (-np.arange(0, half, 2, dtype=np.float64) / half)
    ar = rows[:, None] * inv
    ac = cols[:, None] * inv
    ang = np.concatenate([ar, ar, ac, ac], axis=-1)
    return np.cos(ang), np.sin(ang)


def _head_tables(cos, sin, L):
    ones = np.ones((L, QK_NOPE))
    z64 = np.zeros((L, QK_NOPE))
    z32 = np.zeros((L, HEAD_PAD - QK_DIM))
    if cos is None:
        cos = np.ones((L, QK_ROPE))
        sin = np.zeros((L, QK_ROPE))
    cq = np.concatenate([ones, cos, z32], axis=1) * (MLA_SCALE * LOG2E)
    sq = np.concatenate([z64, sin, z32], axis=1) * (MLA_SCALE * LOG2E)
    ck = np.concatenate([z64, cos, z32], axis=1)
    sk = np.concatenate([z64, sin, z32], axis=1)
    return tuple(np.asarray(t, np.float32) for t in (cq, sq, ck, sk))


def _hy_embed(L, tr):
    t = np.linspace(0.0, 1.0, L)[:, None]
    omega = 2.0 * np.pi * np.arange(L)[:, None] / L
    bands = np.linspace(1e-4, HY_BANDS - 1, HY_BANDS)[None, :]
    z = np.concatenate([t, np.cos(omega * bands), -np.sin(omega * bands)], axis=-1)
    zfull = np.concatenate([z, np.zeros((1, HY_EMB)), z[:0:-1]], axis=0)
    zfull = np.pad(zfull, ((0, 0), (0, HY_HALF - HY_EMB)))
    packed = zfull.reshape(2 * L // tr, 2, tr // 2, HY_HALF).transpose(0, 2, 1, 3).reshape(L, 2 * HY_HALF)
    return np.asarray(packed, np.float32)


def _blockdiag2(a):
    a = _pad2(a, HY_HALF, HY_HALF)
    z = jnp.zeros_like(a)
    return jnp.concatenate([jnp.concatenate([a, z], axis=1), jnp.concatenate([z, a], axis=1)], axis=0)


def _pack_w_in(w):
    z = lambda n: jnp.zeros((D_MODEL, n), F32)
    o_q = 512 + 3 * HY_WIDTH
    o_kv = o_q + Q_LORA
    o_pe = o_kv + KV_LORA
    o_gate = o_pe + QK_ROPE
    kpe = w[:, o_pe:o_gate]
    return jnp.concatenate([
        w[:, :o_pe],
        z(QK_NOPE), kpe, z(HEAD_PAD - QK_DIM),
        z(QK_NOPE), _rot_half(kpe), z(HEAD_PAD - QK_DIM),
        z(LANE),
        w[:, o_gate:]], axis=1).astype(BF16)


def _pack_heads(w_uq, w_ukv):
    wq3 = w_uq.reshape(Q_LORA, MLA_HEADS, QK_DIM)
    pad = jnp.zeros((Q_LORA, MLA_HEADS, HEAD_PAD - QK_DIM), F32)
    wq = jnp.concatenate([wq3, pad], axis=-1).reshape(Q_LORA, -1)
    wqr = jnp.concatenate([jnp.zeros((Q_LORA, MLA_HEADS, QK_NOPE), F32), _rot_half(wq3[..., QK_NOPE:]), pad],
                          axis=-1).reshape(Q_LORA, -1)
    wkv3 = w_ukv.reshape(KV_LORA, MLA_HEADS, QK_NOPE + V_HEAD)
    wk = jnp.concatenate([wkv3[..., :QK_NOPE], jnp.zeros((KV_LORA, MLA_HEADS, HEAD_PAD - QK_NOPE), F32)],
                         axis=-1).reshape(KV_LORA, -1)
    zv = jnp.zeros((KV_LORA, MLA_HEADS // 2, V_HEAD), F32)
    v3 = wkv3[..., QK_NOPE:]
    wv = jnp.stack([jnp.concatenate([v3[:, 0::2], zv], axis=-1),
                    jnp.concatenate([zv, v3[:, 1::2]], axis=-1)], axis=2).reshape(KV_LORA, -1)
    return wq.astype(BF16), wqr.astype(BF16), wk.astype(BF16), wv.astype(BF16)


def _pad2(a, r, c):
    return jnp.pad(a, ((0, r - a.shape[0]), (0, c - a.shape[1])))


TT_LAT = 512
TQ = 1024
TK = 512
TR_LAT = 1024


def kernel(x, c, ctx, c_ctx, w_mod, b_mod, norm1_g, norm2_g, w_in, pool_w, pool_scale, pool_out, hy_conv_w, hy_conv_b, hy_f_w1, hy_f_b1, hy_f_freq1, hy_f_w2, hy_f_b2, hy_f_freq2, hy_f_w3, hy_decay, hy_bias, hy_out, q_norm_g, w_uq, kv_norm_g, w_ukv, w_o, w_out, w_ff1, w_ff2, final_g):
    B, L, _ = x.shape
    Lc = ctx.shape[1]
    cos, sin = _rope_tables(L // GRID_W)
    tabs_lat = _head_tables(cos, sin, L)
    tabs_ctx = _head_tables(None, None, Lc)
    z_lat = _hy_embed(L, TR_LAT)
    z_ctx = _hy_embed(Lc, Lc)

    c_all = jnp.concatenate([c, c_ctx[None], jnp.zeros((8 - B - 1, D_MODEL), F32)], axis=0)
    mod_all = _modulation(c_all, w_mod, b_mod)

    h, hc = x, ctx
    fg = final_g[None]
    for l in range(DEPTH):
        last = l == DEPTH - 1
        mod_lat = mod_all[l, :B][:, None, :]
        mod_ctx = jnp.broadcast_to(mod_all[l, B][None, None, :], (B, 1, N_MOD * D_MODEL))
        g1n, g2n = norm1_g[l][None], norm2_g[l][None]
        w_in_p = _pack_w_in(w_in[l])
        wq, wqr, wk, wv = _pack_heads(w_uq[l], w_ukv[l])
        pw = pool_w[l].astype(BF16)
        ps = pool_scale[l][None]
        cw, cb = hy_conv_w[l], hy_conv_b[l][None]
        wp, wh, wo, wout = (pool_out[l].astype(BF16), hy_out[l].astype(BF16), w_o[l].astype(BF16),
                            w_out[l].astype(BF16))
        wf1, wf2 = w_ff1[l].astype(BF16), w_ff2[l].astype(BF16)
        gq, gkv = q_norm_g[l][None], kv_norm_g[l][None]
        twice = lambda v: jnp.tile(_pad2(v[None], 1, HY_HALF), (1, 2))
        fw1, fw2 = _blockdiag2(hy_f_w1[l]), _blockdiag2(hy_f_w2[l])
        fb1, ff1, fb2, ff2 = twice(hy_f_b1[l]), twice(hy_f_freq1[l]), twice(hy_f_b2[l]), twice(hy_f_freq2[l])
        w3d = _pad2(hy_f_w3[l], HY_HALF, 2 * HY_WIDTH).reshape(HY_HALF, 2, HY_WIDTH).transpose(1, 0, 2)
        z3 = jnp.zeros_like(w3d)
        fw3 = jnp.stack([jnp.concatenate([w3d, z3], axis=1), jnp.concatenate([z3, w3d], axis=1)], axis=1)
        dec = hy_decay[l][:, None, :]

        def stream_pre(hs, mod, tabs, tt):
            proj = _inproj(hs, mod, g1n, w_in_p, tt)
            q, k, v = _qkv(proj, gq, gkv, wq, wqr, wk, wv, *tabs, tt)
            return proj, q, k, v

        def stream_mix(hs, mod, proj, q, kv_sources, zemb, tt, tq, tk, tr):
            Ls = hs.shape[1]
            a_pre, x0c, zvt = _seqlocal(proj, pw, ps, cw, cb, tt)
            filt_t, l1 = _hyena_filter(zemb, fw1, fb1, ff1, fw2, fb2, ff2, fw3, dec, Ls, tr)
            if Ls == FFT_N1 * FFT_N2 // 2:
                yt = _long_conv_lat(zvt, filt_t, l1[0], hy_bias[l])
            else:
                yt = _long_conv_ctx(zvt, filt_t, l1[0], hy_bias[l])
            o = _attention(q, kv_sources, tq, tk)
            return _mixout(a_pre, x0c, yt, o, proj, hs, mod, wp, wh, wo, wout, tt)

        proj_c, q_c, k_c, v_c = stream_pre(hc, mod_ctx, tabs_ctx, Lc)
        proj_l, q_l, k_l, v_l = stream_pre(h, mod_lat, tabs_lat, TT_LAT)
        h = stream_mix(h, mod_lat, proj_l, q_l, [(k_c, v_c), (k_l, v_l)], z_lat, TT_LAT, TQ, TK, TR_LAT)
        h = _ffn(h, mod_lat, g2n, wf1, wf2, fg, TT_LAT, last)
        if not last:
            hc = stream_mix(hc, mod_ctx, proj_c, q_c, [(k_c, v_c)], z_ctx, Lc, Lc, Lc, Lc)
            hc = _ffn(hc, mod_ctx, g2n, wf1, wf2, fg, Lc, False)
    return h
```

```python
import functools
import math

import jax
import jax.numpy as jnp
import numpy as np
from jax import lax
from jax.experimental import pallas as pl
from jax.experimental.pallas import tpu as pltpu

F32 = jnp.float32
BF16 = jnp.bfloat16

D_MODEL = 1024
DEPTH = 4
GRID_W = 64
EPS = 1e-6

POOL_WIDTH = 512
POOL_WINDOWS = (2, 4, 8, 16)
POOL_GROUP = 128
HY_WIDTH = 512
HY_EMB = 33
HY_BANDS = 16
HY_HALF = 64
MLA_HEADS = 8
QK_NOPE = 64
QK_ROPE = 32
QK_DIM = 96
V_HEAD = 64
Q_LORA = 384
KV_LORA = 256
MLA_SCALE = QK_DIM ** -0.5
LOG2E = math.log2(math.e)
ROPE_BASE = 10000.0
D_FF = 4 * D_MODEL
N_MOD = 6
MOD_ROWS = 8

LANE = 128
HALO = 16
HEAD_PAD = 128

COL_POOL = 0
COL_HY = 512
COL_QKV = 2048
COL_GATE = 3072
N_PROJ = 6144

VMEM_LIMIT = 50 * 1024 * 1024


def _cparams(sem):
    return pltpu.CompilerParams(dimension_semantics=sem, vmem_limit_bytes=VMEM_LIMIT)


def _mm(a, b):
    return jnp.dot(a, b, preferred_element_type=F32)


def _split(x):
    hi = x.astype(BF16)
    lo = (x - hi.astype(F32)).astype(BF16)
    return hi, lo


def _dot3(a, b):
    ah, al = _split(a)
    bh, bl = _split(b)
    return _mm(ah, bh) + _mm(al, bh) + _mm(ah, bl)


def _dot3c(a, bh, bl):
    ah, al = _split(a)
    return _mm(ah, bh) + _mm(al, bh) + _mm(ah, bl)


def _cdot3(ch, cl, b):
    bh, bl = _split(b)
    n = b.shape[1]
    r = _mm(ch, jnp.concatenate([bh, bl], axis=1))
    return r[:, :n] + r[:, n:] + _mm(cl, bh)


def _hl(m):
    m32 = np.asarray(m, np.float32)
    hi = m32.astype(BF16)
    lo = (m32 - hi.astype(np.float32)).astype(BF16)
    return hi, lo


def _rot_half(x):
    q = QK_ROPE // 4
    return jnp.concatenate([-x[..., q:2 * q], x[..., :q], -x[..., 3 * q:], x[..., 2 * q:3 * q]], axis=-1)


def _mod_kernel(c_ref, w_ref, b_ref, o_ref):
    c = c_ref[...]
    s = c * jax.nn.sigmoid(c)
    o_ref[0] = _dot3(s, w_ref[0]) + b_ref[0]


def _modulation(c_all, w_mod, b_mod):
    tn = 1536
    n = N_MOD * D_MODEL
    return pl.pallas_call(
        _mod_kernel,
        grid=(DEPTH, n // tn),
        in_specs=[pl.BlockSpec((MOD_ROWS, D_MODEL), lambda l, j: (0, 0)),
                  pl.BlockSpec((1, D_MODEL, tn), lambda l, j: (l, 0, j)),
                  pl.BlockSpec((1, 1, tn), lambda l, j: (l, 0, j))],
        out_specs=pl.BlockSpec((1, MOD_ROWS, tn), lambda l, j: (l, 0, j)),
        out_shape=jax.ShapeDtypeStruct((DEPTH, MOD_ROWS, n), F32),
        compiler_params=_cparams(("arbitrary", "arbitrary")),
        name="modulation",
    )(c_all, w_mod, b_mod.reshape(DEPTH, 1, n))


def _mod_norm(x, g, sc, sh):
    ms = jnp.mean(x * x, axis=-1, keepdims=True)
    return (x * lax.rsqrt(ms + EPS) * g) * (1.0 + sc) + sh


def _inproj_kernel(x_ref, sh_ref, sc_ref, g_ref, w_ref, o_ref, *, nsplit):
    xn = _mod_norm(x_ref[0], g_ref[...], sc_ref[0], sh_ref[0]).astype(BF16)
    tn = N_PROJ // nsplit
    for j in range(nsplit):
        o_ref[0, :, j * tn:(j + 1) * tn] = _mm(xn, w_ref[:, j * tn:(j + 1) * tn]).astype(o_ref.dtype)


def _inproj(h, mod, mrow, g, w, l, tt):
    b_, L, _ = h.shape
    return pl.pallas_call(
        functools.partial(_inproj_kernel, nsplit=3),
        grid=(b_, L // tt),
        in_specs=[pl.BlockSpec((1, tt, D_MODEL), lambda b, i: (b, i, 0)),
                  pl.BlockSpec((1, 1, D_MODEL), lambda b, i: (mrow(b), 0, 0)),
                  pl.BlockSpec((1, 1, D_MODEL), lambda b, i: (mrow(b), 0, 1)),
                  pl.BlockSpec((None, 1, D_MODEL), lambda b, i: (l, 0, 0)),
                  pl.BlockSpec((None, D_MODEL, N_PROJ), lambda b, i: (l, 0, 0), pipeline_mode=pl.Buffered(1))],
        out_specs=pl.BlockSpec((1, tt, N_PROJ), lambda b, i: (b, i, 0)),
        out_shape=jax.ShapeDtypeStruct((b_, L, N_PROJ), BF16),
        compiler_params=_cparams(("arbitrary", "arbitrary")),
        name="inproj",
    )(h, mod, mod, g, w)


def _seqlocal_kernel(cur_ref, prev_ref, next_ref, pw_ref, ps_ref, cw_ref, cb_ref,
                     a_ref, x0_ref, zvt_ref, *, tt, L):
    i = pl.program_id(1)
    nt = pl.num_programs(1)
    keep_prev = (i > 0).astype(F32)
    keep_next = (i < nt - 1).astype(F32)
    n = tt + 2 * HALO

    def ext(lo, hi):
        return jnp.concatenate([prev_ref[0, :, lo:hi].astype(F32) * keep_prev,
                                cur_ref[0, :, lo:hi].astype(F32),
                                next_ref[0, :, lo:hi].astype(F32) * keep_next], axis=0)

    def roll(x, s):
        return pltpu.roll(x, s % n, 0)

    t = i * tt + lax.broadcasted_iota(jnp.int32, (tt, POOL_GROUP), 0)
    for gi, w in enumerate(POOL_WINDOWS):
        lo = COL_POOL + gi * POOL_GROUP
        xg = ext(lo, lo + POOL_GROUP)
        s = xg + roll(xg, 1)
        if w >= 4:
            s = roll(s, 1) + roll(s, -1)
        if w >= 8:
            s = roll(s, 2) + roll(s, -2)
        if w >= 16:
            s = roll(s, 4) + roll(s, -4)
        cnt = (jnp.minimum(t + w // 2, L) - jnp.maximum(t - w // 2, 0)).astype(F32)
        u = s[HALO:HALO + tt] / cnt - xg[HALO:HALO + tt]
        ug = _mm(u.astype(BF16), pw_ref[gi]) * ps_ref[:, gi * POOL_GROUP:(gi + 1) * POOL_GROUP]
        a_ref[0, :, gi * POOL_GROUP:(gi + 1) * POOL_GROUP] = ug.astype(a_ref.dtype)

    def conv(part, j):
        c0 = part * HY_WIDTH + j * LANE
        xe = ext(COL_HY + c0, COL_HY + c0 + LANE)
        y = (roll(xe, 1) * cw_ref[0:1, c0:c0 + LANE] + xe * cw_ref[1:2, c0:c0 + LANE]
             + roll(xe, -1) * cw_ref[2:3, c0:c0 + LANE] + cb_ref[:, c0:c0 + LANE])
        return y[HALO:HALO + tt]

    for j in range(HY_WIDTH // LANE):
        x0_ref[0, :, j * LANE:(j + 1) * LANE] = conv(0, j).astype(x0_ref.dtype)
        zv = conv(2, j) * conv(1, j)
        zvt_ref[0, j * LANE:(j + 1) * LANE, :] = zv.T


def _seqlocal(proj, pool_w, pool_scale, conv_w, conv_b, l, tt):
    b_, L, _ = proj.shape
    hb = tt // HALO
    nh = L // HALO
    wide = COL_QKV
    return pl.pallas_call(
        functools.partial(_seqlocal_kernel, tt=tt, L=L),
        grid=(b_, L // tt),
        in_specs=[pl.BlockSpec((1, tt, wide), lambda b, i: (b, i, 0)),
                  pl.BlockSpec((1, HALO, wide), lambda b, i: (b, jnp.maximum(i * hb - 1, 0), 0)),
                  pl.BlockSpec((1, HALO, wide), lambda b, i: (b, jnp.minimum((i + 1) * hb, nh - 1), 0)),
                  pl.BlockSpec((None, len(POOL_WINDOWS), POOL_GROUP, POOL_GROUP), lambda b, i: (l, 0, 0, 0)),
                  pl.BlockSpec((None, 1, POOL_WIDTH), lambda b, i: (l, 0, 0)),
                  pl.BlockSpec((None, 3, 3 * HY_WIDTH), lambda b, i: (l, 0, 0)),
                  pl.BlockSpec((None, 1, 3 * HY_WIDTH), lambda b, i: (l, 0, 0))],
        out_specs=[pl.BlockSpec((1, tt, POOL_WIDTH), lambda b, i: (b, i, 0)),
                   pl.BlockSpec((1, tt, HY_WIDTH), lambda b, i: (b, i, 0)),
                   pl.BlockSpec((1, HY_WIDTH, tt), lambda b, i: (b, 0, i))],
        out_shape=[jax.ShapeDtypeStruct((b_, L, POOL_WIDTH), BF16),
                   jax.ShapeDtypeStruct((b_, L, HY_WIDTH), BF16),
                   jax.ShapeDtypeStruct((b_, HY_WIDTH, L), F32)],
        compiler_params=_cparams(("arbitrary", "arbitrary")),
        name="seqlocal",
    )(proj, proj, proj, pool_w, pool_scale, conv_w, conv_b)


def _filter_kernel(z_ref, w1_ref, b1_ref, f1_ref, w2_ref, b2_ref, f2_ref, w3_ref, dec_ref,
                   filt_ref, l1_ref, *, tr, L):
    i = pl.program_id(0)
    half = tr // 2
    z = z_ref[...]
    hid = jnp.sin(f1_ref[...] * (_dot3(z, w1_ref[...]) + b1_ref[...]))
    hid = jnp.sin(f2_ref[...] * (_dot3(hid, w2_ref[...]) + b2_ref[...]))
    decay = jnp.abs(dec_ref[0])
    total = None
    for s in range(2):
        h = _dot3(hid, w3_ref[0, s]) * jnp.exp(-z[:, s * HY_HALF:s * HY_HALF + 1] * decay)
        row = i * tr + s * half + lax.broadcasted_iota(jnp.int32, h.shape, 0)
        h = jnp.where(row == L, 0.0, h)
        filt_ref[:, s * half:(s + 1) * half] = h.T
        part = jnp.sum(jnp.abs(h), axis=0, keepdims=True)
        total = part if total is None else total + part

    @pl.when(i == 0)
    def _():
        l1_ref[...] = jnp.zeros_like(l1_ref)

    l1_ref[...] += total


def _hyena_filter(zpacked, w1, b1, f1, w2, b2, f2, w3, dec, l, L, tr):
    n = 2 * L
    nhalf = L // tr
    full = lambda i: (0, 0)
    mat = pl.BlockSpec((None, LANE, LANE), lambda i: (l, 0, 0))
    vec = pl.BlockSpec((None, 1, LANE), lambda i: (l, 0, 0))
    return pl.pallas_call(
        functools.partial(_filter_kernel, tr=tr, L=L),
        grid=(n // tr,),
        in_specs=[pl.BlockSpec((tr // 2, LANE), lambda i: (i, 0)),
                  mat, vec, vec, mat, vec, vec,
                  pl.BlockSpec((None, 1, 2, LANE, HY_WIDTH), lambda i: (l, i // nhalf, 0, 0, 0)),
                  pl.BlockSpec((None, 1, 1, HY_WIDTH), lambda i: (l, i // nhalf, 0, 0))],
        out_specs=[pl.BlockSpec((HY_WIDTH, tr), lambda i: (0, i)),
                   pl.BlockSpec((1, HY_WIDTH), full)],
        out_shape=[jax.ShapeDtypeStruct((HY_WIDTH, n), F32),
                   jax.ShapeDtypeStruct((1, HY_WIDTH), F32)],
        compiler_params=_cparams(("arbitrary",)),
        name="hyena_filter",
    )(zpacked, w1, b1, f1, w2, b2, f2, w3, dec)


FFT_N1 = 128
FFT_N2 = 128
CONV_CB = 16


def _lconv_kernel(bias_ref, l1_ref, z_ref, f_ref, f1d_h, f1d_l, f1f_h, f1f_l, tr_ref, ti_ref,
                  f2_h, f2_l, g2_h, g2_l, e1_h, e1_l, y_ref, bs_ref, bf_ref, cm_ref, *, layer):
    blk = pl.program_id(0)
    tw_r = tr_ref[...]
    tw_i = ti_ref[...]
    n2 = FFT_N2
    for c in range(CONV_CB):
        rows = slice(c * FFT_N1, (c + 1) * FFT_N1)
        zc = jnp.concatenate([z_ref[0, c], z_ref[1, c]], axis=0)
        a = _cdot3(f1d_h[...], f1d_l[...], zc)
        ar, ai = a[:FFT_N1], a[FFT_N1:]
        bs_ref[rows, :n2] = ar * tw_r - ai * tw_i
        bs_ref[rows, n2:] = ar * tw_i + ai * tw_r
        inv_l1 = 1.0 / jnp.full((1, n2), l1_ref[0, blk * CONV_CB + c], F32)
        a = _cdot3(f1f_h[...], f1f_l[...], f_ref[c] * inv_l1)
        ar, ai = a[:FFT_N1], a[FFT_N1:]
        bf_ref[rows, :n2] = ar * tw_r - ai * tw_i
        bf_ref[rows, n2:] = ar * tw_i + ai * tw_r
    x = _dot3c(bs_ref[...], f2_h[...], f2_l[...])
    hf = _dot3c(bf_ref[...], f2_h[...], f2_l[...])
    xr, xi = x[:, :n2], x[:, n2:]
    hr, hi = hf[:, :n2], hf[:, n2:]
    y = jnp.concatenate([xr * hr - xi * hi, xr * hi + xi * hr], axis=1)
    cm_ref[...] = _dot3c(y, g2_h[...], g2_l[...])
    half = FFT_N1 // 2
    for c in range(CONV_CB):
        rows = slice(c * FFT_N1, (c + 1) * FFT_N1)
        cr = cm_ref[rows, :n2]
        ci = cm_ref[rows, n2:]
        dc = jnp.concatenate([cr * tw_r + ci * tw_i, ci * tw_r - cr * tw_i], axis=0)
        yc = _cdot3(e1_h[...], e1_l[...], dc)
        bias = bias_ref[layer, blk * CONV_CB + c]
        y_ref[0, c] = yc[:half] + z_ref[0, c] * bias
        y_ref[1, c] = yc[half:] + z_ref[1, c] * bias


def _lconv_consts():
    n1, n2 = FFT_N1, FFT_N2
    n = n1 * n2
    idx = np.arange(128)
    ang1 = 2.0 * np.pi * ((idx[:, None] * idx[None, :]) % 128) / 128.0
    fr, fi = np.cos(ang1), -np.sin(ang1)
    angt = 2.0 * np.pi * (idx[:, None] * idx[None, :]) / n
    tw_r, tw_i = np.cos(angt), -np.sin(angt)
    h = n1 // 2
    f1d = np.block([[fr[:, :h], -fi[:, :h]], [fi[:, :h], fr[:, :h]]])
    f1f = np.concatenate([fr, fi], axis=0)
    f2 = np.block([[fr, fi], [-fi, fr]])
    g2 = np.block([[fr, -fi], [fi, fr]])
    er, ei = fr[:h] / n, -fi[:h] / n
    e1 = np.block([[er, -ei], [ei, er]])

    return (*_hl(f1d), *_hl(f1f), np.asarray(tw_r, np.float32), np.asarray(tw_i, np.float32),
            *_hl(f2), *_hl(g2), *_hl(e1))


def _long_conv_lat(zvt, filt_t, l1, bias, l):
    b_, C, L = zvt.shape
    assert b_ == 2 and L == FFT_N1 * FFT_N2 // 2
    z4 = zvt.reshape(b_, C, FFT_N1 // 2, FFT_N2)
    f3 = filt_t.reshape(C, FFT_N1, FFT_N2)
    consts = _lconv_consts()
    cb = CONV_CB

    def cspec(a):
        return pl.BlockSpec(a.shape, lambda i: (0,) * a.ndim)

    y = pl.pallas_call(
        functools.partial(_lconv_kernel, layer=l),
        grid=(C // cb,),
        in_specs=[pl.BlockSpec(memory_space=pltpu.SMEM), pl.BlockSpec(memory_space=pltpu.SMEM),
                  pl.BlockSpec((b_, cb, FFT_N1 // 2, FFT_N2), lambda i: (0, i, 0, 0)),
                  pl.BlockSpec((cb, FFT_N1, FFT_N2), lambda i: (i, 0, 0))] + [cspec(a) for a in consts],
        out_specs=pl.BlockSpec((b_, cb, FFT_N1 // 2, FFT_N2), lambda i: (0, i, 0, 0)),
        out_shape=jax.ShapeDtypeStruct(z4.shape, F32),
        scratch_shapes=[pltpu.VMEM((cb * FFT_N1, 2 * FFT_N2), F32),
                        pltpu.VMEM((cb * FFT_N1, 2 * FFT_N2), F32),
                        pltpu.VMEM((cb * FFT_N1, 2 * FFT_N2), F32)],
        compiler_params=_cparams(("arbitrary",)),
        name="long_conv",
    )(bias, l1, z4, f3, *consts)
    return y.reshape(b_, C, L)


def _cconv_kernel(z_ref, f_ref, l1_ref, bias_ref, fd_h, fd_l, ff_h, ff_l, fi_h, fi_l, y_ref, *, L):
    zr, zi = z_ref[0], z_ref[1]
    x = _dot3c(jnp.concatenate([zr, zi], axis=1), fd_h[...], fd_l[...])
    hf = _dot3c(f_ref[...] * (1.0 / l1_ref[...]), ff_h[...], ff_l[...])
    n = 2 * L
    xr, xi = x[:, :n], x[:, n:]
    hr, hi = hf[:, :n], hf[:, n:]
    y = jnp.concatenate([xr * hr - xi * hi, xr * hi + xi * hr], axis=1)
    out = _dot3c(y, fi_h[...], fi_l[...])
    bias = bias_ref[...]
    y_ref[0] = out[:, :L] + zr * bias
    y_ref[1] = out[:, L:] + zi * bias


def _cconv_consts(L):
    n = 2 * L
    idx = np.arange(n)
    ang = 2.0 * np.pi * ((idx[:, None] * idx[None, :]) % n) / n
    fr, fi = np.cos(ang), -np.sin(ang)
    fd = np.block([[fr[:L], fi[:L]], [-fi[:L], fr[:L]]])
    ff = np.concatenate([fr, fi], axis=1)
    er, ei = fr[:, :L] / n, -fi[:, :L] / n
    finv = np.block([[er, ei], [-ei, er]])

    return (*_hl(fd), *_hl(ff), *_hl(finv))


def _long_conv_ctx(zvt, filt_t, l1, bias):
    b_, C, L = zvt.shape
    consts = _cconv_consts(L)
    bias_b = jnp.broadcast_to(bias[:, None], (C, L))
    l1_col = l1.reshape(C, 1)

    def cspec(a):
        return pl.BlockSpec(a.shape, lambda i: (0,) * a.ndim)

    return pl.pallas_call(
        functools.partial(_cconv_kernel, L=L),
        grid=(1,),
        in_specs=[cspec(zvt), cspec(filt_t), cspec(l1_col), cspec(bias_b)] + [cspec(a) for a in consts],
        out_specs=cspec(zvt),
        out_shape=jax.ShapeDtypeStruct(zvt.shape, F32),
        compiler_params=_cparams(("arbitrary",)),
        name="long_conv_ctx",
    )(zvt, filt_t, l1_col, bias_b, *consts)


def _rms(x, g):
    ms = jnp.mean(x * x, axis=-1, keepdims=True)
    return x * lax.rsqrt(ms + EPS) * g


def _qkv_kernel(p_ref, gq_ref, gkv_ref, wq_ref, wqr_ref, wk_ref, wv_ref,
                cq_ref, sq_ref, ck_ref, sk_ref, q_ref, k_ref, v_ref):
    blk = p_ref[0]
    cq = blk[:, :Q_LORA].astype(F32)
    ckv = blk[:, Q_LORA:Q_LORA + KV_LORA].astype(F32)
    o = Q_LORA + KV_LORA
    kpe = blk[:, o:o + LANE].astype(F32)
    kpe_rot = blk[:, o + LANE:o + 2 * LANE].astype(F32)
    qn = _rms(cq, gq_ref[...]).astype(BF16)
    q = _mm(qn, wq_ref[...])
    qr = _mm(qn, wqr_ref[...])
    kvn = _rms(ckv, gkv_ref[...]).astype(BF16)
    k = _mm(kvn, wk_ref[...])
    v = _mm(kvn, wv_ref[...])
    cq_t, sq_t = cq_ref[...], sq_ref[...]
    kpe_r = kpe * ck_ref[...] + kpe_rot * sk_ref[...]
    lane = lax.broadcasted_iota(jnp.int32, (1, HEAD_PAD), 1)
    for h in range(MLA_HEADS):
        sl = slice(h * HEAD_PAD, (h + 1) * HEAD_PAD)
        q_ref[0, h] = (q[:, sl] * cq_t + qr[:, sl] * sq_t).astype(q_ref.dtype)
        k_ref[0, h] = (k[:, sl] + kpe_r).astype(k_ref.dtype)
        ones_col = (lane == (V_HEAD if h % 2 == 0 else 0)).astype(F32)
        v_ref[0, h] = (v[:, sl] + ones_col).astype(v_ref.dtype)


def _qkv(proj, gq, gkv, wq, wqr, wk, wv, cq, sq, ck, sk, l, tt):
    b_, L, _ = proj.shape
    layer = lambda b, i: (l, 0, 0)
    hw = MLA_HEADS * HEAD_PAD
    return pl.pallas_call(
        _qkv_kernel,
        grid=(b_, L // tt),
        in_specs=[pl.BlockSpec((1, tt, 1024), lambda b, i: (b, i, COL_QKV // 1024)),
                  pl.BlockSpec((None, 1, Q_LORA), layer), pl.BlockSpec((None, 1, KV_LORA), layer),
                  pl.BlockSpec((None, Q_LORA, hw), layer), pl.BlockSpec((None, Q_LORA, hw), layer),
                  pl.BlockSpec((None, KV_LORA, hw), layer), pl.BlockSpec((None, KV_LORA, hw), layer),
                  pl.BlockSpec((tt, LANE), lambda b, i: (i, 0)), pl.BlockSpec((tt, LANE), lambda b, i: (i, 0)),
                  pl.BlockSpec((tt, LANE), lambda b, i: (i, 0)), pl.BlockSpec((tt, LANE), lambda b, i: (i, 0))],
        out_specs=[pl.BlockSpec((1, MLA_HEADS, tt, HEAD_PAD), lambda b, i: (b, 0, i, 0)),
                   pl.BlockSpec((1, MLA_HEADS, tt, HEAD_PAD), lambda b, i: (b, 0, i, 0)),
                   pl.BlockSpec((1, MLA_HEADS, tt, HEAD_PAD), lambda b, i: (b, 0, i, 0))],
        out_shape=[jax.ShapeDtypeStruct((b_, MLA_HEADS, L, HEAD_PAD), BF16)] * 3,
        compiler_params=_cparams(("arbitrary", "arbitrary")),
        name="qkv",
    )(proj, gq, gkv, wq, wqr, wk, wv, cq, sq, ck, sk)


def _attn_kernel(q_ref, *refs, chunks):
    o_ref = refs[-1]
    tq = q_ref.shape[2]
    q = [q_ref[0, hh] for hh in range(2)]
    m = [jnp.full((tq, 1), -1e30, F32) for _ in range(2)]
    acc = [jnp.zeros((tq, LANE), F32) for _ in range(2)]
    work = [(src, start, size, hh) for (src, start, size) in chunks for hh in range(2)]

    def scores(n):
        src, start, size, hh = work[n]
        k = refs[2 * src][0, hh, start:start + size, :]
        return lax.dot_general(q[hh], k, (((1,), (1,)), ((), ())), preferred_element_type=F32)

    s_next = scores(0)
    for n, (src, start, size, hh) in enumerate(work):
        s = s_next
        if n + 1 < len(work):
            s_next = scores(n + 1)
        v = refs[2 * src + 1][0, hh, start:start + size, :]
        m_new = jnp.maximum(m[hh], jnp.max(s, axis=-1, keepdims=True))
        p = jnp.exp2(s - m_new)
        alpha = jnp.exp2(m[hh] - m_new)
        acc[hh] = alpha * acc[hh] + _mm(p.astype(BF16), v)
        m[hh] = m_new
    lane = lax.broadcasted_iota(jnp.int32, (tq, LANE), 1)
    o0 = acc[0] / acc[0][:, V_HEAD:V_HEAD + 1]
    o1 = acc[1] / acc[1][:, 0:1]
    o_ref[0] = jnp.where(lane < V_HEAD, o0, o1).astype(o_ref.dtype)


def _attention(q, kv_sources, tq, tk):
    b_, H, L, _ = q.shape
    chunks = []
    in_specs = [pl.BlockSpec((1, 2, tq, HEAD_PAD), lambda b, hp, i: (b, hp, i, 0))]
    args = [q]
    for src, (k, v) in enumerate(kv_sources):
        S = k.shape[2]
        step = min(tk, S)
        chunks += [(src, start, step) for start in range(0, S, step)]
        in_specs += [pl.BlockSpec((1, 2, S, HEAD_PAD), lambda b, hp, i: (b, hp, 0, 0))] * 2
        args += [k, v]
    return pl.pallas_call(
        functools.partial(_attn_kernel, chunks=tuple(chunks)),
        grid=(b_, H // 2, L // tq),
        in_specs=in_specs,
        out_specs=pl.BlockSpec((1, tq, LANE), lambda b, hp, i: (b, i, hp)),
        out_shape=jax.ShapeDtypeStruct((b_, L, H * V_HEAD), BF16),
        compiler_params=_cparams(("arbitrary", "arbitrary", "arbitrary")),
        name="attention",
    )(*args)


def _mixout_kernel(a_ref, x0_ref, yt_ref, o_ref, ga_ref, gb_ref, gc_ref, h_ref, g1_ref,
                   wp_ref, wh_ref, wo_ref, wout_ref, out_ref):
    hy = (x0_ref[0].astype(F32) * yt_ref[0].T).astype(BF16)
    a = _mm(a_ref[0], wp_ref[...])
    b = _mm(hy, wh_ref[...])
    cm = _mm(o_ref[0], wo_ref[...])
    m = (jax.nn.sigmoid(ga_ref[0].astype(F32)) * a + jax.nn.sigmoid(gb_ref[0].astype(F32)) * b
         + jax.nn.sigmoid(gc_ref[0].astype(F32)) * cm)
    out_ref[0] = h_ref[0] + g1_ref[0] * _mm(m.astype(BF16), wout_ref[...])


def _mixout(a_pre, x0c, yt, o, proj, h, mod, mrow, wp, wh, wo, wout, l, tt):
    b_, L, _ = h.shape
    layer = lambda b, i: (l, 0, 0)
    row = lambda b, i: (b, i, 0)
    gcol = COL_GATE // D_MODEL
    return pl.pallas_call(
        _mixout_kernel,
        grid=(b_, L // tt),
        in_specs=[pl.BlockSpec((1, tt, POOL_WIDTH), row),
                  pl.BlockSpec((1, tt, HY_WIDTH), row),
                  pl.BlockSpec((1, HY_WIDTH, tt), lambda b, i: (b, 0, i)),
                  pl.BlockSpec((1, tt, MLA_HEADS * V_HEAD), row),
                  pl.BlockSpec((1, tt, D_MODEL), lambda b, i: (b, i, gcol)),
                  pl.BlockSpec((1, tt, D_MODEL), lambda b, i: (b, i, gcol + 1)),
                  pl.BlockSpec((1, tt, D_MODEL), lambda b, i: (b, i, gcol + 2)),
                  pl.BlockSpec((1, tt, D_MODEL), row),
                  pl.BlockSpec((1, 1, D_MODEL), lambda b, i: (mrow(b), 0, 2)),
                  pl.BlockSpec((None, POOL_WIDTH, D_MODEL), layer), pl.BlockSpec((None, HY_WIDTH, D_MODEL), layer),
                  pl.BlockSpec((None, MLA_HEADS * V_HEAD, D_MODEL), layer),
                  pl.BlockSpec((None, D_MODEL, D_MODEL), layer)],
        out_specs=pl.BlockSpec((1, tt, D_MODEL), row),
        out_shape=jax.ShapeDtypeStruct(h.shape, F32),
        compiler_params=_cparams(("arbitrary", "arbitrary")),
        name="mixout",
    )(a_pre, x0c, yt, o, proj, proj, proj, h, mod, wp, wh, wo, wout)


def _ffn_kernel(x_ref, sh_ref, sc_ref, g2_ref, gn_ref, w1_ref, w2_ref, fg_ref, o_ref, *, final, nsplit):
    x = x_ref[0]
    xn = _mod_norm(x, gn_ref[...], sc_ref[0], sh_ref[0]).astype(BF16)
    tf = D_FF // nsplit
    acc = None
    for k in range(nsplit):
        u = jnp.maximum(_mm(xn, w1_ref[:, k * tf:(k + 1) * tf]), 0.0)
        part = _mm((u * u).astype(BF16), w2_ref[k * tf:(k + 1) * tf, :])
        acc = part if acc is None else acc + part
    y = x + g2_ref[0] * acc
    if final:
        y = _rms(y, fg_ref[...])
    o_ref[0] = y


def _ffn(h, mod, mrow, gn, w1, w2, final_g, l, tt, final):
    b_, L, _ = h.shape
    layer = lambda b, i: (l, 0, 0)
    row = lambda b, i: (b, i, 0)
    resident = dict(pipeline_mode=pl.Buffered(1))
    return pl.pallas_call(
        functools.partial(_ffn_kernel, final=final, nsplit=2),
        grid=(b_, L // tt),
        in_specs=[pl.BlockSpec((1, tt, D_MODEL), row),
                  pl.BlockSpec((1, 1, D_MODEL), lambda b, i: (mrow(b), 0, 3)),
                  pl.BlockSpec((1, 1, D_MODEL), lambda b, i: (mrow(b), 0, 4)),
                  pl.BlockSpec((1, 1, D_MODEL), lambda b, i: (mrow(b), 0, 5)),
                  pl.BlockSpec((None, 1, D_MODEL), layer),
                  pl.BlockSpec((None, D_MODEL, D_FF), layer, **resident),
                  pl.BlockSpec((None, D_FF, D_MODEL), layer, **resident),
                  pl.BlockSpec((1, D_MODEL), lambda b, i: (0, 0))],
        out_specs=pl.BlockSpec((1, tt, D_MODEL), row),
        out_shape=jax.ShapeDtypeStruct(h.shape, F32),
        compiler_params=_cparams(("arbitrary", "arbitrary")),
        name="ffn",
    )(h, mod, mod, mod, gn, w1, w2, final_g)


def _rope_tables(n_rows):
    rows = np.repeat(np.arange(n_rows, dtype=np.float64), GRID_W)
    cols = np.tile(np.arange(GRID_W, dtype=np.float64), n_rows)
    half = QK_ROPE // 2
    inv = ROPE_BASE ** (-np.arange(0, half, 2, dtype=np.float64) / half)
    ar = rows[:, None] * inv
    ac = cols[:, None] * inv
    ang = np.concatenate([ar, ar, ac, ac], axis=-1)
    return np.cos(ang), np.sin(ang)


def _head_tables(cos, sin, L):
    ones = np.ones((L, QK_NOPE))
    z64 = np.zeros((L, QK_NOPE))
    z32 = np.zeros((L, HEAD_PAD - QK_DIM))
    if cos is None:
        cos = np.ones((L, QK_ROPE))
        sin = np.zeros((L, QK_ROPE))
    cq = np.concatenate([ones, cos, z32], axis=1) * (MLA_SCALE * LOG2E)
    sq = np.concatenate([z64, sin, z32], axis=1) * (MLA_SCALE * LOG2E)
    ck = np.concatenate([z64, cos, z32], axis=1)
    sk = np.concatenate([z64, sin, z32], axis=1)
    return tuple(np.asarray(t, np.float32) for t in (cq, sq, ck, sk))


def _hy_embed(L, tr):
    t = np.linspace(0.0, 1.0, L)[:, None]
    omega = 2.0 * np.pi * np.arange(L)[:, None] / L
    bands = np.linspace(1e-4, HY_BANDS - 1, HY_BANDS)[None, :]
    z = np.concatenate([t, np.cos(omega * bands), -np.sin(omega * bands)], axis=-1)
    zfull = np.concatenate([z, np.zeros((1, HY_EMB)), z[:0:-1]], axis=0)
    zfull = np.pad(zfull, ((0, 0), (0, HY_HALF - HY_EMB)))
    packed = zfull.reshape(2 * L // tr, 2, tr // 2, HY_HALF).transpose(0, 2, 1, 3).reshape(L, 2 * HY_HALF)
    return np.asarray(packed, np.float32)


def _blockdiag2(a):
    a = _pad2(a, HY_HALF, HY_HALF)
    z = jnp.zeros_like(a)
    return jnp.concatenate([jnp.concatenate([a, z], axis=-1), jnp.concatenate([z, a], axis=-1)], axis=-2)


def _pack_w_in(w):
    w = w.astype(BF16)
    z = lambda n: jnp.zeros(w.shape[:-1] + (n,), BF16)
    o_q = 512 + 3 * HY_WIDTH
    o_kv = o_q + Q_LORA
    o_pe = o_kv + KV_LORA
    o_gate = o_pe + QK_ROPE
    kpe = w[..., o_pe:o_gate]
    return jnp.concatenate([
        w[..., :o_pe],
        z(QK_NOPE), kpe, z(HEAD_PAD - QK_DIM),
        z(QK_NOPE), _rot_half(kpe), z(HEAD_PAD - QK_DIM),
        z(LANE),
        w[..., o_gate:]], axis=-1)


def _pack_heads(w_uq, w_ukv):
    lead = w_uq.shape[:-2]
    wq3 = w_uq.astype(BF16).reshape(*lead, Q_LORA, MLA_HEADS, QK_DIM)
    pad = jnp.zeros((*lead, Q_LORA, MLA_HEADS, HEAD_PAD - QK_DIM), BF16)
    wq = jnp.concatenate([wq3, pad], axis=-1).reshape(*lead, Q_LORA, -1)
    wqr = jnp.concatenate([jnp.zeros((*lead, Q_LORA, MLA_HEADS, QK_NOPE), BF16), _rot_half(wq3[..., QK_NOPE:]), pad],
                          axis=-1).reshape(*lead, Q_LORA, -1)
    wkv3 = w_ukv.astype(BF16).reshape(*lead, KV_LORA, MLA_HEADS, QK_NOPE + V_HEAD)
    wk = jnp.concatenate([wkv3[..., :QK_NOPE], jnp.zeros((*lead, KV_LORA, MLA_HEADS, HEAD_PAD - QK_NOPE), BF16)],
                         axis=-1).reshape(*lead, KV_LORA, -1)
    zv = jnp.zeros((*lead, KV_LORA, MLA_HEADS // 2, V_HEAD), BF16)
    v3 = wkv3[..., QK_NOPE:]
    wv = jnp.stack([jnp.concatenate([v3[..., 0::2, :], zv], axis=-1),
                    jnp.concatenate([zv, v3[..., 1::2, :]], axis=-1)], axis=-2).reshape(*lead, KV_LORA, -1)
    return wq, wqr, wk, wv


def _pad2(a, r, c):
    pads = [(0, 0)] * (a.ndim - 2) + [(0, r - a.shape[-2]), (0, c - a.shape[-1])]
    return jnp.pad(a, pads)


TT_LAT = 512
TQ = 1024
TK = 512
TR_LAT = 1024


def kernel(x, c, ctx, c_ctx, w_mod, b_mod, norm1_g, norm2_g, w_in, pool_w, pool_scale, pool_out, hy_conv_w, hy_conv_b, hy_f_w1, hy_f_b1, hy_f_freq1, hy_f_w2, hy_f_b2, hy_f_freq2, hy_f_w3, hy_decay, hy_bias, hy_out, q_norm_g, w_uq, kv_norm_g, w_ukv, w_o, w_out, w_ff1, w_ff2, final_g):
    B, L, _ = x.shape
    Lc = ctx.shape[1]
    cos, sin = _rope_tables(L // GRID_W)
    tabs_lat = _head_tables(cos, sin, L)
    tabs_ctx = _head_tables(None, None, Lc)
    z_lat = _hy_embed(L, TR_LAT)
    z_ctx = _hy_embed(Lc, Lc)

    c_all = jnp.concatenate([c, c_ctx[None], jnp.zeros((MOD_ROWS - B - 1, D_MODEL), F32)], axis=0)
    mod = _modulation(c_all, w_mod, b_mod).reshape(DEPTH * MOD_ROWS, 1, N_MOD * D_MODEL)

    row3 = lambda a: a[:, None, :]
    g1n, g2n, gq, gkv = row3(norm1_g), row3(norm2_g), row3(q_norm_g), row3(kv_norm_g)
    w_in_p = _pack_w_in(w_in)
    wq, wqr, wk, wv = _pack_heads(w_uq, w_ukv)
    pw, ps = pool_w.astype(BF16), row3(pool_scale)
    cw, cb = hy_conv_w, row3(hy_conv_b)
    wp, wh, wo, wout = pool_out.astype(BF16), hy_out.astype(BF16), w_o.astype(BF16), w_out.astype(BF16)
    wf1, wf2 = w_ff1.astype(BF16), w_ff2.astype(BF16)
    twice = lambda v: jnp.tile(_pad2(row3(v), 1, HY_HALF), (1, 1, 2))
    fw1, fw2 = _blockdiag2(hy_f_w1), _blockdiag2(hy_f_w2)
    fb1, ff1, fb2, ff2 = twice(hy_f_b1), twice(hy_f_freq1), twice(hy_f_b2), twice(hy_f_freq2)
    w3d = hy_f_w3.reshape(DEPTH, HY_HALF, 2, HY_WIDTH).transpose(0, 2, 1, 3)
    z3 = jnp.zeros_like(w3d)
    fw3 = jnp.stack([jnp.concatenate([w3d, z3], axis=2), jnp.concatenate([z3, w3d], axis=2)], axis=2)
    dec = hy_decay[:, :, None, :]
    fg = final_g[None]

    h, hc = x, ctx
    for l in range(DEPTH):
        last = l == DEPTH - 1
        row_lat = lambda b, l=l: l * MOD_ROWS + b
        row_ctx = lambda b, l=l: l * MOD_ROWS + B

        def stream_pre(hs, mrow, tabs, tt):
            proj = _inproj(hs, mod, mrow, g1n, w_in_p, l, tt)
            q, k, v = _qkv(proj, gq, gkv, wq, wqr, wk, wv, *tabs, l, tt)
            return proj, q, k, v

        def stream_mix(hs, mrow, proj, q, kv_sources, zemb, tt, tq, tk, tr):
            Ls = hs.shape[1]
            a_pre, x0c, zvt = _seqlocal(proj, pw, ps, cw, cb, l, tt)
            filt_t, l1 = _hyena_filter(zemb, fw1, fb1, ff1, fw2, fb2, ff2, fw3, dec, l, Ls, tr)
            if Ls == FFT_N1 * FFT_N2 // 2:
                yt = _long_conv_lat(zvt, filt_t, l1, hy_bias, l)
            else:
                yt = _long_conv_ctx(zvt, filt_t, l1, hy_bias[l])
            o = _attention(q, kv_sources, tq, tk)
            return _mixout(a_pre, x0c, yt, o, proj, hs, mod, mrow, wp, wh, wo, wout, l, tt)

        proj_c, q_c, k_c, v_c = stream_pre(hc, row_ctx, tabs_ctx, Lc)
        proj_l, q_l, k_l, v_l = stream_pre(h, row_lat, tabs_lat, TT_LAT)
        h = stream_mix(h, row_lat, proj_l, q_l, [(k_c, v_c), (k_l, v_l)], z_lat, TT_LAT, TQ, TK, TR_LAT)
        h = _ffn(h, mod, row_lat, g2n, wf1, wf2, fg, l, TT_LAT, last)
        if not last:
            hc = stream_mix(hc, row_ctx, proj_c, q_c, [(k_c, v_c)], z_ctx, Lc, Lc, Lc, Lc)
            hc = _ffn(hc, mod, row_ctx, g2n, wf1, wf2, fg, l, Lc, False)
    return h
```

```python
import functools
import math

import jax
import jax.numpy as jnp
import numpy as np
from jax import lax
from jax.experimental import pallas as pl
from jax.experimental.pallas import tpu as pltpu

F32 = jnp.float32
BF16 = jnp.bfloat16

D_MODEL = 1024
DEPTH = 4
GRID_W = 64
EPS = 1e-6

POOL_WIDTH = 512
POOL_WINDOWS = (2, 4, 8, 16)
POOL_GROUP = 128
HY_WIDTH = 512
HY_EMB = 33
HY_BANDS = 16
HY_HALF = 64
MLA_HEADS = 8
QK_NOPE = 64
QK_ROPE = 32
QK_DIM = 96
V_HEAD = 64
Q_LORA = 384
KV_LORA = 256
MLA_SCALE = QK_DIM ** -0.5
LOG2E = math.log2(math.e)
ROPE_BASE = 10000.0
D_FF = 4 * D_MODEL
N_MOD = 6
MOD_ROWS = 8

LANE = 128
HALO = 16
HEAD_PAD = 128

COL_POOL = 0
COL_HY = 512
COL_QKV = 2048
COL_GATE = 3072
N_PROJ = 6144

VMEM_LIMIT = 50 * 1024 * 1024
VMEM_LIMIT_FUSED = 58 * 1024 * 1024


def _cparams(sem):
    return pltpu.CompilerParams(dimension_semantics=sem, vmem_limit_bytes=VMEM_LIMIT)


def _mm(a, b):
    return jnp.dot(a, b, preferred_element_type=F32)


def _split(x):
    hi = x.astype(BF16)
    lo = (x - hi.astype(F32)).astype(BF16)
    return hi, lo


def _dot3(a, b):
    ah, al = _split(a)
    bh, bl = _split(b)
    return _mm(ah, bh) + _mm(al, bh) + _mm(ah, bl)


def _dot3c(a, bh, bl):
    ah, al = _split(a)
    return _mm(ah, bh) + _mm(al, bh) + _mm(ah, bl)


def _cdot3(ch, cl, b):
    bh, bl = _split(b)
    n = b.shape[1]
    r = _mm(ch, jnp.concatenate([bh, bl], axis=1))
    return r[:, :n] + r[:, n:] + _mm(cl, bh)


def _hl(m):
    m32 = np.asarray(m, np.float32)
    hi = m32.astype(BF16)
    lo = (m32 - hi.astype(np.float32)).astype(BF16)
    return hi, lo


def _rot_half(x):
    q = QK_ROPE // 4
    return jnp.concatenate([-x[..., q:2 * q], x[..., :q], -x[..., 3 * q:], x[..., 2 * q:3 * q]], axis=-1)


def _mod_kernel(c_ref, w_ref, b_ref, o_ref):
    c = c_ref[...]
    s = c * jax.nn.sigmoid(c)
    o_ref[0] = _dot3(s, w_ref[0]) + b_ref[0]


def _modulation(c_all, w_mod, b_mod):
    tn = 1536
    n = N_MOD * D_MODEL
    return pl.pallas_call(
        _mod_kernel,
        grid=(DEPTH, n // tn),
        in_specs=[pl.BlockSpec((MOD_ROWS, D_MODEL), lambda l, j: (0, 0)),
                  pl.BlockSpec((1, D_MODEL, tn), lambda l, j: (l, 0, j)),
                  pl.BlockSpec((1, 1, tn), lambda l, j: (l, 0, j))],
        out_specs=pl.BlockSpec((1, MOD_ROWS, tn), lambda l, j: (l, 0, j)),
        out_shape=jax.ShapeDtypeStruct((DEPTH, MOD_ROWS, n), F32),
        compiler_params=_cparams(("arbitrary", "arbitrary")),
        name="modulation",
    )(c_all, w_mod, b_mod.reshape(DEPTH, 1, n))


def _mod_norm(x, g, sc, sh):
    ms = jnp.mean(x * x, axis=-1, keepdims=True)
    return (x * lax.rsqrt(ms + EPS) * g) * (1.0 + sc) + sh


def _inproj_kernel(x_ref, sh_ref, sc_ref, g_ref, *refs):
    w_refs, o_ref = refs[:-1], refs[-1]
    xn = _mod_norm(x_ref[0], g_ref[...], sc_ref[0], sh_ref[0]).astype(BF16)
    col = 0
    for w_ref in w_refs:
        n = w_ref.shape[1]
        o_ref[0, :, col:col + n] = _mm(xn, w_ref[...]).astype(o_ref.dtype)
        col += n


def _inproj(h, mod, mrow, g, ws, l, tt):
    b_, L, _ = h.shape
    assert sum(w.shape[2] for w in ws) == N_PROJ
    w_specs = [pl.BlockSpec((None, D_MODEL, w.shape[2]), lambda b, i: (l, 0, 0), pipeline_mode=pl.Buffered(1))
               for w in ws]
    return pl.pallas_call(
        _inproj_kernel,
        grid=(b_, L // tt),
        in_specs=[pl.BlockSpec((1, tt, D_MODEL), lambda b, i: (b, i, 0)),
                  pl.BlockSpec((1, 1, D_MODEL), lambda b, i: (mrow(b), 0, 0)),
                  pl.BlockSpec((1, 1, D_MODEL), lambda b, i: (mrow(b), 0, 1)),
                  pl.BlockSpec((None, 1, D_MODEL), lambda b, i: (l, 0, 0))] + w_specs,
        out_specs=pl.BlockSpec((1, tt, N_PROJ), lambda b, i: (b, i, 0)),
        out_shape=jax.ShapeDtypeStruct((b_, L, N_PROJ), BF16),
        compiler_params=_cparams(("arbitrary", "arbitrary")),
        name="inproj",
    )(h, mod, mod, g, *ws)


def _seqlocal_kernel(cur_ref, prev_ref, next_ref, pw_ref, ps_ref, cw_ref, cb_ref,
                     a_ref, x0_ref, zvt_ref, *, tt, L):
    i = pl.program_id(1)
    nt = pl.num_programs(1)
    keep_prev = (i > 0).astype(F32)
    keep_next = (i < nt - 1).astype(F32)
    n = tt + 2 * HALO

    def ext(lo, hi):
        return jnp.concatenate([prev_ref[0, :, lo:hi].astype(F32) * keep_prev,
                                cur_ref[0, :, lo:hi].astype(F32),
                                next_ref[0, :, lo:hi].astype(F32) * keep_next], axis=0)

    def roll(x, s):
        return pltpu.roll(x, s % n, 0)

    t = i * tt + lax.broadcasted_iota(jnp.int32, (tt, POOL_GROUP), 0)
    for gi, w in enumerate(POOL_WINDOWS):
        lo = COL_POOL + gi * POOL_GROUP
        xg = ext(lo, lo + POOL_GROUP)
        s = xg + roll(xg, 1)
        if w >= 4:
            s = roll(s, 1) + roll(s, -1)
        if w >= 8:
            s = roll(s, 2) + roll(s, -2)
        if w >= 16:
            s = roll(s, 4) + roll(s, -4)
        cnt = (jnp.minimum(t + w // 2, L) - jnp.maximum(t - w // 2, 0)).astype(F32)
        u = s[HALO:HALO + tt] / cnt - xg[HALO:HALO + tt]
        ug = _mm(u.astype(BF16), pw_ref[gi]) * ps_ref[:, gi * POOL_GROUP:(gi + 1) * POOL_GROUP]
        a_ref[0, :, gi * POOL_GROUP:(gi + 1) * POOL_GROUP] = ug.astype(a_ref.dtype)

    def conv(part, j):
        c0 = part * HY_WIDTH + j * LANE
        xe = ext(COL_HY + c0, COL_HY + c0 + LANE)
        y = (roll(xe, 1) * cw_ref[0:1, c0:c0 + LANE] + xe * cw_ref[1:2, c0:c0 + LANE]
             + roll(xe, -1) * cw_ref[2:3, c0:c0 + LANE] + cb_ref[:, c0:c0 + LANE])
        return y[HALO:HALO + tt]

    for j in range(HY_WIDTH // LANE):
        x0_ref[0, :, j * LANE:(j + 1) * LANE] = conv(0, j).astype(x0_ref.dtype)
        zv = conv(2, j) * conv(1, j)
        zvt_ref[0, j * LANE:(j + 1) * LANE, :] = zv.T


def _seqlocal(proj, pool_w, pool_scale, conv_w, conv_b, l, tt):
    b_, L, _ = proj.shape
    hb = tt // HALO
    nh = L // HALO
    wide = COL_QKV
    return pl.pallas_call(
        functools.partial(_seqlocal_kernel, tt=tt, L=L),
        grid=(b_, L // tt),
        in_specs=[pl.BlockSpec((1, tt, wide), lambda b, i: (b, i, 0)),
                  pl.BlockSpec((1, HALO, wide), lambda b, i: (b, jnp.maximum(i * hb - 1, 0), 0)),
                  pl.BlockSpec((1, HALO, wide), lambda b, i: (b, jnp.minimum((i + 1) * hb, nh - 1), 0)),
                  pl.BlockSpec((None, len(POOL_WINDOWS), POOL_GROUP, POOL_GROUP), lambda b, i: (l, 0, 0, 0)),
                  pl.BlockSpec((None, 1, POOL_WIDTH), lambda b, i: (l, 0, 0)),
                  pl.BlockSpec((None, 3, 3 * HY_WIDTH), lambda b, i: (l, 0, 0)),
                  pl.BlockSpec((None, 1, 3 * HY_WIDTH), lambda b, i: (l, 0, 0))],
        out_specs=[pl.BlockSpec((1, tt, POOL_WIDTH), lambda b, i: (b, i, 0)),
                   pl.BlockSpec((1, tt, HY_WIDTH), lambda b, i: (b, i, 0)),
                   pl.BlockSpec((1, HY_WIDTH, tt), lambda b, i: (b, 0, i))],
        out_shape=[jax.ShapeDtypeStruct((b_, L, POOL_WIDTH), BF16),
                   jax.ShapeDtypeStruct((b_, L, HY_WIDTH), BF16),
                   jax.ShapeDtypeStruct((b_, HY_WIDTH, L), F32)],
        compiler_params=_cparams(("arbitrary", "arbitrary")),
        name="seqlocal",
    )(proj, proj, proj, pool_w, pool_scale, conv_w, conv_b)


def _filter_kernel(z_ref, w1_ref, b1_ref, f1_ref, w2_ref, b2_ref, f2_ref, w3_ref, dec_ref,
                   filt_ref, l1_ref, *, tr, L):
    i = pl.program_id(0)
    half = tr // 2
    z = z_ref[...]
    hid = jnp.sin(f1_ref[...] * (_dot3(z, w1_ref[...]) + b1_ref[...]))
    hid = jnp.sin(f2_ref[...] * (_dot3(hid, w2_ref[...]) + b2_ref[...]))
    decay = jnp.abs(dec_ref[0])
    total = None
    for s in range(2):
        h = _dot3(hid, w3_ref[0, s]) * jnp.exp(-z[:, s * HY_HALF:s * HY_HALF + 1] * decay)
        row = i * tr + s * half + lax.broadcasted_iota(jnp.int32, h.shape, 0)
        h = jnp.where(row == L, 0.0, h)
        filt_ref[:, s * half:(s + 1) * half] = h.T
        part = jnp.sum(jnp.abs(h), axis=0, keepdims=True)
        total = part if total is None else total + part

    @pl.when(i == 0)
    def _():
        l1_ref[...] = jnp.zeros_like(l1_ref)

    l1_ref[...] += total


def _hyena_filter(zpacked, w1, b1, f1, w2, b2, f2, w3, dec, l, L, tr):
    n = 2 * L
    nhalf = L // tr
    full = lambda i: (0, 0)
    mat = pl.BlockSpec((None, LANE, LANE), lambda i: (l, 0, 0))
    vec = pl.BlockSpec((None, 1, LANE), lambda i: (l, 0, 0))
    return pl.pallas_call(
        functools.partial(_filter_kernel, tr=tr, L=L),
        grid=(n // tr,),
        in_specs=[pl.BlockSpec((tr // 2, LANE), lambda i: (i, 0)),
                  mat, vec, vec, mat, vec, vec,
                  pl.BlockSpec((None, 1, 2, LANE, HY_WIDTH), lambda i: (l, i // nhalf, 0, 0, 0)),
                  pl.BlockSpec((None, 1, 1, HY_WIDTH), lambda i: (l, i // nhalf, 0, 0))],
        out_specs=[pl.BlockSpec((HY_WIDTH, tr), lambda i: (0, i)),
                   pl.BlockSpec((1, HY_WIDTH), full)],
        out_shape=[jax.ShapeDtypeStruct((HY_WIDTH, n), F32),
                   jax.ShapeDtypeStruct((1, HY_WIDTH), F32)],
        compiler_params=_cparams(("arbitrary",)),
        name="hyena_filter",
    )(zpacked, w1, b1, f1, w2, b2, f2, w3, dec)


FFT_N1 = 128
FFT_N2 = 128
CONV_CB = 16


def _lconv_kernel(bias_ref, l1_ref, z_ref, f_ref, f1d_h, f1d_l, f1f_h, f1f_l, tr_ref, ti_ref,
                  f2_h, f2_l, g2_h, g2_l, e1_h, e1_l, y_ref, bs_ref, bf_ref, cm_ref, ys_ref, *, layer):
    blk = pl.program_id(0)
    tw_r = tr_ref[...]
    tw_i = ti_ref[...]
    n2 = FFT_N2
    for c in range(CONV_CB):
        rows = slice(c * FFT_N1, (c + 1) * FFT_N1)
        zc = jnp.concatenate([z_ref[0, c], z_ref[1, c]], axis=0)
        a = _cdot3(f1d_h[...], f1d_l[...], zc)
        ar, ai = a[:FFT_N1], a[FFT_N1:]
        bs_ref[rows, :n2] = ar * tw_r - ai * tw_i
        bs_ref[rows, n2:] = ar * tw_i + ai * tw_r
        inv_l1 = 1.0 / jnp.full((1, n2), l1_ref[0, blk * CONV_CB + c], F32)
        a = _cdot3(f1f_h[...], f1f_l[...], f_ref[c] * inv_l1)
        ar, ai = a[:FFT_N1], a[FFT_N1:]
        bf_ref[rows, :n2] = ar * tw_r - ai * tw_i
        bf_ref[rows, n2:] = ar * tw_i + ai * tw_r
    x = _dot3c(bs_ref[...], f2_h[...], f2_l[...])
    hf = _dot3c(bf_ref[...], f2_h[...], f2_l[...])
    xr, xi = x[:, :n2], x[:, n2:]
    hr, hi = hf[:, :n2], hf[:, n2:]
    y = jnp.concatenate([xr * hr - xi * hi, xr * hi + xi * hr], axis=1)
    cm_ref[...] = _dot3c(y, g2_h[...], g2_l[...])
    half = FFT_N1 // 2
    for c in range(CONV_CB):
        rows = slice(c * FFT_N1, (c + 1) * FFT_N1)
        cr = cm_ref[rows, :n2]
        ci = cm_ref[rows, n2:]
        dc = jnp.concatenate([cr * tw_r + ci * tw_i, ci * tw_r - cr * tw_i], axis=0)
        yc = _cdot3(e1_h[...], e1_l[...], dc)
        bias = bias_ref[layer, blk * CONV_CB + c]
        ys_ref[0, c] = yc[:half] + z_ref[0, c] * bias
        ys_ref[1, c] = yc[half:] + z_ref[1, c] * bias
    sub = 8
    for b in range(2):
        for cg in range(CONV_CB // sub):
            for ng in range(half // sub):
                blk8 = ys_ref[b, cg * sub:(cg + 1) * sub, ng * sub:(ng + 1) * sub, :]
                t8 = jnp.swapaxes(blk8, 0, 1)
                for j in range(sub):
                    n1 = ng * sub + j
                    y_ref[b, cg * sub:(cg + 1) * sub, n1 * FFT_N2:(n1 + 1) * FFT_N2] = t8[j]


def _lconv_consts():
    n1, n2 = FFT_N1, FFT_N2
    n = n1 * n2
    idx = np.arange(128)
    ang1 = 2.0 * np.pi * ((idx[:, None] * idx[None, :]) % 128) / 128.0
    fr, fi = np.cos(ang1), -np.sin(ang1)
    angt = 2.0 * np.pi * (idx[:, None] * idx[None, :]) / n
    tw_r, tw_i = np.cos(angt), -np.sin(angt)
    h = n1 // 2
    f1d = np.block([[fr[:, :h], -fi[:, :h]], [fi[:, :h], fr[:, :h]]])
    f1f = np.concatenate([fr, fi], axis=0)
    f2 = np.block([[fr, fi], [-fi, fr]])
    g2 = np.block([[fr, -fi], [fi, fr]])
    er, ei = fr[:h] / n, -fi[:h] / n
    e1 = np.block([[er, -ei], [ei, er]])

    return (*_hl(f1d), *_hl(f1f), np.asarray(tw_r, np.float32), np.asarray(tw_i, np.float32),
            *_hl(f2), *_hl(g2), *_hl(e1))


def _long_conv_lat(zvt, filt_t, l1, bias, l):
    b_, C, L = zvt.shape
    assert b_ == 2 and L == FFT_N1 * FFT_N2 // 2
    z4 = zvt.reshape(b_, C, FFT_N1 // 2, FFT_N2)
    f3 = filt_t.reshape(C, FFT_N1, FFT_N2)
    consts = _lconv_consts()
    cb = CONV_CB

    def cspec(a):
        return pl.BlockSpec(a.shape, lambda i: (0,) * a.ndim)

    return pl.pallas_call(
        functools.partial(_lconv_kernel, layer=l),
        grid=(C // cb,),
        in_specs=[pl.BlockSpec(memory_space=pltpu.SMEM), pl.BlockSpec(memory_space=pltpu.SMEM),
                  pl.BlockSpec((b_, cb, FFT_N1 // 2, FFT_N2), lambda i: (0, i, 0, 0)),
                  pl.BlockSpec((cb, FFT_N1, FFT_N2), lambda i: (i, 0, 0))] + [cspec(a) for a in consts],
        out_specs=pl.BlockSpec((b_, cb, L), lambda i: (0, i, 0)),
        out_shape=jax.ShapeDtypeStruct(zvt.shape, F32),
        scratch_shapes=[pltpu.VMEM((cb * FFT_N1, 2 * FFT_N2), F32),
                        pltpu.VMEM((cb * FFT_N1, 2 * FFT_N2), F32),
                        pltpu.VMEM((cb * FFT_N1, 2 * FFT_N2), F32),
                        pltpu.VMEM((b_, cb, FFT_N1 // 2, FFT_N2), F32)],
        compiler_params=_cparams(("arbitrary",)),
        name="long_conv",
    )(bias, l1, z4, f3, *consts)


def _cconv_kernel(z_ref, f_ref, l1_ref, bias_ref, fd_h, fd_l, ff_h, ff_l, fi_h, fi_l, y_ref, *, L):
    zr, zi = z_ref[0], z_ref[1]
    x = _dot3c(jnp.concatenate([zr, zi], axis=1), fd_h[...], fd_l[...])
    hf = _dot3c(f_ref[...] * (1.0 / l1_ref[...]), ff_h[...], ff_l[...])
    n = 2 * L
    xr, xi = x[:, :n], x[:, n:]
    hr, hi = hf[:, :n], hf[:, n:]
    y = jnp.concatenate([xr * hr - xi * hi, xr * hi + xi * hr], axis=1)
    out = _dot3c(y, fi_h[...], fi_l[...])
    bias = bias_ref[...]
    y_ref[0] = out[:, :L] + zr * bias
    y_ref[1] = out[:, L:] + zi * bias


def _cconv_consts(L):
    n = 2 * L
    idx = np.arange(n)
    ang = 2.0 * np.pi * ((idx[:, None] * idx[None, :]) % n) / n
    fr, fi = np.cos(ang), -np.sin(ang)
    fd = np.block([[fr[:L], fi[:L]], [-fi[:L], fr[:L]]])
    ff = np.concatenate([fr, fi], axis=1)
    er, ei = fr[:, :L] / n, -fi[:, :L] / n
    finv = np.block([[er, ei], [-ei, er]])

    return (*_hl(fd), *_hl(ff), *_hl(finv))


def _long_conv_ctx(zvt, filt_t, l1, bias):
    b_, C, L = zvt.shape
    consts = _cconv_consts(L)
    bias_b = jnp.broadcast_to(bias[:, None], (C, L))
    l1_col = l1.reshape(C, 1)

    def cspec(a):
        return pl.BlockSpec(a.shape, lambda i: (0,) * a.ndim)

    return pl.pallas_call(
        functools.partial(_cconv_kernel, L=L),
        grid=(1,),
        in_specs=[cspec(zvt), cspec(filt_t), cspec(l1_col), cspec(bias_b)] + [cspec(a) for a in consts],
        out_specs=cspec(zvt),
        out_shape=jax.ShapeDtypeStruct(zvt.shape, F32),
        compiler_params=_cparams(("arbitrary",)),
        name="long_conv_ctx",
    )(zvt, filt_t, l1_col, bias_b, *consts)


def _rms(x, g):
    ms = jnp.mean(x * x, axis=-1, keepdims=True)
    return x * lax.rsqrt(ms + EPS) * g


def _qkv_kernel(p_ref, gq_ref, gkv_ref, wq_ref, wqr_ref, wk_ref, wv_ref,
                cq_ref, sq_ref, ck_ref, sk_ref, q_ref, k_ref, v_ref):
    blk = p_ref[0]
    cq = blk[:, :Q_LORA].astype(F32)
    ckv = blk[:, Q_LORA:Q_LORA + KV_LORA].astype(F32)
    o = Q_LORA + KV_LORA
    kpe = blk[:, o:o + LANE].astype(F32)
    kpe_rot = blk[:, o + LANE:o + 2 * LANE].astype(F32)
    qn = _rms(cq, gq_ref[...]).astype(BF16)
    q = _mm(qn, wq_ref[...])
    qr = _mm(qn, wqr_ref[...])
    kvn = _rms(ckv, gkv_ref[...]).astype(BF16)
    k = _mm(kvn, wk_ref[...])
    v = _mm(kvn, wv_ref[...])
    cq_t, sq_t = cq_ref[...], sq_ref[...]
    kpe_r = kpe * ck_ref[...] + kpe_rot * sk_ref[...]
    lane = lax.broadcasted_iota(jnp.int32, (1, HEAD_PAD), 1)
    for h in range(MLA_HEADS):
        sl = slice(h * HEAD_PAD, (h + 1) * HEAD_PAD)
        q_ref[0, h] = (q[:, sl] * cq_t + qr[:, sl] * sq_t).astype(q_ref.dtype)
        k_ref[0, h] = (k[:, sl] + kpe_r).astype(k_ref.dtype)
        ones_col = (lane == (V_HEAD if h % 2 == 0 else 0)).astype(F32)
        v_ref[0, h] = (v[:, sl] + ones_col).astype(v_ref.dtype)


def _qkv(proj, gq, gkv, wq, wqr, wk, wv, cq, sq, ck, sk, l, tt):
    b_, L, _ = proj.shape
    layer = lambda b, i: (l, 0, 0)
    hw = MLA_HEADS * HEAD_PAD
    return pl.pallas_call(
        _qkv_kernel,
        grid=(b_, L // tt),
        in_specs=[pl.BlockSpec((1, tt, 1024), lambda b, i: (b, i, COL_QKV // 1024)),
                  pl.BlockSpec((None, 1, Q_LORA), layer), pl.BlockSpec((None, 1, KV_LORA), layer),
                  pl.BlockSpec((None, Q_LORA, hw), layer), pl.BlockSpec((None, Q_LORA, hw), layer),
                  pl.BlockSpec((None, KV_LORA, hw), layer), pl.BlockSpec((None, KV_LORA, hw), layer),
                  pl.BlockSpec((tt, LANE), lambda b, i: (i, 0)), pl.BlockSpec((tt, LANE), lambda b, i: (i, 0)),
                  pl.BlockSpec((tt, LANE), lambda b, i: (i, 0)), pl.BlockSpec((tt, LANE), lambda b, i: (i, 0))],
        out_specs=[pl.BlockSpec((1, MLA_HEADS, tt, HEAD_PAD), lambda b, i: (b, 0, i, 0)),
                   pl.BlockSpec((1, MLA_HEADS, tt, HEAD_PAD), lambda b, i: (b, 0, i, 0)),
                   pl.BlockSpec((1, MLA_HEADS, tt, HEAD_PAD), lambda b, i: (b, 0, i, 0))],
        out_shape=[jax.ShapeDtypeStruct((b_, MLA_HEADS, L, HEAD_PAD), BF16)] * 3,
        compiler_params=_cparams(("arbitrary", "arbitrary")),
        name="qkv",
    )(proj, gq, gkv, wq, wqr, wk, wv, cq, sq, ck, sk)


def _attn_kernel(q_ref, *refs, chunks):
    o_ref = refs[-1]
    tq = q_ref.shape[2]
    q = [q_ref[0, hh] for hh in range(2)]
    m = [jnp.full((tq, 1), -1e30, F32) for _ in range(2)]
    acc = [jnp.zeros((tq, LANE), F32) for _ in range(2)]
    work = [(src, start, size, hh) for (src, start, size) in chunks for hh in range(2)]

    def scores(n):
        src, start, size, hh = work[n]
        k = refs[2 * src][0, hh, start:start + size, :]
        return lax.dot_general(q[hh], k, (((1,), (1,)), ((), ())), preferred_element_type=F32)

    s_next = scores(0)
    for n, (src, start, size, hh) in enumerate(work):
        s = s_next
        if n + 1 < len(work):
            s_next = scores(n + 1)
        v = refs[2 * src + 1][0, hh, start:start + size, :]
        m_new = jnp.maximum(m[hh], jnp.max(s, axis=-1, keepdims=True))
        p = jnp.exp2(s - m_new)
        alpha = jnp.exp2(m[hh] - m_new)
        acc[hh] = alpha * acc[hh] + _mm(p.astype(BF16), v)
        m[hh] = m_new
    lane = lax.broadcasted_iota(jnp.int32, (tq, LANE), 1)
    o0 = acc[0] / acc[0][:, V_HEAD:V_HEAD + 1]
    o1 = acc[1] / acc[1][:, 0:1]
    o_ref[0] = jnp.where(lane < V_HEAD, o0, o1).astype(o_ref.dtype)


def _attention(q, kv_sources, tq, tk):
    b_, H, L, _ = q.shape
    chunks = []
    in_specs = [pl.BlockSpec((1, 2, tq, HEAD_PAD), lambda b, hp, i: (b, hp, i, 0))]
    args = [q]
    for src, (k, v) in enumerate(kv_sources):
        S = k.shape[2]
        step = min(tk, S)
        chunks += [(src, start, step) for start in range(0, S, step)]
        in_specs += [pl.BlockSpec((1, 2, S, HEAD_PAD), lambda b, hp, i: (b, hp, 0, 0))] * 2
        args += [k, v]
    return pl.pallas_call(
        functools.partial(_attn_kernel, chunks=tuple(chunks)),
        grid=(b_, H // 2, L // tq),
        in_specs=in_specs,
        out_specs=pl.BlockSpec((1, tq, LANE), lambda b, hp, i: (b, i, hp)),
        out_shape=jax.ShapeDtypeStruct((b_, L, H * V_HEAD), BF16),
        compiler_params=_cparams(("arbitrary", "arbitrary", "arbitrary")),
        name="attention",
    )(*args)


def _mixout_kernel(a_ref, x0_ref, yt_ref, o_ref, ga_ref, gb_ref, gc_ref, h_ref, g1_ref,
                   wp_ref, wh_ref, wo_ref, wout_ref, out_ref):
    hy = (x0_ref[0].astype(F32) * yt_ref[0].T).astype(BF16)
    a = _mm(a_ref[0], wp_ref[...])
    b = _mm(hy, wh_ref[...])
    cm = _mm(o_ref[0], wo_ref[...])
    m = (jax.nn.sigmoid(ga_ref[0].astype(F32)) * a + jax.nn.sigmoid(gb_ref[0].astype(F32)) * b
         + jax.nn.sigmoid(gc_ref[0].astype(F32)) * cm)
    out_ref[0] = h_ref[0] + g1_ref[0] * _mm(m.astype(BF16), wout_ref[...])


def _mixout(a_pre, x0c, yt, o, proj, h, mod, mrow, wp, wh, wo, wout, l, tt):
    b_, L, _ = h.shape
    layer = lambda b, i: (l, 0, 0)
    row = lambda b, i: (b, i, 0)
    gcol = COL_GATE // D_MODEL
    return pl.pallas_call(
        _mixout_kernel,
        grid=(b_, L // tt),
        in_specs=[pl.BlockSpec((1, tt, POOL_WIDTH), row),
                  pl.BlockSpec((1, tt, HY_WIDTH), row),
                  pl.BlockSpec((1, HY_WIDTH, tt), lambda b, i: (b, 0, i)),
                  pl.BlockSpec((1, tt, MLA_HEADS * V_HEAD), row),
                  pl.BlockSpec((1, tt, D_MODEL), lambda b, i: (b, i, gcol)),
                  pl.BlockSpec((1, tt, D_MODEL), lambda b, i: (b, i, gcol + 1)),
                  pl.BlockSpec((1, tt, D_MODEL), lambda b, i: (b, i, gcol + 2)),
                  pl.BlockSpec((1, tt, D_MODEL), row),
                  pl.BlockSpec((1, 1, D_MODEL), lambda b, i: (mrow(b), 0, 2)),
                  pl.BlockSpec((None, POOL_WIDTH, D_MODEL), layer), pl.BlockSpec((None, HY_WIDTH, D_MODEL), layer),
                  pl.BlockSpec((None, MLA_HEADS * V_HEAD, D_MODEL), layer),
                  pl.BlockSpec((None, D_MODEL, D_MODEL), layer)],
        out_specs=pl.BlockSpec((1, tt, D_MODEL), row),
        out_shape=jax.ShapeDtypeStruct(h.shape, F32),
        compiler_params=_cparams(("arbitrary", "arbitrary")),
        name="mixout",
    )(a_pre, x0c, yt, o, proj, proj, proj, h, mod, wp, wh, wo, wout)


def _ffn_kernel(x_ref, sh_ref, sc_ref, g2_ref, gn_ref, w1_ref, w2_ref, fg_ref, o_ref, *, final, nsplit):
    x = x_ref[0]
    xn = _mod_norm(x, gn_ref[...], sc_ref[0], sh_ref[0]).astype(BF16)
    tf = D_FF // nsplit
    acc = None
    for k in range(nsplit):
        u = jnp.maximum(_mm(xn, w1_ref[:, k * tf:(k + 1) * tf]), 0.0)
        part = _mm((u * u).astype(BF16), w2_ref[k * tf:(k + 1) * tf, :])
        acc = part if acc is None else acc + part
    y = x + g2_ref[0] * acc
    if final:
        y = _rms(y, fg_ref[...])
    o_ref[0] = y


def _ffn(h, mod, mrow, gn, w1, w2, final_g, l, tt, final):
    b_, L, _ = h.shape
    layer = lambda b, i: (l, 0, 0)
    row = lambda b, i: (b, i, 0)
    resident = dict(pipeline_mode=pl.Buffered(1))
    return pl.pallas_call(
        functools.partial(_ffn_kernel, final=final, nsplit=2),
        grid=(b_, L // tt),
        in_specs=[pl.BlockSpec((1, tt, D_MODEL), row),
                  pl.BlockSpec((1, 1, D_MODEL), lambda b, i: (mrow(b), 0, 3)),
                  pl.BlockSpec((1, 1, D_MODEL), lambda b, i: (mrow(b), 0, 4)),
                  pl.BlockSpec((1, 1, D_MODEL), lambda b, i: (mrow(b), 0, 5)),
                  pl.BlockSpec((None, 1, D_MODEL), layer),
                  pl.BlockSpec((None, D_MODEL, D_FF), layer, **resident),
                  pl.BlockSpec((None, D_FF, D_MODEL), layer, **resident),
                  pl.BlockSpec((1, D_MODEL), lambda b, i: (0, 0))],
        out_specs=pl.BlockSpec((1, tt, D_MODEL), row),
        out_shape=jax.ShapeDtypeStruct(h.shape, F32),
        compiler_params=_cparams(("arbitrary", "arbitrary")),
        name="ffn",
    )(h, mod, mod, mod, gn, w1, w2, final_g)


def _mixffn_kernel(a_ref, x0_ref, yt_ref, o_ref, ga_ref, gb_ref, gc_ref, h_ref, g1_ref, sh2_ref, sc2_ref, g2_ref,
                   gn_ref, wp_ref, wh_ref, wo_ref, wout_ref, w1_ref, w2_ref, fg_ref, out_ref, *, final, nsplit):
    hy = (x0_ref[0].astype(F32) * yt_ref[0].T).astype(BF16)
    a = _mm(a_ref[0], wp_ref[...])
    b = _mm(hy, wh_ref[...])
    cm = _mm(o_ref[0], wo_ref[...])
    m = (jax.nn.sigmoid(ga_ref[0].astype(F32)) * a + jax.nn.sigmoid(gb_ref[0].astype(F32)) * b
         + jax.nn.sigmoid(gc_ref[0].astype(F32)) * cm)
    x = h_ref[0] + g1_ref[0] * _mm(m.astype(BF16), wout_ref[...])
    xn = _mod_norm(x, gn_ref[...], sc2_ref[0], sh2_ref[0]).astype(BF16)
    tf = D_FF // nsplit
    acc = None
    for k in range(nsplit):
        u = jnp.maximum(_mm(xn, w1_ref[:, k * tf:(k + 1) * tf]), 0.0)
        part = _mm((u * u).astype(BF16), w2_ref[k * tf:(k + 1) * tf, :])
        acc = part if acc is None else acc + part
    y = x + g2_ref[0] * acc
    if final:
        y = _rms(y, fg_ref[...])
    out_ref[0] = y


def _mixffn(a_pre, x0c, yt, o, proj, h, mod, mrow, wp, wh, wo, wout, gn, w1, w2, final_g, l, tt, final):
    b_, L, _ = h.shape
    layer = lambda b, i: (l, 0, 0)
    row = lambda b, i: (b, i, 0)
    gcol = COL_GATE // D_MODEL
    resident = dict(pipeline_mode=pl.Buffered(1))
    modrow = lambda k: pl.BlockSpec((1, 1, D_MODEL), lambda b, i: (mrow(b), 0, k))
    return pl.pallas_call(
        functools.partial(_mixffn_kernel, final=final, nsplit=2),
        grid=(b_, L // tt),
        in_specs=[pl.BlockSpec((1, tt, POOL_WIDTH), row),
                  pl.BlockSpec((1, tt, HY_WIDTH), row),
                  pl.BlockSpec((1, HY_WIDTH, tt), lambda b, i: (b, 0, i)),
                  pl.BlockSpec((1, tt, MLA_HEADS * V_HEAD), row),
                  pl.BlockSpec((1, tt, D_MODEL), lambda b, i: (b, i, gcol)),
                  pl.BlockSpec((1, tt, D_MODEL), lambda b, i: (b, i, gcol + 1)),
                  pl.BlockSpec((1, tt, D_MODEL), lambda b, i: (b, i, gcol + 2)),
                  pl.BlockSpec((1, tt, D_MODEL), row),
                  modrow(2), modrow(3), modrow(4), modrow(5),
                  pl.BlockSpec((None, 1, D_MODEL), layer),
                  pl.BlockSpec((None, POOL_WIDTH, D_MODEL), layer, **resident),
                  pl.BlockSpec((None, HY_WIDTH, D_MODEL), layer, **resident),
                  pl.BlockSpec((None, MLA_HEADS * V_HEAD, D_MODEL), layer, **resident),
                  pl.BlockSpec((None, D_MODEL, D_MODEL), layer, **resident),
                  pl.BlockSpec((None, D_MODEL, D_FF), layer, **resident),
                  pl.BlockSpec((None, D_FF, D_MODEL), layer, **resident),
                  pl.BlockSpec((1, D_MODEL), lambda b, i: (0, 0))],
        out_specs=pl.BlockSpec((1, tt, D_MODEL), row),
        out_shape=jax.ShapeDtypeStruct(h.shape, F32),
        compiler_params=pltpu.CompilerParams(dimension_semantics=("arbitrary", "arbitrary"),
                                             vmem_limit_bytes=VMEM_LIMIT_FUSED),
        name="mixffn",
    )(a_pre, x0c, yt, o, proj, proj, proj, h, mod, mod, mod, mod, gn, wp, wh, wo, wout, w1, w2, final_g)


def _rope_tables(n_rows):
    rows = np.repeat(np.arange(n_rows, dtype=np.float64), GRID_W)
    cols = np.tile(np.arange(GRID_W, dtype=np.float64), n_rows)
    half = QK_ROPE // 2
    inv = ROPE_BASE ** (-np.arange(0, half, 2, dtype=np.float64) / half)
    ar = rows[:, None] * inv
    ac = cols[:, None] * inv
    ang = np.concatenate([ar, ar, ac, ac], axis=-1)
    return np.cos(ang), np.sin(ang)


def _head_tables(cos, sin, L):
    ones = np.ones((L, QK_NOPE))
    z64 = np.zeros((L, QK_NOPE))
    z32 = np.zeros((L, HEAD_PAD - QK_DIM))
    if cos is None:
        cos = np.ones((L, QK_ROPE))
        sin = np.zeros((L, QK_ROPE))
    cq = np.concatenate([ones, cos, z32], axis=1) * (MLA_SCALE * LOG2E)
    sq = np.concatenate([z64, sin, z32], axis=1) * (MLA_SCALE * LOG2E)
    ck = np.concatenate([z64, cos, z32], axis=1)
    sk = np.concatenate([z64, sin, z32], axis=1)
    return tuple(np.asarray(t, np.float32) for t in (cq, sq, ck, sk))


def _hy_embed(L, tr):
    t = np.linspace(0.0, 1.0, L)[:, None]
    omega = 2.0 * np.pi * np.arange(L)[:, None] / L
    bands = np.linspace(1e-4, HY_BANDS - 1, HY_BANDS)[None, :]
    z = np.concatenate([t, np.cos(omega * bands), -np.sin(omega * bands)], axis=-1)
    zfull = np.concatenate([z, np.zeros((1, HY_EMB)), z[:0:-1]], axis=0)
    zfull = np.pad(zfull, ((0, 0), (0, HY_HALF - HY_EMB)))
    packed = zfull.reshape(2 * L // tr, 2, tr // 2, HY_HALF).transpose(0, 2, 1, 3).reshape(L, 2 * HY_HALF)
    return np.asarray(packed, np.float32)


def _blockdiag2(a):
    a = _pad2(a, HY_HALF, HY_HALF)
    z = jnp.zeros_like(a)
    return jnp.concatenate([jnp.concatenate([a, z], axis=-1), jnp.concatenate([z, a], axis=-1)], axis=-2)


def _pack_w_in(w):
    z = lambda n: jnp.zeros(w.shape[:-1] + (n,), BF16)
    o_q = 512 + 3 * HY_WIDTH
    o_kv = o_q + Q_LORA
    o_pe = o_kv + KV_LORA
    o_gate = o_pe + QK_ROPE
    kpe = w[..., o_pe:o_gate].astype(BF16)
    mid = jnp.concatenate([
        w[..., o_q:o_pe].astype(BF16),
        z(QK_NOPE), kpe, z(HEAD_PAD - QK_DIM),
        z(QK_NOPE), _rot_half(kpe), z(HEAD_PAD - QK_DIM),
        z(LANE)], axis=-1)
    return w[..., :o_q].astype(BF16), mid, w[..., o_gate:].astype(BF16)


def _pack_heads(w_uq, w_ukv):
    lead = w_uq.shape[:-2]
    wq3 = w_uq.astype(BF16).reshape(*lead, Q_LORA, MLA_HEADS, QK_DIM)
    pad = jnp.zeros((*lead, Q_LORA, MLA_HEADS, HEAD_PAD - QK_DIM), BF16)
    wq = jnp.concatenate([wq3, pad], axis=-1).reshape(*lead, Q_LORA, -1)
    wqr = jnp.concatenate([jnp.zeros((*lead, Q_LORA, MLA_HEADS, QK_NOPE), BF16), _rot_half(wq3[..., QK_NOPE:]), pad],
                          axis=-1).reshape(*lead, Q_LORA, -1)
    wkv3 = w_ukv.astype(BF16).reshape(*lead, KV_LORA, MLA_HEADS, QK_NOPE + V_HEAD)
    wk = jnp.concatenate([wkv3[..., :QK_NOPE], jnp.zeros((*lead, KV_LORA, MLA_HEADS, HEAD_PAD - QK_NOPE), BF16)],
                         axis=-1).reshape(*lead, KV_LORA, -1)
    zv = jnp.zeros((*lead, KV_LORA, MLA_HEADS // 2, V_HEAD), BF16)
    v3 = wkv3[..., QK_NOPE:]
    wv = jnp.stack([jnp.concatenate([v3[..., 0::2, :], zv], axis=-1),
                    jnp.concatenate([zv, v3[..., 1::2, :]], axis=-1)], axis=-2).reshape(*lead, KV_LORA, -1)
    return wq, wqr, wk, wv


def _pad2(a, r, c):
    pads = [(0, 0)] * (a.ndim - 2) + [(0, r - a.shape[-2]), (0, c - a.shape[-1])]
    return jnp.pad(a, pads)


TT_LAT = 512
TQ = 1024
TK = 512
TR_LAT = 1024


def kernel(x, c, ctx, c_ctx, w_mod, b_mod, norm1_g, norm2_g, w_in, pool_w, pool_scale, pool_out, hy_conv_w, hy_conv_b, hy_f_w1, hy_f_b1, hy_f_freq1, hy_f_w2, hy_f_b2, hy_f_freq2, hy_f_w3, hy_decay, hy_bias, hy_out, q_norm_g, w_uq, kv_norm_g, w_ukv, w_o, w_out, w_ff1, w_ff2, final_g):
    B, L, _ = x.shape
    Lc = ctx.shape[1]
    cos, sin = _rope_tables(L // GRID_W)
    tabs_lat = _head_tables(cos, sin, L)
    tabs_ctx = _head_tables(None, None, Lc)
    z_lat = _hy_embed(L, TR_LAT)
    z_ctx = _hy_embed(Lc, Lc)

    c_all = jnp.concatenate([c, c_ctx[None], jnp.zeros((MOD_ROWS - B - 1, D_MODEL), F32)], axis=0)
    mod = _modulation(c_all, w_mod, b_mod).reshape(DEPTH * MOD_ROWS, 1, N_MOD * D_MODEL)

    row3 = lambda a: a[:, None, :]
    g1n, g2n, gq, gkv = row3(norm1_g), row3(norm2_g), row3(q_norm_g), row3(kv_norm_g)
    w_in_p = _pack_w_in(w_in)
    wq, wqr, wk, wv = _pack_heads(w_uq, w_ukv)
    pw, ps = pool_w.astype(BF16), row3(pool_scale)
    cw, cb = hy_conv_w, row3(hy_conv_b)
    wp, wh, wo, wout = pool_out.astype(BF16), hy_out.astype(BF16), w_o.astype(BF16), w_out.astype(BF16)
    wf1, wf2 = w_ff1.astype(BF16), w_ff2.astype(BF16)
    twice = lambda v: jnp.tile(_pad2(row3(v), 1, HY_HALF), (1, 1, 2))
    fw1, fw2 = _blockdiag2(hy_f_w1), _blockdiag2(hy_f_w2)
    fb1, ff1, fb2, ff2 = twice(hy_f_b1), twice(hy_f_freq1), twice(hy_f_b2), twice(hy_f_freq2)
    w3d = hy_f_w3.reshape(DEPTH, HY_HALF, 2, HY_WIDTH).transpose(0, 2, 1, 3)
    z3 = jnp.zeros_like(w3d)
    fw3 = jnp.stack([jnp.concatenate([w3d, z3], axis=2), jnp.concatenate([z3, w3d], axis=2)], axis=2)
    dec = hy_decay[:, :, None, :]
    fg = final_g[None]

    h, hc = x, ctx
    for l in range(DEPTH):
        last = l == DEPTH - 1
        row_lat = lambda b, l=l: l * MOD_ROWS + b
        row_ctx = lambda b, l=l: l * MOD_ROWS + B

        def stream_pre(hs, mrow, tabs, tt):
            proj = _inproj(hs, mod, mrow, g1n, w_in_p, l, tt)
            q, k, v = _qkv(proj, gq, gkv, wq, wqr, wk, wv, *tabs, l, tt)
            return proj, q, k, v

        def stream_mix(hs, mrow, proj, q, kv_sources, zemb, tt, tq, tk, tr, final):
            Ls = hs.shape[1]
            a_pre, x0c, zvt = _seqlocal(proj, pw, ps, cw, cb, l, tt)
            filt_t, l1 = _hyena_filter(zemb, fw1, fb1, ff1, fw2, fb2, ff2, fw3, dec, l, Ls, tr)
            if Ls == FFT_N1 * FFT_N2 // 2:
                yt = _long_conv_lat(zvt, filt_t, l1, hy_bias, l)
            else:
                yt = _long_conv_ctx(zvt, filt_t, l1, hy_bias[l])
            o = _attention(q, kv_sources, tq, tk)
            return _mixffn(a_pre, x0c, yt, o, proj, hs, mod, mrow, wp, wh, wo, wout, g2n, wf1, wf2, fg, l, tt, final)

        proj_c, q_c, k_c, v_c = stream_pre(hc, row_ctx, tabs_ctx, Lc)
        proj_l, q_l, k_l, v_l = stream_pre(h, row_lat, tabs_lat, TT_LAT)
        h = stream_mix(h, row_lat, proj_l, q_l, [(k_c, v_c), (k_l, v_l)], z_lat, TT_LAT, TQ, TK, TR_LAT, last)
        if not last:
            hc = stream_mix(hc, row_ctx, proj_c, q_c, [(k_c, v_c)], z_ctx, Lc, Lc, Lc, Lc, False)
    return h
```

```python
import functools
import math

import jax
import jax.numpy as jnp
import numpy as np
from jax import lax
from jax.experimental import pallas as pl
from jax.experimental.pallas import tpu as pltpu

F32 = jnp.float32
BF16 = jnp.bfloat16

D_MODEL = 1024
DEPTH = 4
GRID_W = 64
EPS = 1e-6

POOL_WIDTH = 512
POOL_WINDOWS = (2, 4, 8, 16)
POOL_GROUP = 128
HY_WIDTH = 512
HY_EMB = 33
HY_BANDS = 16
HY_HALF = 64
MLA_HEADS = 8
QK_NOPE = 64
QK_ROPE = 32
QK_DIM = 96
V_HEAD = 64
Q_LORA = 384
KV_LORA = 256
MLA_SCALE = QK_DIM ** -0.5
LOG2E = math.log2(math.e)
ROPE_BASE = 10000.0
D_FF = 4 * D_MODEL
N_MOD = 6
MOD_ROWS = 8

LANE = 128
HALO = 16
HEAD_PAD = 128

COL_POOL = 0
COL_HY = 512
COL_GATE = 2048
N_PROJ = 5120

VMEM_LIMIT = 50 * 1024 * 1024
VMEM_LIMIT_FUSED = 58 * 1024 * 1024


def _cparams(sem):
    return pltpu.CompilerParams(dimension_semantics=sem, vmem_limit_bytes=VMEM_LIMIT)


def _mm(a, b):
    return jnp.dot(a, b, preferred_element_type=F32)


def _split(x):
    hi = x.astype(BF16)
    lo = (x - hi.astype(F32)).astype(BF16)
    return hi, lo


def _dot3(a, b):
    ah, al = _split(a)
    bh, bl = _split(b)
    return _mm(ah, bh) + _mm(al, bh) + _mm(ah, bl)


def _dot3c(a, bh, bl):
    ah, al = _split(a)
    return _mm(ah, bh) + _mm(al, bh) + _mm(ah, bl)


def _cdot3(ch, cl, b):
    bh, bl = _split(b)
    n = b.shape[1]
    r = _mm(ch, jnp.concatenate([bh, bl], axis=1))
    return r[:, :n] + r[:, n:] + _mm(cl, bh)


def _hl(m):
    m32 = np.asarray(m, np.float32)
    hi = m32.astype(BF16)
    lo = (m32 - hi.astype(np.float32)).astype(BF16)
    return hi, lo


def _rot_half(x):
    q = QK_ROPE // 4
    return jnp.concatenate([-x[..., q:2 * q], x[..., :q], -x[..., 3 * q:], x[..., 2 * q:3 * q]], axis=-1)


def _mod_kernel(c_ref, w_ref, b_ref, o_ref):
    c = c_ref[...]
    s = c * jax.nn.sigmoid(c)
    o_ref[0] = _dot3(s, w_ref[0]) + b_ref[0]


def _modulation(c_all, w_mod, b_mod):
    tn = 1536
    n = N_MOD * D_MODEL
    return pl.pallas_call(
        _mod_kernel,
        grid=(DEPTH, n // tn),
        in_specs=[pl.BlockSpec((MOD_ROWS, D_MODEL), lambda l, j: (0, 0)),
                  pl.BlockSpec((1, D_MODEL, tn), lambda l, j: (l, 0, j)),
                  pl.BlockSpec((1, 1, tn), lambda l, j: (l, 0, j))],
        out_specs=pl.BlockSpec((1, MOD_ROWS, tn), lambda l, j: (l, 0, j)),
        out_shape=jax.ShapeDtypeStruct((DEPTH, MOD_ROWS, n), F32),
        compiler_params=_cparams(("arbitrary", "arbitrary")),
        name="modulation",
    )(c_all, w_mod, b_mod.reshape(DEPTH, 1, n))


def _mod_norm(x, g, sc, sh):
    ms = jnp.mean(x * x, axis=-1, keepdims=True)
    return (x * lax.rsqrt(ms + EPS) * g) * (1.0 + sc) + sh


def _rms(x, g):
    ms = jnp.mean(x * x, axis=-1, keepdims=True)
    return x * lax.rsqrt(ms + EPS) * g


def _inproj_kernel(x_ref, sh_ref, sc_ref, g_ref, wa_ref, wm_ref, wg_ref, gq_ref, gkv_ref,
                   wq_ref, wqr_ref, wk_ref, wv_ref, cq_ref, sq_ref, ck_ref, sk_ref,
                   proj_ref, q_ref, k_ref, v_ref):
    xn = _mod_norm(x_ref[0], g_ref[...], sc_ref[0], sh_ref[0]).astype(BF16)
    proj_ref[0, :, :COL_GATE] = _mm(xn, wa_ref[...]).astype(proj_ref.dtype)
    proj_ref[0, :, COL_GATE:] = _mm(xn, wg_ref[...]).astype(proj_ref.dtype)
    mid = _mm(xn, wm_ref[...])
    cq = mid[:, :Q_LORA]
    ckv = mid[:, Q_LORA:Q_LORA + KV_LORA]
    o = Q_LORA + KV_LORA
    kpe = mid[:, o:o + LANE]
    kpe_rot = mid[:, o + LANE:o + 2 * LANE]
    qn = _rms(cq, gq_ref[...]).astype(BF16)
    q = _mm(qn, wq_ref[...])
    qr = _mm(qn, wqr_ref[...])
    kvn = _rms(ckv, gkv_ref[...]).astype(BF16)
    k = _mm(kvn, wk_ref[...])
    v = _mm(kvn, wv_ref[...])
    cq_t, sq_t = cq_ref[...], sq_ref[...]
    kpe_r = kpe * ck_ref[...] + kpe_rot * sk_ref[...]
    lane = lax.broadcasted_iota(jnp.int32, (1, HEAD_PAD), 1)
    for h in range(MLA_HEADS):
        sl = slice(h * HEAD_PAD, (h + 1) * HEAD_PAD)
        q_ref[0, h] = (q[:, sl] * cq_t + qr[:, sl] * sq_t).astype(q_ref.dtype)
        k_ref[0, h] = (k[:, sl] + kpe_r).astype(k_ref.dtype)
        ones_col = (lane == (V_HEAD if h % 2 == 0 else 0)).astype(F32)
        v_ref[0, h] = (v[:, sl] + ones_col).astype(v_ref.dtype)


def _inproj(h, mod, mrow, g, ws, gq, gkv, wq, wqr, wk, wv, cq, sq, ck, sk, l, tt):
    b_, L, _ = h.shape
    layer = lambda b, i: (l, 0, 0)
    resident = dict(pipeline_mode=pl.Buffered(1))
    hw = MLA_HEADS * HEAD_PAD
    table = pl.BlockSpec((tt, LANE), lambda b, i: (i, 0))
    heads = pl.BlockSpec((1, MLA_HEADS, tt, HEAD_PAD), lambda b, i: (b, 0, i, 0))
    return pl.pallas_call(
        _inproj_kernel,
        grid=(b_, L // tt),
        in_specs=[pl.BlockSpec((1, tt, D_MODEL), lambda b, i: (b, i, 0)),
                  pl.BlockSpec((1, 1, D_MODEL), lambda b, i: (mrow(b), 0, 0)),
                  pl.BlockSpec((1, 1, D_MODEL), lambda b, i: (mrow(b), 0, 1)),
                  pl.BlockSpec((None, 1, D_MODEL), layer)]
        + [pl.BlockSpec((None, D_MODEL, w.shape[2]), layer, **resident) for w in ws]
        + [pl.BlockSpec((None, 1, Q_LORA), layer), pl.BlockSpec((None, 1, KV_LORA), layer),
           pl.BlockSpec((None, Q_LORA, hw), layer, **resident), pl.BlockSpec((None, Q_LORA, hw), layer, **resident),
           pl.BlockSpec((None, KV_LORA, hw), layer, **resident), pl.BlockSpec((None, KV_LORA, hw), layer, **resident),
           table, table, table, table],
        out_specs=[pl.BlockSpec((1, tt, N_PROJ), lambda b, i: (b, i, 0)), heads, heads, heads],
        out_shape=[jax.ShapeDtypeStruct((b_, L, N_PROJ), BF16)]
        + [jax.ShapeDtypeStruct((b_, MLA_HEADS, L, HEAD_PAD), BF16)] * 3,
        compiler_params=_cparams(("arbitrary", "arbitrary")),
        name="inproj",
    )(h, mod, mod, g, *ws, gq, gkv, wq, wqr, wk, wv, cq, sq, ck, sk)


def _seqlocal_kernel(cur_ref, prev_ref, next_ref, pw_ref, ps_ref, cw_ref, cb_ref,
                     a_ref, x0_ref, zvt_ref, *, tt, L):
    i = pl.program_id(1)
    nt = pl.num_programs(1)
    keep_prev = (i > 0).astype(F32)
    keep_next = (i < nt - 1).astype(F32)
    n = tt + 2 * HALO

    def ext(lo, hi):
        return jnp.concatenate([prev_ref[0, :, lo:hi].astype(F32) * keep_prev,
                                cur_ref[0, :, lo:hi].astype(F32),
                                next_ref[0, :, lo:hi].astype(F32) * keep_next], axis=0)

    def roll(x, s):
        return pltpu.roll(x, s % n, 0)

    t = i * tt + lax.broadcasted_iota(jnp.int32, (tt, POOL_GROUP), 0)
    for gi, w in enumerate(POOL_WINDOWS):
        lo = COL_POOL + gi * POOL_GROUP
        xg = ext(lo, lo + POOL_GROUP)
        s = xg + roll(xg, 1)
        if w >= 4:
            s = roll(s, 1) + roll(s, -1)
        if w >= 8:
            s = roll(s, 2) + roll(s, -2)
        if w >= 16:
            s = roll(s, 4) + roll(s, -4)
        cnt = (jnp.minimum(t + w // 2, L) - jnp.maximum(t - w // 2, 0)).astype(F32)
        u = s[HALO:HALO + tt] / cnt - xg[HALO:HALO + tt]
        ug = _mm(u.astype(BF16), pw_ref[gi]) * ps_ref[:, gi * POOL_GROUP:(gi + 1) * POOL_GROUP]
        a_ref[0, :, gi * POOL_GROUP:(gi + 1) * POOL_GROUP] = ug.astype(a_ref.dtype)

    def conv(part, j):
        c0 = part * HY_WIDTH + j * LANE
        xe = ext(COL_HY + c0, COL_HY + c0 + LANE)
        y = (roll(xe, 1) * cw_ref[0:1, c0:c0 + LANE] + xe * cw_ref[1:2, c0:c0 + LANE]
             + roll(xe, -1) * cw_ref[2:3, c0:c0 + LANE] + cb_ref[:, c0:c0 + LANE])
        return y[HALO:HALO + tt]

    for j in range(HY_WIDTH // LANE):
        x0_ref[0, :, j * LANE:(j + 1) * LANE] = conv(0, j).astype(x0_ref.dtype)
        zv = conv(2, j) * conv(1, j)
        zvt_ref[0, j * LANE:(j + 1) * LANE, :] = zv.T


def _seqlocal(proj, pool_w, pool_scale, conv_w, conv_b, l, tt):
    b_, L, _ = proj.shape
    hb = tt // HALO
    nh = L // HALO
    wide = COL_GATE
    return pl.pallas_call(
        functools.partial(_seqlocal_kernel, tt=tt, L=L),
        grid=(b_, L // tt),
        in_specs=[pl.BlockSpec((1, tt, wide), lambda b, i: (b, i, 0)),
                  pl.BlockSpec((1, HALO, wide), lambda b, i: (b, jnp.maximum(i * hb - 1, 0), 0)),
                  pl.BlockSpec((1, HALO, wide), lambda b, i: (b, jnp.minimum((i + 1) * hb, nh - 1), 0)),
                  pl.BlockSpec((None, len(POOL_WINDOWS), POOL_GROUP, POOL_GROUP), lambda b, i: (l, 0, 0, 0)),
                  pl.BlockSpec((None, 1, POOL_WIDTH), lambda b, i: (l, 0, 0)),
                  pl.BlockSpec((None, 3, 3 * HY_WIDTH), lambda b, i: (l, 0, 0)),
                  pl.BlockSpec((None, 1, 3 * HY_WIDTH), lambda b, i: (l, 0, 0))],
        out_specs=[pl.BlockSpec((1, tt, POOL_WIDTH), lambda b, i: (b, i, 0)),
                   pl.BlockSpec((1, tt, HY_WIDTH), lambda b, i: (b, i, 0)),
                   pl.BlockSpec((1, HY_WIDTH, tt), lambda b, i: (b, 0, i))],
        out_shape=[jax.ShapeDtypeStruct((b_, L, POOL_WIDTH), BF16),
                   jax.ShapeDtypeStruct((b_, L, HY_WIDTH), BF16),
                   jax.ShapeDtypeStruct((b_, HY_WIDTH, L), F32)],
        compiler_params=_cparams(("arbitrary", "arbitrary")),
        name="seqlocal",
    )(proj, proj, proj, pool_w, pool_scale, conv_w, conv_b)


def _filter_kernel(z_ref, w1_ref, b1_ref, f1_ref, w2_ref, b2_ref, f2_ref, w3_ref, dec_ref,
                   filt_ref, l1_ref, *, tr, L):
    i = pl.program_id(0)
    half = tr // 2
    z = z_ref[...]
    hid = jnp.sin(f1_ref[...] * (_dot3(z, w1_ref[...]) + b1_ref[...]))
    hid = jnp.sin(f2_ref[...] * (_dot3(hid, w2_ref[...]) + b2_ref[...]))
    decay = jnp.abs(dec_ref[0])
    total = None
    for s in range(2):
        h = _dot3(hid, w3_ref[0, s]) * jnp.exp(-z[:, s * HY_HALF:s * HY_HALF + 1] * decay)
        row = i * tr + s * half + lax.broadcasted_iota(jnp.int32, h.shape, 0)
        h = jnp.where(row == L, 0.0, h)
        filt_ref[:, s * half:(s + 1) * half] = h.T
        part = jnp.sum(jnp.abs(h), axis=0, keepdims=True)
        total = part if total is None else total + part

    @pl.when(i == 0)
    def _():
        l1_ref[...] = jnp.zeros_like(l1_ref)

    l1_ref[...] += total


def _hyena_filter(zpacked, w1, b1, f1, w2, b2, f2, w3, dec, l, L, tr):
    n = 2 * L
    nhalf = L // tr
    full = lambda i: (0, 0)
    mat = pl.BlockSpec((None, LANE, LANE), lambda i: (l, 0, 0))
    vec = pl.BlockSpec((None, 1, LANE), lambda i: (l, 0, 0))
    return pl.pallas_call(
        functools.partial(_filter_kernel, tr=tr, L=L),
        grid=(n // tr,),
        in_specs=[pl.BlockSpec((tr // 2, LANE), lambda i: (i, 0)),
                  mat, vec, vec, mat, vec, vec,
                  pl.BlockSpec((None, 1, 2, LANE, HY_WIDTH), lambda i: (l, i // nhalf, 0, 0, 0)),
                  pl.BlockSpec((None, 1, 1, HY_WIDTH), lambda i: (l, i // nhalf, 0, 0))],
        out_specs=[pl.BlockSpec((HY_WIDTH, tr), lambda i: (0, i)),
                   pl.BlockSpec((1, HY_WIDTH), full)],
        out_shape=[jax.ShapeDtypeStruct((HY_WIDTH, n), F32),
                   jax.ShapeDtypeStruct((1, HY_WIDTH), F32)],
        compiler_params=_cparams(("arbitrary",)),
        name="hyena_filter",
    )(zpacked, w1, b1, f1, w2, b2, f2, w3, dec)


FFT_N1 = 128
FFT_N2 = 128
CONV_CB = 16


def _lconv_kernel(bias_ref, l1_ref, z_ref, f_ref, f1d_h, f1d_l, f1f_h, f1f_l, tr_ref, ti_ref,
                  f2_h, f2_l, g2_h, g2_l, e1_h, e1_l, y_ref, bs_ref, bf_ref, cm_ref, ys_ref, *, layer):
    blk = pl.program_id(0)
    tw_r = tr_ref[...]
    tw_i = ti_ref[...]
    n2 = FFT_N2
    for c in range(CONV_CB):
        rows = slice(c * FFT_N1, (c + 1) * FFT_N1)
        zc = jnp.concatenate([z_ref[0, c], z_ref[1, c]], axis=0)
        a = _cdot3(f1d_h[...], f1d_l[...], zc)
        ar, ai = a[:FFT_N1], a[FFT_N1:]
        bs_ref[rows, :n2] = ar * tw_r - ai * tw_i
        bs_ref[rows, n2:] = ar * tw_i + ai * tw_r
        inv_l1 = 1.0 / jnp.full((1, n2), l1_ref[0, blk * CONV_CB + c], F32)
        a = _cdot3(f1f_h[...], f1f_l[...], f_ref[c] * inv_l1)
        ar, ai = a[:FFT_N1], a[FFT_N1:]
        bf_ref[rows, :n2] = ar * tw_r - ai * tw_i
        bf_ref[rows, n2:] = ar * tw_i + ai * tw_r
    x = _dot3c(bs_ref[...], f2_h[...], f2_l[...])
    hf = _dot3c(bf_ref[...], f2_h[...], f2_l[...])
    xr, xi = x[:, :n2], x[:, n2:]
    hr, hi = hf[:, :n2], hf[:, n2:]
    y = jnp.concatenate([xr * hr - xi * hi, xr * hi + xi * hr], axis=1)
    cm_ref[...] = _dot3c(y, g2_h[...], g2_l[...])
    half = FFT_N1 // 2
    for c in range(CONV_CB):
        rows = slice(c * FFT_N1, (c + 1) * FFT_N1)
        cr = cm_ref[rows, :n2]
        ci = cm_ref[rows, n2:]
        dc = jnp.concatenate([cr * tw_r + ci * tw_i, ci * tw_r - cr * tw_i], axis=0)
        yc = _cdot3(e1_h[...], e1_l[...], dc)
        bias = bias_ref[layer, blk * CONV_CB + c]
        ys_ref[0, c] = yc[:half] + z_ref[0, c] * bias
        ys_ref[1, c] = yc[half:] + z_ref[1, c] * bias
    sub = 8
    for b in range(2):
        for cg in range(CONV_CB // sub):
            for ng in range(half // sub):
                blk8 = ys_ref[b, cg * sub:(cg + 1) * sub, ng * sub:(ng + 1) * sub, :]
                t8 = jnp.swapaxes(blk8, 0, 1)
                for j in range(sub):
                    n1 = ng * sub + j
                    y_ref[b, cg * sub:(cg + 1) * sub, n1 * FFT_N2:(n1 + 1) * FFT_N2] = t8[j]


def _lconv_consts():
    n1, n2 = FFT_N1, FFT_N2
    n = n1 * n2
    idx = np.arange(128)
    ang1 = 2.0 * np.pi * ((idx[:, None] * idx[None, :]) % 128) / 128.0
    fr, fi = np.cos(ang1), -np.sin(ang1)
    angt = 2.0 * np.pi * (idx[:, None] * idx[None, :]) / n
    tw_r, tw_i = np.cos(angt), -np.sin(angt)
    h = n1 // 2
    f1d = np.block([[fr[:, :h], -fi[:, :h]], [fi[:, :h], fr[:, :h]]])
    f1f = np.concatenate([fr, fi], axis=0)
    f2 = np.block([[fr, fi], [-fi, fr]])
    g2 = np.block([[fr, -fi], [fi, fr]])
    er, ei = fr[:h] / n, -fi[:h] / n
    e1 = np.block([[er, -ei], [ei, er]])

    return (*_hl(f1d), *_hl(f1f), np.asarray(tw_r, np.float32), np.asarray(tw_i, np.float32),
            *_hl(f2), *_hl(g2), *_hl(e1))


def _long_conv_lat(zvt, filt_t, l1, bias, l):
    b_, C, L = zvt.shape
    assert b_ == 2 and L == FFT_N1 * FFT_N2 // 2
    z4 = zvt.reshape(b_, C, FFT_N1 // 2, FFT_N2)
    f3 = filt_t.reshape(C, FFT_N1, FFT_N2)
    consts = _lconv_consts()
    cb = CONV_CB

    def cspec(a):
        return pl.BlockSpec(a.shape, lambda i: (0,) * a.ndim)

    return pl.pallas_call(
        functools.partial(_lconv_kernel, layer=l),
        grid=(C // cb,),
        in_specs=[pl.BlockSpec(memory_space=pltpu.SMEM), pl.BlockSpec(memory_space=pltpu.SMEM),
                  pl.BlockSpec((b_, cb, FFT_N1 // 2, FFT_N2), lambda i: (0, i, 0, 0)),
                  pl.BlockSpec((cb, FFT_N1, FFT_N2), lambda i: (i, 0, 0))] + [cspec(a) for a in consts],
        out_specs=pl.BlockSpec((b_, cb, L), lambda i: (0, i, 0)),
        out_shape=jax.ShapeDtypeStruct(zvt.shape, F32),
        scratch_shapes=[pltpu.VMEM((cb * FFT_N1, 2 * FFT_N2), F32),
                        pltpu.VMEM((cb * FFT_N1, 2 * FFT_N2), F32),
                        pltpu.VMEM((cb * FFT_N1, 2 * FFT_N2), F32),
                        pltpu.VMEM((b_, cb, FFT_N1 // 2, FFT_N2), F32)],
        compiler_params=_cparams(("arbitrary",)),
        name="long_conv",
    )(bias, l1, z4, f3, *consts)


def _cconv_kernel(z_ref, f_ref, l1_ref, bias_ref, fd_h, fd_l, ff_h, ff_l, fi_h, fi_l, y_ref, *, L):
    zr, zi = z_ref[0], z_ref[1]
    x = _dot3c(jnp.concatenate([zr, zi], axis=1), fd_h[...], fd_l[...])
    hf = _dot3c(f_ref[...] * (1.0 / l1_ref[...]), ff_h[...], ff_l[...])
    n = 2 * L
    xr, xi = x[:, :n], x[:, n:]
    hr, hi = hf[:, :n], hf[:, n:]
    y = jnp.concatenate([xr * hr - xi * hi, xr * hi + xi * hr], axis=1)
    out = _dot3c(y, fi_h[...], fi_l[...])
    bias = bias_ref[...]
    y_ref[0] = out[:, :L] + zr * bias
    y_ref[1] = out[:, L:] + zi * bias


def _cconv_consts(L):
    n = 2 * L
    idx = np.arange(n)
    ang = 2.0 * np.pi * ((idx[:, None] * idx[None, :]) % n) / n
    fr, fi = np.cos(ang), -np.sin(ang)
    fd = np.block([[fr[:L], fi[:L]], [-fi[:L], fr[:L]]])
    ff = np.concatenate([fr, fi], axis=1)
    er, ei = fr[:, :L] / n, -fi[:, :L] / n
    finv = np.block([[er, ei], [-ei, er]])

    return (*_hl(fd), *_hl(ff), *_hl(finv))


def _long_conv_ctx(zvt, filt_t, l1, bias):
    b_, C, L = zvt.shape
    consts = _cconv_consts(L)
    bias_b = jnp.broadcast_to(bias[:, None], (C, L))
    l1_col = l1.reshape(C, 1)

    def cspec(a):
        return pl.BlockSpec(a.shape, lambda i: (0,) * a.ndim)

    return pl.pallas_call(
        functools.partial(_cconv_kernel, L=L),
        grid=(1,),
        in_specs=[cspec(zvt), cspec(filt_t), cspec(l1_col), cspec(bias_b)] + [cspec(a) for a in consts],
        out_specs=cspec(zvt),
        out_shape=jax.ShapeDtypeStruct(zvt.shape, F32),
        compiler_params=_cparams(("arbitrary",)),
        name="long_conv_ctx",
    )(zvt, filt_t, l1_col, bias_b, *consts)


def _attn_kernel(q_ref, *refs, chunks):
    o_ref = refs[-1]
    tq = q_ref.shape[2]
    q = [q_ref[0, hh] for hh in range(2)]
    m = [jnp.full((tq, 1), -1e30, F32) for _ in range(2)]
    acc = [jnp.zeros((tq, LANE), F32) for _ in range(2)]
    work = [(src, start, size, hh) for (src, start, size) in chunks for hh in range(2)]

    def scores(n):
        src, start, size, hh = work[n]
        k = refs[2 * src][0, hh, start:start + size, :]
        return lax.dot_general(q[hh], k, (((1,), (1,)), ((), ())), preferred_element_type=F32)

    s_next = scores(0)
    for n, (src, start, size, hh) in enumerate(work):
        s = s_next
        if n + 1 < len(work):
            s_next = scores(n + 1)
        v = refs[2 * src + 1][0, hh, start:start + size, :]
        m_new = jnp.maximum(m[hh], jnp.max(s, axis=-1, keepdims=True))
        p = jnp.exp2(s - m_new)
        alpha = jnp.exp2(m[hh] - m_new)
        acc[hh] = alpha * acc[hh] + _mm(p.astype(BF16), v)
        m[hh] = m_new
    lane = lax.broadcasted_iota(jnp.int32, (tq, LANE), 1)
    o0 = acc[0] / acc[0][:, V_HEAD:V_HEAD + 1]
    o1 = acc[1] / acc[1][:, 0:1]
    o_ref[0] = jnp.where(lane < V_HEAD, o0, o1).astype(o_ref.dtype)


def _attention(q, kv_sources, tq, tk):
    b_, H, L, _ = q.shape
    chunks = []
    in_specs = [pl.BlockSpec((1, 2, tq, HEAD_PAD), lambda b, hp, i: (b, hp, i, 0))]
    args = [q]
    for src, (k, v) in enumerate(kv_sources):
        S = k.shape[2]
        step = min(tk, S)
        chunks += [(src, start, step) for start in range(0, S, step)]
        in_specs += [pl.BlockSpec((1, 2, S, HEAD_PAD), lambda b, hp, i: (b, hp, 0, 0))] * 2
        args += [k, v]
    return pl.pallas_call(
        functools.partial(_attn_kernel, chunks=tuple(chunks)),
        grid=(b_, H // 2, L // tq),
        in_specs=in_specs,
        out_specs=pl.BlockSpec((1, tq, LANE), lambda b, hp, i: (b, i, hp)),
        out_shape=jax.ShapeDtypeStruct((b_, L, H * V_HEAD), BF16),
        compiler_params=_cparams(("arbitrary", "arbitrary", "arbitrary")),
        name="attention",
    )(*args)


def _mixffn_kernel(a_ref, x0_ref, yt_ref, o_ref, ga_ref, gb_ref, gc_ref, h_ref, g1_ref, sh2_ref, sc2_ref, g2_ref,
                   gn_ref, wp_ref, wh_ref, wo_ref, wout_ref, w1_ref, w2_ref, fg_ref, out_ref, *, final, nsplit):
    hy = (x0_ref[0].astype(F32) * yt_ref[0].T).astype(BF16)
    a = _mm(a_ref[0], wp_ref[...])
    b = _mm(hy, wh_ref[...])
    cm = _mm(o_ref[0], wo_ref[...])
    m = (jax.nn.sigmoid(ga_ref[0].astype(F32)) * a + jax.nn.sigmoid(gb_ref[0].astype(F32)) * b
         + jax.nn.sigmoid(gc_ref[0].astype(F32)) * cm)
    x = h_ref[0] + g1_ref[0] * _mm(m.astype(BF16), wout_ref[...])
    xn = _mod_norm(x, gn_ref[...], sc2_ref[0], sh2_ref[0]).astype(BF16)
    tf = D_FF // nsplit
    acc = None
    for k in range(nsplit):
        u = jnp.maximum(_mm(xn, w1_ref[:, k * tf:(k + 1) * tf]), 0.0)
        part = _mm((u * u).astype(BF16), w2_ref[k * tf:(k + 1) * tf, :])
        acc = part if acc is None else acc + part
    y = x + g2_ref[0] * acc
    if final:
        y = _rms(y, fg_ref[...])
    out_ref[0] = y


def _mixffn(a_pre, x0c, yt, o, proj, h, mod, mrow, wp, wh, wo, wout, gn, w1, w2, final_g, l, tt, final):
    b_, L, _ = h.shape
    layer = lambda b, i: (l, 0, 0)
    row = lambda b, i: (b, i, 0)
    gcol = COL_GATE // D_MODEL
    resident = dict(pipeline_mode=pl.Buffered(1))
    modrow = lambda k: pl.BlockSpec((1, 1, D_MODEL), lambda b, i: (mrow(b), 0, k))
    return pl.pallas_call(
        functools.partial(_mixffn_kernel, final=final, nsplit=2),
        grid=(b_, L // tt),
        in_specs=[pl.BlockSpec((1, tt, POOL_WIDTH), row),
                  pl.BlockSpec((1, tt, HY_WIDTH), row),
                  pl.BlockSpec((1, HY_WIDTH, tt), lambda b, i: (b, 0, i)),
                  pl.BlockSpec((1, tt, MLA_HEADS * V_HEAD), row),
                  pl.BlockSpec((1, tt, D_MODEL), lambda b, i: (b, i, gcol)),
                  pl.BlockSpec((1, tt, D_MODEL), lambda b, i: (b, i, gcol + 1)),
                  pl.BlockSpec((1, tt, D_MODEL), lambda b, i: (b, i, gcol + 2)),
                  pl.BlockSpec((1, tt, D_MODEL), row),
                  modrow(2), modrow(3), modrow(4), modrow(5),
                  pl.BlockSpec((None, 1, D_MODEL), layer),
                  pl.BlockSpec((None, POOL_WIDTH, D_MODEL), layer, **resident),
                  pl.BlockSpec((None, HY_WIDTH, D_MODEL), layer, **resident),
                  pl.BlockSpec((None, MLA_HEADS * V_HEAD, D_MODEL), layer, **resident),
                  pl.BlockSpec((None, D_MODEL, D_MODEL), layer, **resident),
                  pl.BlockSpec((None, D_MODEL, D_FF), layer, **resident),
                  pl.BlockSpec((None, D_FF, D_MODEL), layer, **resident),
                  pl.BlockSpec((1, D_MODEL), lambda b, i: (0, 0))],
        out_specs=pl.BlockSpec((1, tt, D_MODEL), row),
        out_shape=jax.ShapeDtypeStruct(h.shape, F32),
        compiler_params=pltpu.CompilerParams(dimension_semantics=("arbitrary", "arbitrary"),
                                             vmem_limit_bytes=VMEM_LIMIT_FUSED),
        name="mixffn",
    )(a_pre, x0c, yt, o, proj, proj, proj, h, mod, mod, mod, mod, gn, wp, wh, wo, wout, w1, w2, final_g)


def _rope_tables(n_rows):
    rows = np.repeat(np.arange(n_rows, dtype=np.float64), GRID_W)
    cols = np.tile(np.arange(GRID_W, dtype=np.float64), n_rows)
    half = QK_ROPE // 2
    inv = ROPE_BASE ** (-np.arange(0, half, 2, dtype=np.float64) / half)
    ar = rows[:, None] * inv
    ac = cols[:, None] * inv
    ang = np.concatenate([ar, ar, ac, ac], axis=-1)
    return np.cos(ang), np.sin(ang)


def _head_tables(cos, sin, L):
    ones = np.ones((L, QK_NOPE))
    z64 = np.zeros((L, QK_NOPE))
    z32 = np.zeros((L, HEAD_PAD - QK_DIM))
    if cos is None:
        cos = np.ones((L, QK_ROPE))
        sin = np.zeros((L, QK_ROPE))
    cq = np.concatenate([ones, cos, z32], axis=1) * (MLA_SCALE * LOG2E)
    sq = np.concatenate([z64, sin, z32], axis=1) * (MLA_SCALE * LOG2E)
    ck = np.concatenate([z64, cos, z32], axis=1)
    sk = np.concatenate([z64, sin, z32], axis=1)
    return tuple(np.asarray(t, np.float32) for t in (cq, sq, ck, sk))


def _hy_embed(L, tr):
    t = np.linspace(0.0, 1.0, L)[:, None]
    omega = 2.0 * np.pi * np.arange(L)[:, None] / L
    bands = np.linspace(1e-4, HY_BANDS - 1, HY_BANDS)[None, :]
    z = np.concatenate([t, np.cos(omega * bands), -np.sin(omega * bands)], axis=-1)
    zfull = np.concatenate([z, np.zeros((1, HY_EMB)), z[:0:-1]], axis=0)
    zfull = np.pad(zfull, ((0, 0), (0, HY_HALF - HY_EMB)))
    packed = zfull.reshape(2 * L // tr, 2, tr // 2, HY_HALF).transpose(0, 2, 1, 3).reshape(L, 2 * HY_HALF)
    return np.asarray(packed, np.float32)


def _blockdiag2(a):
    a = _pad2(a, HY_HALF, HY_HALF)
    z = jnp.zeros_like(a)
    return jnp.concatenate([jnp.concatenate([a, z], axis=-1), jnp.concatenate([z, a], axis=-1)], axis=-2)


def _pack_w_in(w):
    z = lambda n: jnp.zeros(w.shape[:-1] + (n,), BF16)
    o_q = 512 + 3 * HY_WIDTH
    o_kv = o_q + Q_LORA
    o_pe = o_kv + KV_LORA
    o_gate = o_pe + QK_ROPE
    kpe = w[..., o_pe:o_gate].astype(BF16)
    mid = jnp.concatenate([
        w[..., o_q:o_pe].astype(BF16),
        z(QK_NOPE), kpe, z(HEAD_PAD - QK_DIM),
        z(QK_NOPE), _rot_half(kpe), z(HEAD_PAD - QK_DIM),
        z(LANE)], axis=-1)
    return w[..., :o_q].astype(BF16), mid, w[..., o_gate:].astype(BF16)


def _pack_heads(w_uq, w_ukv):
    lead = w_uq.shape[:-2]
    wq3 = w_uq.astype(BF16).reshape(*lead, Q_LORA, MLA_HEADS, QK_DIM)
    pad = jnp.zeros((*lead, Q_LORA, MLA_HEADS, HEAD_PAD - QK_DIM), BF16)
    wq = jnp.concatenate([wq3, pad], axis=-1).reshape(*lead, Q_LORA, -1)
    wqr = jnp.concatenate([jnp.zeros((*lead, Q_LORA, MLA_HEADS, QK_NOPE), BF16), _rot_half(wq3[..., QK_NOPE:]), pad],
                          axis=-1).reshape(*lead, Q_LORA, -1)
    wkv3 = w_ukv.astype(BF16).reshape(*lead, KV_LORA, MLA_HEADS, QK_NOPE + V_HEAD)
    wk = jnp.concatenate([wkv3[..., :QK_NOPE], jnp.zeros((*lead, KV_LORA, MLA_HEADS, HEAD_PAD - QK_NOPE), BF16)],
                         axis=-1).reshape(*lead, KV_LORA, -1)
    zv = jnp.zeros((*lead, KV_LORA, MLA_HEADS // 2, V_HEAD), BF16)
    v3 = wkv3[..., QK_NOPE:]
    wv = jnp.stack([jnp.concatenate([v3[..., 0::2, :], zv], axis=-1),
                    jnp.concatenate([zv, v3[..., 1::2, :]], axis=-1)], axis=-2).reshape(*lead, KV_LORA, -1)
    return wq, wqr, wk, wv


def _pad2(a, r, c):
    pads = [(0, 0)] * (a.ndim - 2) + [(0, r - a.shape[-2]), (0, c - a.shape[-1])]
    return jnp.pad(a, pads)


TT_LAT = 512
TQ = 1024
TK = 512
TR_LAT = 1024


def kernel(x, c, ctx, c_ctx, w_mod, b_mod, norm1_g, norm2_g, w_in, pool_w, pool_scale, pool_out, hy_conv_w, hy_conv_b, hy_f_w1, hy_f_b1, hy_f_freq1, hy_f_w2, hy_f_b2, hy_f_freq2, hy_f_w3, hy_decay, hy_bias, hy_out, q_norm_g, w_uq, kv_norm_g, w_ukv, w_o, w_out, w_ff1, w_ff2, final_g):
    B, L, _ = x.shape
    Lc = ctx.shape[1]
    cos, sin = _rope_tables(L // GRID_W)
    tabs_lat = _head_tables(cos, sin, L)
    tabs_ctx = _head_tables(None, None, Lc)
    z_lat = _hy_embed(L, TR_LAT)
    z_ctx = _hy_embed(Lc, Lc)

    c_all = jnp.concatenate([c, c_ctx[None], jnp.zeros((MOD_ROWS - B - 1, D_MODEL), F32)], axis=0)
    mod = _modulation(c_all, w_mod, b_mod).reshape(DEPTH * MOD_ROWS, 1, N_MOD * D_MODEL)

    row3 = lambda a: a[:, None, :]
    g1n, g2n, gq, gkv = row3(norm1_g), row3(norm2_g), row3(q_norm_g), row3(kv_norm_g)
    w_in_p = _pack_w_in(w_in)
    wq, wqr, wk, wv = _pack_heads(w_uq, w_ukv)
    pw, ps = pool_w.astype(BF16), row3(pool_scale)
    cw, cb = hy_conv_w, row3(hy_conv_b)
    wp, wh, wo, wout = pool_out.astype(BF16), hy_out.astype(BF16), w_o.astype(BF16), w_out.astype(BF16)
    wf1, wf2 = w_ff1.astype(BF16), w_ff2.astype(BF16)
    twice = lambda v: jnp.tile(_pad2(row3(v), 1, HY_HALF), (1, 1, 2))
    fw1, fw2 = _blockdiag2(hy_f_w1), _blockdiag2(hy_f_w2)
    fb1, ff1, fb2, ff2 = twice(hy_f_b1), twice(hy_f_freq1), twice(hy_f_b2), twice(hy_f_freq2)
    w3d = hy_f_w3.reshape(DEPTH, HY_HALF, 2, HY_WIDTH).transpose(0, 2, 1, 3)
    z3 = jnp.zeros_like(w3d)
    fw3 = jnp.stack([jnp.concatenate([w3d, z3], axis=2), jnp.concatenate([z3, w3d], axis=2)], axis=2)
    dec = hy_decay[:, :, None, :]
    fg = final_g[None]

    h, hc = x, ctx
    for l in range(DEPTH):
        last = l == DEPTH - 1
        row_lat = lambda b, l=l: l * MOD_ROWS + b
        row_ctx = lambda b, l=l: l * MOD_ROWS + B

        def stream_pre(hs, mrow, tabs, tt):
            return _inproj(hs, mod, mrow, g1n, w_in_p, gq, gkv, wq, wqr, wk, wv, *tabs, l, tt)

        def stream_mix(hs, mrow, proj, q, kv_sources, zemb, tt, tq, tk, tr, final):
            Ls = hs.shape[1]
            a_pre, x0c, zvt = _seqlocal(proj, pw, ps, cw, cb, l, tt)
            filt_t, l1 = _hyena_filter(zemb, fw1, fb1, ff1, fw2, fb2, ff2, fw3, dec, l, Ls, tr)
            if Ls == FFT_N1 * FFT_N2 // 2:
                yt = _long_conv_lat(zvt, filt_t, l1, hy_bias, l)
            else:
                yt = _long_conv_ctx(zvt, filt_t, l1, hy_bias[l])
            o = _attention(q, kv_sources, tq, tk)
            return _mixffn(a_pre, x0c, yt, o, proj, hs, mod, mrow, wp, wh, wo, wout, g2n, wf1, wf2, fg, l, tt, final)

        proj_c, q_c, k_c, v_c = stream_pre(hc, row_ctx, tabs_ctx, Lc)
        proj_l, q_l, k_l, v_l = stream_pre(h, row_lat, tabs_lat, TT_LAT)
        h = stream_mix(h, row_lat, proj_l, q_l, [(k_c, v_c), (k_l, v_l)], z_lat, TT_LAT, TQ, TK, TR_LAT, last)
        if not last:
            hc = stream_mix(hc, row_ctx, proj_c, q_c, [(k_c, v_c)], z_ctx, Lc, Lc, Lc, Lc, False)
    return h
```

```python
import functools
import math

import jax
import jax.numpy as jnp
import numpy as np
from jax import lax
from jax.experimental import pallas as pl
from jax.experimental.pallas import tpu as pltpu

F32 = jnp.float32
BF16 = jnp.bfloat16

D_MODEL = 1024
DEPTH = 4
GRID_W = 64
EPS = 1e-6

POOL_WIDTH = 512
POOL_WINDOWS = (2, 4, 8, 16)
POOL_GROUP = 128
HY_WIDTH = 512
HY_EMB = 33
HY_BANDS = 16
HY_HALF = 64
MLA_HEADS = 8
QK_NOPE = 64
QK_ROPE = 32
QK_DIM = 96
V_HEAD = 64
Q_LORA = 384
KV_LORA = 256
MLA_SCALE = QK_DIM ** -0.5
LOG2E = math.log2(math.e)
ROPE_BASE = 10000.0
D_FF = 4 * D_MODEL
N_MOD = 6
MOD_ROWS = 8

LANE = 128
HALO = 16
HEAD_PAD = 128
VT_ROWS = 80

COL_POOL = 0
COL_HY = 512
COL_GATE = 2048
N_PROJ = 5120

VMEM_LIMIT = 50 * 1024 * 1024
VMEM_LIMIT_FUSED = 58 * 1024 * 1024


def _cparams(sem):
    return pltpu.CompilerParams(dimension_semantics=sem, vmem_limit_bytes=VMEM_LIMIT)


def _mm(a, b):
    return jnp.dot(a, b, preferred_element_type=F32)


def _split(x):
    hi = x.astype(BF16)
    lo = (x - hi.astype(F32)).astype(BF16)
    return hi, lo


def _dot3(a, b):
    ah, al = _split(a)
    bh, bl = _split(b)
    return _mm(ah, bh) + _mm(al, bh) + _mm(ah, bl)


def _dot3c(a, bh, bl):
    ah, al = _split(a)
    return _mm(ah, bh) + _mm(al, bh) + _mm(ah, bl)


def _cdot3_pair(ch, cl, b0, b1):
    h0, l0 = _split(b0)
    h1, l1 = _split(b1)
    n = b0.shape[1]
    r0 = _mm(ch, jnp.concatenate([h0, l0], axis=1))
    r1 = _mm(ch, jnp.concatenate([h1, l1], axis=1))
    rl = _mm(cl, jnp.concatenate([h0, h1], axis=1))
    return r0[:, :n] + r0[:, n:] + rl[:, :n], r1[:, :n] + r1[:, n:] + rl[:, n:]


def _hl(m):
    m32 = np.asarray(m, np.float32)
    hi = m32.astype(BF16)
    lo = (m32 - hi.astype(np.float32)).astype(BF16)
    return hi, lo


def _rot_half(x):
    q = QK_ROPE // 4
    return jnp.concatenate([-x[..., q:2 * q], x[..., :q], -x[..., 3 * q:], x[..., 2 * q:3 * q]], axis=-1)


def _mod_kernel(c_ref, w_ref, b_ref, o_ref):
    c = c_ref[...]
    s = c * jax.nn.sigmoid(c)
    o_ref[0] = _dot3(s, w_ref[0]) + b_ref[0]


def _modulation(c_all, w_mod, b_mod):
    tn = 1536
    n = N_MOD * D_MODEL
    return pl.pallas_call(
        _mod_kernel,
        grid=(DEPTH, n // tn),
        in_specs=[pl.BlockSpec((MOD_ROWS, D_MODEL), lambda l, j: (0, 0)),
                  pl.BlockSpec((1, D_MODEL, tn), lambda l, j: (l, 0, j)),
                  pl.BlockSpec((1, 1, tn), lambda l, j: (l, 0, j))],
        out_specs=pl.BlockSpec((1, MOD_ROWS, tn), lambda l, j: (l, 0, j)),
        out_shape=jax.ShapeDtypeStruct((DEPTH, MOD_ROWS, n), F32),
        compiler_params=_cparams(("arbitrary", "arbitrary")),
        name="modulation",
    )(c_all, w_mod, b_mod.reshape(DEPTH, 1, n))


def _mod_norm(x, g, sc, sh):
    ms = jnp.mean(x * x, axis=-1, keepdims=True)
    return (x * lax.rsqrt(ms + EPS) * g) * (1.0 + sc) + sh


def _rms(x, g):
    ms = jnp.mean(x * x, axis=-1, keepdims=True)
    return x * lax.rsqrt(ms + EPS) * g


def _inproj_kernel(x_ref, sh_ref, sc_ref, g_ref, wa_ref, wm_ref, wg_ref, gq_ref, gkv_ref,
                   wq_ref, wqr_ref, wk_ref, wv_ref, cq_ref, sq_ref, ck_ref, sk_ref,
                   proj_ref, q_ref, k_ref, v_ref):
    xn = _mod_norm(x_ref[0], g_ref[...], sc_ref[0], sh_ref[0]).astype(BF16)
    proj_ref[0, :, :COL_GATE] = _mm(xn, wa_ref[...]).astype(proj_ref.dtype)
    proj_ref[0, :, COL_GATE:] = _mm(xn, wg_ref[...]).astype(proj_ref.dtype)
    mid = _mm(xn, wm_ref[...])
    cq = mid[:, :Q_LORA]
    ckv = mid[:, Q_LORA:Q_LORA + KV_LORA]
    o = Q_LORA + KV_LORA
    kpe = mid[:, o:o + LANE]
    kpe_rot = mid[:, o + LANE:o + 2 * LANE]
    qn = _rms(cq, gq_ref[...]).astype(BF16)
    q = _mm(qn, wq_ref[...])
    qr = _mm(qn, wqr_ref[...])
    kvn = _rms(ckv, gkv_ref[...]).astype(BF16)
    k = _mm(kvn, wk_ref[...])
    v = _mm(kvn, wv_ref[...])
    cq_t, sq_t = cq_ref[...], sq_ref[...]
    kpe_r = kpe * ck_ref[...] + kpe_rot * sk_ref[...]
    ones_col = (lax.broadcasted_iota(jnp.int32, (1, HEAD_PAD), 1) == V_HEAD).astype(F32)
    for h in range(MLA_HEADS):
        sl = slice(h * HEAD_PAD, (h + 1) * HEAD_PAD)
        q_ref[0, h] = (q[:, sl] * cq_t + qr[:, sl] * sq_t).astype(q_ref.dtype)
        k_ref[0, h] = (k[:, sl] + kpe_r).astype(k_ref.dtype)
        v_ref[0, h] = (v[:, sl] + ones_col).T[:VT_ROWS].astype(v_ref.dtype)


def _inproj(h, mod, mrow, g, ws, gq, gkv, wq, wqr, wk, wv, cq, sq, ck, sk, l, tt):
    b_, L, _ = h.shape
    layer = lambda b, i: (l, 0, 0)
    resident = dict(pipeline_mode=pl.Buffered(1))
    hw = MLA_HEADS * HEAD_PAD
    table = pl.BlockSpec((tt, LANE), lambda b, i: (i, 0))
    heads = pl.BlockSpec((1, MLA_HEADS, tt, HEAD_PAD), lambda b, i: (b, 0, i, 0))
    return pl.pallas_call(
        _inproj_kernel,
        grid=(b_, L // tt),
        in_specs=[pl.BlockSpec((1, tt, D_MODEL), lambda b, i: (b, i, 0)),
                  pl.BlockSpec((1, 1, D_MODEL), lambda b, i: (mrow(b), 0, 0)),
                  pl.BlockSpec((1, 1, D_MODEL), lambda b, i: (mrow(b), 0, 1)),
                  pl.BlockSpec((None, 1, D_MODEL), layer)]
        + [pl.BlockSpec((None, D_MODEL, w.shape[2]), layer, **resident) for w in ws]
        + [pl.BlockSpec((None, 1, Q_LORA), layer), pl.BlockSpec((None, 1, KV_LORA), layer),
           pl.BlockSpec((None, Q_LORA, hw), layer, **resident), pl.BlockSpec((None, Q_LORA, hw), layer, **resident),
           pl.BlockSpec((None, KV_LORA, hw), layer, **resident), pl.BlockSpec((None, KV_LORA, hw), layer, **resident),
           table, table, table, table],
        out_specs=[pl.BlockSpec((1, tt, N_PROJ), lambda b, i: (b, i, 0)), heads, heads,
                   pl.BlockSpec((1, MLA_HEADS, VT_ROWS, tt), lambda b, i: (b, 0, 0, i))],
        out_shape=[jax.ShapeDtypeStruct((b_, L, N_PROJ), BF16)]
        + [jax.ShapeDtypeStruct((b_, MLA_HEADS, L, HEAD_PAD), BF16)] * 2
        + [jax.ShapeDtypeStruct((b_, MLA_HEADS, VT_ROWS, L), BF16)],
        compiler_params=_cparams(("arbitrary", "arbitrary")),
        name="inproj",
    )(h, mod, mod, g, *ws, gq, gkv, wq, wqr, wk, wv, cq, sq, ck, sk)


def _seqlocal_kernel(cur_ref, prev_ref, next_ref, pw_ref, ps_ref, cw_ref, cb_ref,
                     a_ref, x0_ref, zvt_ref, *, tt, L):
    i = pl.program_id(1)
    nt = pl.num_programs(1)
    keep_prev = (i > 0).astype(F32)
    keep_next = (i < nt - 1).astype(F32)
    n = tt + 2 * HALO

    def ext(lo, hi):
        return jnp.concatenate([prev_ref[0, :, lo:hi].astype(F32) * keep_prev,
                                cur_ref[0, :, lo:hi].astype(F32),
                                next_ref[0, :, lo:hi].astype(F32) * keep_next], axis=0)

    def roll(x, s):
        return pltpu.roll(x, s % n, 0)

    t = i * tt + lax.broadcasted_iota(jnp.int32, (tt, POOL_GROUP), 0)
    for gi, w in enumerate(POOL_WINDOWS):
        lo = COL_POOL + gi * POOL_GROUP
        xg = ext(lo, lo + POOL_GROUP)
        s = xg + roll(xg, 1)
        if w >= 4:
            s = roll(s, 1) + roll(s, -1)
        if w >= 8:
            s = roll(s, 2) + roll(s, -2)
        if w >= 16:
            s = roll(s, 4) + roll(s, -4)
        cnt = (jnp.minimum(t + w // 2, L) - jnp.maximum(t - w // 2, 0)).astype(F32)
        u = s[HALO:HALO + tt] / cnt - xg[HALO:HALO + tt]
        ug = _mm(u.astype(BF16), pw_ref[gi]) * ps_ref[:, gi * POOL_GROUP:(gi + 1) * POOL_GROUP]
        a_ref[0, :, gi * POOL_GROUP:(gi + 1) * POOL_GROUP] = ug.astype(a_ref.dtype)

    def conv(part, j):
        c0 = part * HY_WIDTH + j * LANE
        xe = ext(COL_HY + c0, COL_HY + c0 + LANE)
        y = (roll(xe, 1) * cw_ref[0:1, c0:c0 + LANE] + xe * cw_ref[1:2, c0:c0 + LANE]
             + roll(xe, -1) * cw_ref[2:3, c0:c0 + LANE] + cb_ref[:, c0:c0 + LANE])
        return y[HALO:HALO + tt]

    for j in range(HY_WIDTH // LANE):
        x0_ref[0, :, j * LANE:(j + 1) * LANE] = conv(0, j).astype(x0_ref.dtype)
        zv = conv(2, j) * conv(1, j)
        zvt_ref[0, j * LANE:(j + 1) * LANE, :] = zv.T


def _seqlocal(proj, pool_w, pool_scale, conv_w, conv_b, l, tt):
    b_, L, _ = proj.shape
    hb = tt // HALO
    nh = L // HALO
    wide = COL_GATE
    return pl.pallas_call(
        functools.partial(_seqlocal_kernel, tt=tt, L=L),
        grid=(b_, L // tt),
        in_specs=[pl.BlockSpec((1, tt, wide), lambda b, i: (b, i, 0)),
                  pl.BlockSpec((1, HALO, wide), lambda b, i: (b, jnp.maximum(i * hb - 1, 0), 0)),
                  pl.BlockSpec((1, HALO, wide), lambda b, i: (b, jnp.minimum((i + 1) * hb, nh - 1), 0)),
                  pl.BlockSpec((None, len(POOL_WINDOWS), POOL_GROUP, POOL_GROUP), lambda b, i: (l, 0, 0, 0)),
                  pl.BlockSpec((None, 1, POOL_WIDTH), lambda b, i: (l, 0, 0)),
                  pl.BlockSpec((None, 3, 3 * HY_WIDTH), lambda b, i: (l, 0, 0)),
                  pl.BlockSpec((None, 1, 3 * HY_WIDTH), lambda b, i: (l, 0, 0))],
        out_specs=[pl.BlockSpec((1, tt, POOL_WIDTH), lambda b, i: (b, i, 0)),
                   pl.BlockSpec((1, tt, HY_WIDTH), lambda b, i: (b, i, 0)),
                   pl.BlockSpec((1, HY_WIDTH, tt), lambda b, i: (b, 0, i))],
        out_shape=[jax.ShapeDtypeStruct((b_, L, POOL_WIDTH), BF16),
                   jax.ShapeDtypeStruct((b_, L, HY_WIDTH), BF16),
                   jax.ShapeDtypeStruct((b_, HY_WIDTH, L), F32)],
        compiler_params=_cparams(("arbitrary", "arbitrary")),
        name="seqlocal",
    )(proj, proj, proj, pool_w, pool_scale, conv_w, conv_b)


def _filter_kernel(z_ref, w1_ref, b1_ref, f1_ref, w2_ref, b2_ref, f2_ref, w3_ref, dec_ref,
                   filt_ref, l1_ref, *, tr, L):
    i = pl.program_id(0)
    half = tr // 2
    z = z_ref[...]
    hid = jnp.sin(f1_ref[...] * (_dot3(z, w1_ref[...]) + b1_ref[...]))
    hid = jnp.sin(f2_ref[...] * (_dot3(hid, w2_ref[...]) + b2_ref[...]))
    decay = jnp.abs(dec_ref[0])
    total = None
    for s in range(2):
        h = _dot3(hid, w3_ref[0, s]) * jnp.exp(-z[:, s * HY_HALF:s * HY_HALF + 1] * decay)
        row = i * tr + s * half + lax.broadcasted_iota(jnp.int32, h.shape, 0)
        h = jnp.where(row == L, 0.0, h)
        filt_ref[:, s * half:(s + 1) * half] = h.T
        part = jnp.sum(jnp.abs(h), axis=0, keepdims=True)
        total = part if total is None else total + part

    @pl.when(i == 0)
    def _():
        l1_ref[...] = jnp.zeros_like(l1_ref)

    l1_ref[...] += total


def _hyena_filter(zpacked, w1, b1, f1, w2, b2, f2, w3, dec, l, L, tr):
    n = 2 * L
    nhalf = L // tr
    full = lambda i: (0, 0)
    mat = pl.BlockSpec((None, LANE, LANE), lambda i: (l, 0, 0))
    vec = pl.BlockSpec((None, 1, LANE), lambda i: (l, 0, 0))
    return pl.pallas_call(
        functools.partial(_filter_kernel, tr=tr, L=L),
        grid=(n // tr,),
        in_specs=[pl.BlockSpec((tr // 2, LANE), lambda i: (i, 0)),
                  mat, vec, vec, mat, vec, vec,
                  pl.BlockSpec((None, 1, 2, LANE, HY_WIDTH), lambda i: (l, i // nhalf, 0, 0, 0)),
                  pl.BlockSpec((None, 1, 1, HY_WIDTH), lambda i: (l, i // nhalf, 0, 0))],
        out_specs=[pl.BlockSpec((HY_WIDTH, tr), lambda i: (0, i)),
                   pl.BlockSpec((1, HY_WIDTH), full)],
        out_shape=[jax.ShapeDtypeStruct((HY_WIDTH, n), F32),
                   jax.ShapeDtypeStruct((1, HY_WIDTH), F32)],
        compiler_params=_cparams(("arbitrary",)),
        name="hyena_filter",
    )(zpacked, w1, b1, f1, w2, b2, f2, w3, dec)


FFT_N1 = 128
FFT_N2 = 128
CONV_CB = 16


def _lconv_kernel(bias_ref, l1_ref, z_ref, f_ref, f1d_h, f1d_l, f1f_h, f1f_l, tr_ref, ti_ref,
                  f2_h, f2_l, g2_h, g2_l, e1_h, e1_l, y_ref, bs_ref, bf_ref, cm_ref, ys_ref, *, layer):
    blk = pl.program_id(0)
    tw_r = tr_ref[...]
    tw_i = ti_ref[...]
    n2 = FFT_N2
    def twiddled(a, dst, c):
        rows = slice(c * FFT_N1, (c + 1) * FFT_N1)
        ar, ai = a[:FFT_N1], a[FFT_N1:]
        dst[rows, :n2] = ar * tw_r - ai * tw_i
        dst[rows, n2:] = ar * tw_i + ai * tw_r

    def filt(c):
        inv_l1 = 1.0 / jnp.full((1, n2), l1_ref[0, blk * CONV_CB + c], F32)
        return f_ref[c] * inv_l1

    for c in range(0, CONV_CB, 2):
        zs = [jnp.concatenate([z_ref[0, c + d], z_ref[1, c + d]], axis=0) for d in range(2)]
        for d, a in enumerate(_cdot3_pair(f1d_h[...], f1d_l[...], zs[0], zs[1])):
            twiddled(a, bs_ref, c + d)
        for d, a in enumerate(_cdot3_pair(f1f_h[...], f1f_l[...], filt(c), filt(c + 1))):
            twiddled(a, bf_ref, c + d)
    x = _dot3c(bs_ref[...], f2_h[...], f2_l[...])
    hf = _dot3c(bf_ref[...], f2_h[...], f2_l[...])
    xr, xi = x[:, :n2], x[:, n2:]
    hr, hi = hf[:, :n2], hf[:, n2:]
    y = jnp.concatenate([xr * hr - xi * hi, xr * hi + xi * hr], axis=1)
    cm_ref[...] = _dot3c(y, g2_h[...], g2_l[...])
    half = FFT_N1 // 2
    def untwiddled(c):
        rows = slice(c * FFT_N1, (c + 1) * FFT_N1)
        cr = cm_ref[rows, :n2]
        ci = cm_ref[rows, n2:]
        return jnp.concatenate([cr * tw_r + ci * tw_i, ci * tw_r - cr * tw_i], axis=0)

    for c in range(0, CONV_CB, 2):
        for d, yc in enumerate(_cdot3_pair(e1_h[...], e1_l[...], untwiddled(c), untwiddled(c + 1))):
            bias = bias_ref[layer, blk * CONV_CB + c + d]
            ys_ref[0, c + d] = yc[:half] + z_ref[0, c + d] * bias
            ys_ref[1, c + d] = yc[half:] + z_ref[1, c + d] * bias
    sub = 8
    for b in range(2):
        for cg in range(CONV_CB // sub):
            for ng in range(half // sub):
                blk8 = ys_ref[b, cg * sub:(cg + 1) * sub, ng * sub:(ng + 1) * sub, :]
                t8 = jnp.swapaxes(blk8, 0, 1)
                for j in range(sub):
                    n1 = ng * sub + j
                    y_ref[b, cg * sub:(cg + 1) * sub, n1 * FFT_N2:(n1 + 1) * FFT_N2] = t8[j]


def _lconv_consts():
    n1, n2 = FFT_N1, FFT_N2
    n = n1 * n2
    idx = np.arange(128)
    ang1 = 2.0 * np.pi * ((idx[:, None] * idx[None, :]) % 128) / 128.0
    fr, fi = np.cos(ang1), -np.sin(ang1)
    angt = 2.0 * np.pi * (idx[:, None] * idx[None, :]) / n
    tw_r, tw_i = np.cos(angt), -np.sin(angt)
    h = n1 // 2
    f1d = np.block([[fr[:, :h], -fi[:, :h]], [fi[:, :h], fr[:, :h]]])
    f1f = np.concatenate([fr, fi], axis=0)
    f2 = np.block([[fr, fi], [-fi, fr]])
    g2 = np.block([[fr, -fi], [fi, fr]])
    er, ei = fr[:h] / n, -fi[:h] / n
    e1 = np.block([[er, -ei], [ei, er]])

    return (*_hl(f1d), *_hl(f1f), np.asarray(tw_r, np.float32), np.asarray(tw_i, np.float32),
            *_hl(f2), *_hl(g2), *_hl(e1))


def _long_conv_lat(zvt, filt_t, l1, bias, l):
    b_, C, L = zvt.shape
    assert b_ == 2 and L == FFT_N1 * FFT_N2 // 2
    z4 = zvt.reshape(b_, C, FFT_N1 // 2, FFT_N2)
    f3 = filt_t.reshape(C, FFT_N1, FFT_N2)
    consts = _lconv_consts()
    cb = CONV_CB

    def cspec(a):
        return pl.BlockSpec(a.shape, lambda i: (0,) * a.ndim)

    return pl.pallas_call(
        functools.partial(_lconv_kernel, layer=l),
        grid=(C // cb,),
        in_specs=[pl.BlockSpec(memory_space=pltpu.SMEM), pl.BlockSpec(memory_space=pltpu.SMEM),
                  pl.BlockSpec((b_, cb, FFT_N1 // 2, FFT_N2), lambda i: (0, i, 0, 0)),
                  pl.BlockSpec((cb, FFT_N1, FFT_N2), lambda i: (i, 0, 0))] + [cspec(a) for a in consts],
        out_specs=pl.BlockSpec((b_, cb, L), lambda i: (0, i, 0)),
        out_shape=jax.ShapeDtypeStruct(zvt.shape, F32),
        scratch_shapes=[pltpu.VMEM((cb * FFT_N1, 2 * FFT_N2), F32),
                        pltpu.VMEM((cb * FFT_N1, 2 * FFT_N2), F32),
                        pltpu.VMEM((cb * FFT_N1, 2 * FFT_N2), F32),
                        pltpu.VMEM((b_, cb, FFT_N1 // 2, FFT_N2), F32)],
        compiler_params=_cparams(("arbitrary",)),
        name="long_conv",
    )(bias, l1, z4, f3, *consts)


def _cconv_kernel(z_ref, f_ref, l1_ref, bias_ref, fd_h, fd_l, ff_h, ff_l, fi_h, fi_l, y_ref, *, L):
    zr, zi = z_ref[0], z_ref[1]
    x = _dot3c(jnp.concatenate([zr, zi], axis=1), fd_h[...], fd_l[...])
    hf = _dot3c(f_ref[...] * (1.0 / l1_ref[...]), ff_h[...], ff_l[...])
    n = 2 * L
    xr, xi = x[:, :n], x[:, n:]
    hr, hi = hf[:, :n], hf[:, n:]
    y = jnp.concatenate([xr * hr - xi * hi, xr * hi + xi * hr], axis=1)
    out = _dot3c(y, fi_h[...], fi_l[...])
    bias = bias_ref[...]
    y_ref[0] = out[:, :L] + zr * bias
    y_ref[1] = out[:, L:] + zi * bias


def _cconv_consts(L):
    n = 2 * L
    idx = np.arange(n)
    ang = 2.0 * np.pi * ((idx[:, None] * idx[None, :]) % n) / n
    fr, fi = np.cos(ang), -np.sin(ang)
    fd = np.block([[fr[:L], fi[:L]], [-fi[:L], fr[:L]]])
    ff = np.concatenate([fr, fi], axis=1)
    er, ei = fr[:, :L] / n, -fi[:, :L] / n
    finv = np.block([[er, ei], [-ei, er]])

    return (*_hl(fd), *_hl(ff), *_hl(finv))


def _long_conv_ctx(zvt, filt_t, l1, bias):
    b_, C, L = zvt.shape
    consts = _cconv_consts(L)
    bias_b = jnp.broadcast_to(bias[:, None], (C, L))
    l1_col = l1.reshape(C, 1)

    def cspec(a):
        return pl.BlockSpec(a.shape, lambda i: (0,) * a.ndim)

    return pl.pallas_call(
        functools.partial(_cconv_kernel, L=L),
        grid=(1,),
        in_specs=[cspec(zvt), cspec(filt_t), cspec(l1_col), cspec(bias_b)] + [cspec(a) for a in consts],
        out_specs=cspec(zvt),
        out_shape=jax.ShapeDtypeStruct(zvt.shape, F32),
        compiler_params=_cparams(("arbitrary",)),
        name="long_conv_ctx",
    )(zvt, filt_t, l1_col, bias_b, *consts)


def _attn_kernel(q_ref, *refs, chunks):
    o_ref = refs[-1]
    tq = q_ref.shape[2]
    q = [q_ref[0, hh] for hh in range(2)]
    m = [jnp.full((tq, 1), -1e30, F32) for _ in range(2)]
    m_row = [jnp.full((1, tq), -1e30, F32) for _ in range(2)]
    acc = [jnp.zeros((VT_ROWS, tq), F32) for _ in range(2)]
    work = [(src, start, size, hh) for (src, start, size) in chunks for hh in range(2)]
    nt = (((1,), (1,)), ((), ()))

    def scores(n):
        src, start, size, hh = work[n]
        k = refs[2 * src][0, hh, start:start + size, :]
        return lax.dot_general(q[hh], k, nt, preferred_element_type=F32)

    def to_row(col):
        wide = jnp.broadcast_to(col, (tq, LANE))
        return jnp.concatenate([wide[i * LANE:(i + 1) * LANE].T[0:1] for i in range(tq // LANE)], axis=1)

    s_next = scores(0)
    for n, (src, start, size, hh) in enumerate(work):
        s = s_next
        if n + 1 < len(work):
            s_next = scores(n + 1)
        vt = refs[2 * src + 1][0, hh, :, start:start + size]
        m_new = jnp.maximum(m[hh], jnp.max(s, axis=-1, keepdims=True))
        p = jnp.exp2(s - m_new)
        m_new_row = to_row(m_new)
        alpha_row = jnp.exp2(m_row[hh] - m_new_row)
        acc[hh] = alpha_row * acc[hh] + lax.dot_general(vt, p.astype(BF16), nt, preferred_element_type=F32)
        m[hh] = m_new
        m_row[hh] = m_new_row
    o_t = jnp.concatenate([acc[hh][:V_HEAD] / acc[hh][V_HEAD:V_HEAD + 1] for hh in range(2)], axis=0)
    o_ref[0] = o_t.T.astype(o_ref.dtype)


def _attention(q, kv_sources, tq, tk):
    b_, H, L, _ = q.shape
    chunks = []
    in_specs = [pl.BlockSpec((1, 2, tq, HEAD_PAD), lambda b, hp, i: (b, hp, i, 0))]
    args = [q]
    for src, (k, v) in enumerate(kv_sources):
        S = k.shape[2]
        step = min(tk, S)
        chunks += [(src, start, step) for start in range(0, S, step)]
        in_specs += [pl.BlockSpec((1, 2, S, HEAD_PAD), lambda b, hp, i: (b, hp, 0, 0)),
                     pl.BlockSpec((1, 2, VT_ROWS, S), lambda b, hp, i: (b, hp, 0, 0))]
        args += [k, v]
    return pl.pallas_call(
        functools.partial(_attn_kernel, chunks=tuple(chunks)),
        grid=(b_, H // 2, L // tq),
        in_specs=in_specs,
        out_specs=pl.BlockSpec((1, tq, LANE), lambda b, hp, i: (b, i, hp)),
        out_shape=jax.ShapeDtypeStruct((b_, L, H * V_HEAD), BF16),
        compiler_params=_cparams(("arbitrary", "arbitrary", "arbitrary")),
        name="attention",
    )(*args)


def _mixffn_kernel(a_ref, x0_ref, yt_ref, o_ref, ga_ref, gb_ref, gc_ref, h_ref, g1_ref, sh2_ref, sc2_ref, g2_ref,
                   gn_ref, wp_ref, wh_ref, wo_ref, wout_ref, w1_ref, w2_ref, fg_ref, out_ref, *, final, nsplit):
    hy = (x0_ref[0].astype(F32) * yt_ref[0].T).astype(BF16)
    a = _mm(a_ref[0], wp_ref[...])
    b = _mm(hy, wh_ref[...])
    cm = _mm(o_ref[0], wo_ref[...])
    m = (jax.nn.sigmoid(ga_ref[0].astype(F32)) * a + jax.nn.sigmoid(gb_ref[0].astype(F32)) * b
         + jax.nn.sigmoid(gc_ref[0].astype(F32)) * cm)
    x = h_ref[0] + g1_ref[0] * _mm(m.astype(BF16), wout_ref[...])
    xn = _mod_norm(x, gn_ref[...], sc2_ref[0], sh2_ref[0]).astype(BF16)
    tf = D_FF // nsplit
    acc = None
    for k in range(nsplit):
        u = jnp.maximum(_mm(xn, w1_ref[:, k * tf:(k + 1) * tf]), 0.0)
        part = _mm((u * u).astype(BF16), w2_ref[k * tf:(k + 1) * tf, :])
        acc = part if acc is None else acc + part
    y = x + g2_ref[0] * acc
    if final:
        y = _rms(y, fg_ref[...])
    out_ref[0] = y


def _mixffn(a_pre, x0c, yt, o, proj, h, mod, mrow, wp, wh, wo, wout, gn, w1, w2, final_g, l, tt, final):
    b_, L, _ = h.shape
    layer = lambda b, i: (l, 0, 0)
    row = lambda b, i: (b, i, 0)
    gcol = COL_GATE // D_MODEL
    resident = dict(pipeline_mode=pl.Buffered(1))
    modrow = lambda k: pl.BlockSpec((1, 1, D_MODEL), lambda b, i: (mrow(b), 0, k))
    return pl.pallas_call(
        functools.partial(_mixffn_kernel, final=final, nsplit=2),
        grid=(b_, L // tt),
        in_specs=[pl.BlockSpec((1, tt, POOL_WIDTH), row),
                  pl.BlockSpec((1, tt, HY_WIDTH), row),
                  pl.BlockSpec((1, HY_WIDTH, tt), lambda b, i: (b, 0, i)),
                  pl.BlockSpec((1, tt, MLA_HEADS * V_HEAD), row),
                  pl.BlockSpec((1, tt, D_MODEL), lambda b, i: (b, i, gcol)),
                  pl.BlockSpec((1, tt, D_MODEL), lambda b, i: (b, i, gcol + 1)),
                  pl.BlockSpec((1, tt, D_MODEL), lambda b, i: (b, i, gcol + 2)),
                  pl.BlockSpec((1, tt, D_MODEL), row),
                  modrow(2), modrow(3), modrow(4), modrow(5),
                  pl.BlockSpec((None, 1, D_MODEL), layer),
                  pl.BlockSpec((None, POOL_WIDTH, D_MODEL), layer, **resident),
                  pl.BlockSpec((None, HY_WIDTH, D_MODEL), layer, **resident),
                  pl.BlockSpec((None, MLA_HEADS * V_HEAD, D_MODEL), layer, **resident),
                  pl.BlockSpec((None, D_MODEL, D_MODEL), layer, **resident),
                  pl.BlockSpec((None, D_MODEL, D_FF), layer, **resident),
                  pl.BlockSpec((None, D_FF, D_MODEL), layer, **resident),
                  pl.BlockSpec((1, D_MODEL), lambda b, i: (0, 0))],
        out_specs=pl.BlockSpec((1, tt, D_MODEL), row),
        out_shape=jax.ShapeDtypeStruct(h.shape, F32),
        compiler_params=pltpu.CompilerParams(dimension_semantics=("arbitrary", "arbitrary"),
                                             vmem_limit_bytes=VMEM_LIMIT_FUSED),
        name="mixffn",
    )(a_pre, x0c, yt, o, proj, proj, proj, h, mod, mod, mod, mod, gn, wp, wh, wo, wout, w1, w2, final_g)


def _rope_tables(n_rows):
    rows = np.repeat(np.arange(n_rows, dtype=np.float64), GRID_W)
    cols = np.tile(np.arange(GRID_W, dtype=np.float64), n_rows)
    half = QK_ROPE // 2
    inv = ROPE_BASE ** (-np.arange(0, half, 2, dtype=np.float64) / half)
    ar = rows[:, None] * inv
    ac = cols[:, None] * inv
    ang = np.concatenate([ar, ar, ac, ac], axis=-1)
    return np.cos(ang), np.sin(ang)


def _head_tables(cos, sin, L):
    ones = np.ones((L, QK_NOPE))
    z64 = np.zeros((L, QK_NOPE))
    z32 = np.zeros((L, HEAD_PAD - QK_DIM))
    if cos is None:
        cos = np.ones((L, QK_ROPE))
        sin = np.zeros((L, QK_ROPE))
    cq = np.concatenate([ones, cos, z32], axis=1) * (MLA_SCALE * LOG2E)
    sq = np.concatenate([z64, sin, z32], axis=1) * (MLA_SCALE * LOG2E)
    ck = np.concatenate([z64, cos, z32], axis=1)
    sk = np.concatenate([z64, sin, z32], axis=1)
    return tuple(np.asarray(t, np.float32) for t in (cq, sq, ck, sk))


def _hy_embed(L, tr):
    t = np.linspace(0.0, 1.0, L)[:, None]
    omega = 2.0 * np.pi * np.arange(L)[:, None] / L
    bands = np.linspace(1e-4, HY_BANDS - 1, HY_BANDS)[None, :]
    z = np.concatenate([t, np.cos(omega * bands), -np.sin(omega * bands)], axis=-1)
    zfull = np.concatenate([z, np.zeros((1, HY_EMB)), z[:0:-1]], axis=0)
    zfull = np.pad(zfull, ((0, 0), (0, HY_HALF - HY_EMB)))
    packed = zfull.reshape(2 * L // tr, 2, tr // 2, HY_HALF).transpose(0, 2, 1, 3).reshape(L, 2 * HY_HALF)
    return np.asarray(packed, np.float32)


def _blockdiag2(a):
    a = _pad2(a, HY_HALF, HY_HALF)
    z = jnp.zeros_like(a)
    return jnp.concatenate([jnp.concatenate([a, z], axis=-1), jnp.concatenate([z, a], axis=-1)], axis=-2)


def _pack_w_in(w):
    z = lambda n: jnp.zeros(w.shape[:-1] + (n,), BF16)
    o_q = 512 + 3 * HY_WIDTH
    o_kv = o_q + Q_LORA
    o_pe = o_kv + KV_LORA
    o_gate = o_pe + QK_ROPE
    kpe = w[..., o_pe:o_gate].astype(BF16)
    mid = jnp.concatenate([
        w[..., o_q:o_pe].astype(BF16),
        z(QK_NOPE), kpe, z(HEAD_PAD - QK_DIM),
        z(QK_NOPE), _rot_half(kpe), z(HEAD_PAD - QK_DIM),
        z(LANE)], axis=-1)
    return w[..., :o_q].astype(BF16), mid, w[..., o_gate:].astype(BF16)


def _pack_heads(w_uq, w_ukv):
    lead = w_uq.shape[:-2]
    wq3 = w_uq.astype(BF16).reshape(*lead, Q_LORA, MLA_HEADS, QK_DIM)
    pad = jnp.zeros((*lead, Q_LORA, MLA_HEADS, HEAD_PAD - QK_DIM), BF16)
    wq = jnp.concatenate([wq3, pad], axis=-1).reshape(*lead, Q_LORA, -1)
    wqr = jnp.concatenate([jnp.zeros((*lead, Q_LORA, MLA_HEADS, QK_NOPE), BF16), _rot_half(wq3[..., QK_NOPE:]), pad],
                          axis=-1).reshape(*lead, Q_LORA, -1)
    wkv3 = w_ukv.astype(BF16).reshape(*lead, KV_LORA, MLA_HEADS, QK_NOPE + V_HEAD)
    wk = jnp.concatenate([wkv3[..., :QK_NOPE], jnp.zeros((*lead, KV_LORA, MLA_HEADS, HEAD_PAD - QK_NOPE), BF16)],
                         axis=-1).reshape(*lead, KV_LORA, -1)
    wv = jnp.concatenate([wkv3[..., QK_NOPE:], jnp.zeros((*lead, KV_LORA, MLA_HEADS, HEAD_PAD - V_HEAD), BF16)],
                         axis=-1).reshape(*lead, KV_LORA, -1)
    return wq, wqr, wk, wv


def _pad2(a, r, c):
    pads = [(0, 0)] * (a.ndim - 2) + [(0, r - a.shape[-2]), (0, c - a.shape[-1])]
    return jnp.pad(a, pads)


TT_LAT = 512
TQ = 1024
TK = 512
TR_LAT = 1024


def kernel(x, c, ctx, c_ctx, w_mod, b_mod, norm1_g, norm2_g, w_in, pool_w, pool_scale, pool_out, hy_conv_w, hy_conv_b, hy_f_w1, hy_f_b1, hy_f_freq1, hy_f_w2, hy_f_b2, hy_f_freq2, hy_f_w3, hy_decay, hy_bias, hy_out, q_norm_g, w_uq, kv_norm_g, w_ukv, w_o, w_out, w_ff1, w_ff2, final_g):
    B, L, _ = x.shape
    Lc = ctx.shape[1]
    cos, sin = _rope_tables(L // GRID_W)
    tabs_lat = _head_tables(cos, sin, L)
    tabs_ctx = _head_tables(None, None, Lc)
    z_lat = _hy_embed(L, TR_LAT)
    z_ctx = _hy_embed(Lc, Lc)

    c_all = jnp.concatenate([c, c_ctx[None], jnp.zeros((MOD_ROWS - B - 1, D_MODEL), F32)], axis=0)
    mod = _modulation(c_all, w_mod, b_mod).reshape(DEPTH * MOD_ROWS, 1, N_MOD * D_MODEL)

    row3 = lambda a: a[:, None, :]
    g1n, g2n, gq, gkv = row3(norm1_g), row3(norm2_g), row3(q_norm_g), row3(kv_norm_g)
    w_in_p = _pack_w_in(w_in)
    wq, wqr, wk, wv = _pack_heads(w_uq, w_ukv)
    pw, ps = pool_w.astype(BF16), row3(pool_scale)
    cw, cb = hy_conv_w, row3(hy_conv_b)
    wp, wh, wo, wout = pool_out.astype(BF16), hy_out.astype(BF16), w_o.astype(BF16), w_out.astype(BF16)
    wf1, wf2 = w_ff1.astype(BF16), w_ff2.astype(BF16)
    twice = lambda v: jnp.tile(_pad2(row3(v), 1, HY_HALF), (1, 1, 2))
    fw1, fw2 = _blockdiag2(hy_f_w1), _blockdiag2(hy_f_w2)
    fb1, ff1, fb2, ff2 = twice(hy_f_b1), twice(hy_f_freq1), twice(hy_f_b2), twice(hy_f_freq2)
    w3d = hy_f_w3.reshape(DEPTH, HY_HALF, 2, HY_WIDTH).transpose(0, 2, 1, 3)
    z3 = jnp.zeros_like(w3d)
    fw3 = jnp.stack([jnp.concatenate([w3d, z3], axis=2), jnp.concatenate([z3, w3d], axis=2)], axis=2)
    dec = hy_decay[:, :, None, :]
    fg = final_g[None]

    h, hc = x, ctx
    for l in range(DEPTH):
        last = l == DEPTH - 1
        row_lat = lambda b, l=l: l * MOD_ROWS + b
        row_ctx = lambda b, l=l: l * MOD_ROWS + B

        def stream_pre(hs, mrow, tabs, tt):
            return _inproj(hs, mod, mrow, g1n, w_in_p, gq, gkv, wq, wqr, wk, wv, *tabs, l, tt)

        def stream_mix(hs, mrow, proj, q, kv_sources, zemb, tt, tq, tk, tr, final):
            Ls = hs.shape[1]
            a_pre, x0c, zvt = _seqlocal(proj, pw, ps, cw, cb, l, tt)
            filt_t, l1 = _hyena_filter(zemb, fw1, fb1, ff1, fw2, fb2, ff2, fw3, dec, l, Ls, tr)
            if Ls == FFT_N1 * FFT_N2 // 2:
                yt = _long_conv_lat(zvt, filt_t, l1, hy_bias, l)
            else:
                yt = _long_conv_ctx(zvt, filt_t, l1, hy_bias[l])
            o = _attention(q, kv_sources, tq, tk)
            return _mixffn(a_pre, x0c, yt, o, proj, hs, mod, mrow, wp, wh, wo, wout, g2n, wf1, wf2, fg, l, tt, final)

        proj_c, q_c, k_c, v_c = stream_pre(hc, row_ctx, tabs_ctx, Lc)
        proj_l, q_l, k_l, v_l = stream_pre(h, row_lat, tabs_lat, TT_LAT)
        h = stream_mix(h, row_lat, proj_l, q_l, [(k_c, v_c), (k_l, v_l)], z_lat, TT_LAT, TQ, TK, TR_LAT, last)
        if not last:
            hc = stream_mix(hc, row_ctx, proj_c, q_c, [(k_c, v_c)], z_ctx, Lc, Lc, Lc, Lc, False)
    return h
```

```python
import functools
import math

import jax
import jax.numpy as jnp
import numpy as np
from jax import lax
from jax.experimental import pallas as pl
from jax.experimental.pallas import tpu as pltpu

F32 = jnp.float32
BF16 = jnp.bfloat16

D_MODEL = 1024
DEPTH = 4
GRID_W = 64
EPS = 1e-6

POOL_WIDTH = 512
POOL_WINDOWS = (2, 4, 8, 16)
POOL_GROUP = 128
HY_WIDTH = 512
HY_EMB = 33
HY_BANDS = 16
HY_HALF = 64
MLA_HEADS = 8
QK_NOPE = 64
QK_ROPE = 32
QK_DIM = 96
V_HEAD = 64
Q_LORA = 384
KV_LORA = 256
MLA_SCALE = QK_DIM ** -0.5
LOG2E = math.log2(math.e)
ROPE_BASE = 10000.0
D_FF = 4 * D_MODEL
N_MOD = 6
MOD_ROWS = 8

LANE = 128
HALO = 16
HEAD_PAD = 128

COL_POOL = 0
COL_HY = 512
COL_GATE = 2048
N_PROJ = 5120

VMEM_LIMIT = 50 * 1024 * 1024
VMEM_LIMIT_FUSED = 58 * 1024 * 1024


def _cparams(sem):
    return pltpu.CompilerParams(dimension_semantics=sem, vmem_limit_bytes=VMEM_LIMIT)


def _mm(a, b):
    return jnp.dot(a, b, preferred_element_type=F32)


def _split(x):
    hi = x.astype(BF16)
    lo = (x - hi.astype(F32)).astype(BF16)
    return hi, lo


def _dot3(a, b):
    ah, al = _split(a)
    bh, bl = _split(b)
    return _mm(ah, bh) + _mm(al, bh) + _mm(ah, bl)


def _dot3c(a, bh, bl):
    ah, al = _split(a)
    return _mm(ah, bh) + _mm(al, bh) + _mm(ah, bl)


def _cdot3(ch, cl, b):
    bh, bl = _split(b)
    n = b.shape[1]
    r = _mm(ch, jnp.concatenate([bh, bl], axis=1))
    return r[:, :n] + r[:, n:] + _mm(cl, bh)


def _hl(m):
    m32 = np.asarray(m, np.float32)
    hi = m32.astype(BF16)
    lo = (m32 - hi.astype(np.float32)).astype(BF16)
    return hi, lo


def _rot_half(x):
    q = QK_ROPE // 4
    return jnp.concatenate([-x[..., q:2 * q], x[..., :q], -x[..., 3 * q:], x[..., 2 * q:3 * q]], axis=-1)


def _mod_kernel(c_ref, w_ref, b_ref, o_ref):
    c = c_ref[...]
    s = c * jax.nn.sigmoid(c)
    o_ref[0] = _dot3(s, w_ref[0]) + b_ref[0]


def _modulation(c_all, w_mod, b_mod):
    tn = 1536
    n = N_MOD * D_MODEL
    return pl.pallas_call(
        _mod_kernel,
        grid=(DEPTH, n // tn),
        in_specs=[pl.BlockSpec((MOD_ROWS, D_MODEL), lambda l, j: (0, 0)),
                  pl.BlockSpec((1, D_MODEL, tn), lambda l, j: (l, 0, j)),
                  pl.BlockSpec((1, 1, tn), lambda l, j: (l, 0, j))],
        out_specs=pl.BlockSpec((1, MOD_ROWS, tn), lambda l, j: (l, 0, j)),
        out_shape=jax.ShapeDtypeStruct((DEPTH, MOD_ROWS, n), F32),
        compiler_params=_cparams(("arbitrary", "arbitrary")),
        name="modulation",
    )(c_all, w_mod, b_mod.reshape(DEPTH, 1, n))


def _mod_norm(x, g, sc, sh):
    ms = jnp.mean(x * x, axis=-1, keepdims=True)
    return (x * lax.rsqrt(ms + EPS) * g) * (1.0 + sc) + sh


def _rms(x, g):
    ms = jnp.mean(x * x, axis=-1, keepdims=True)
    return x * lax.rsqrt(ms + EPS) * g


def _inproj_kernel(x_ref, sh_ref, sc_ref, g_ref, wa_ref, wm_ref, wg_ref, gq_ref, gkv_ref,
                   wq_ref, wqr_ref, wk_ref, wv_ref, cq_ref, sq_ref, ck_ref, sk_ref,
                   proj_ref, q_ref, k_ref, v_ref):
    xn = _mod_norm(x_ref[0], g_ref[...], sc_ref[0], sh_ref[0]).astype(BF16)
    proj_ref[0, :, :COL_GATE] = _mm(xn, wa_ref[...]).astype(proj_ref.dtype)
    proj_ref[0, :, COL_GATE:] = _mm(xn, wg_ref[...]).astype(proj_ref.dtype)
    mid = _mm(xn, wm_ref[...])
    cq = mid[:, :Q_LORA]
    ckv = mid[:, Q_LORA:Q_LORA + KV_LORA]
    o = Q_LORA + KV_LORA
    kpe = mid[:, o:o + LANE]
    kpe_rot = mid[:, o + LANE:o + 2 * LANE]
    qn = _rms(cq, gq_ref[...]).astype(BF16)
    q = _mm(qn, wq_ref[...])
    qr = _mm(qn, wqr_ref[...])
    kvn = _rms(ckv, gkv_ref[...]).astype(BF16)
    k = _mm(kvn, wk_ref[...])
    v = _mm(kvn, wv_ref[...])
    cq_t, sq_t = cq_ref[...], sq_ref[...]
    kpe_r = kpe * ck_ref[...] + kpe_rot * sk_ref[...]
    lane = lax.broadcasted_iota(jnp.int32, (1, HEAD_PAD), 1)
    for h in range(MLA_HEADS):
        sl = slice(h * HEAD_PAD, (h + 1) * HEAD_PAD)
        q_ref[0, h] = (q[:, sl] * cq_t + qr[:, sl] * sq_t).astype(q_ref.dtype)
        k_ref[0, h] = (k[:, sl] + kpe_r).astype(k_ref.dtype)
        ones_col = (lane == (V_HEAD if h % 2 == 0 else 0)).astype(F32)
        v_ref[0, h] = (v[:, sl] + ones_col).astype(v_ref.dtype)


def _inproj(h, mod, mrow, g, ws, gq, gkv, wq, wqr, wk, wv, cq, sq, ck, sk, l, tt):
    b_, L, _ = h.shape
    layer = lambda b, i: (l, 0, 0)
    resident = dict(pipeline_mode=pl.Buffered(1))
    hw = MLA_HEADS * HEAD_PAD
    table = pl.BlockSpec((tt, LANE), lambda b, i: (i, 0))
    heads = pl.BlockSpec((1, MLA_HEADS, tt, HEAD_PAD), lambda b, i: (b, 0, i, 0))
    return pl.pallas_call(
        _inproj_kernel,
        grid=(b_, L // tt),
        in_specs=[pl.BlockSpec((1, tt, D_MODEL), lambda b, i: (b, i, 0)),
                  pl.BlockSpec((1, 1, D_MODEL), lambda b, i: (mrow(b), 0, 0)),
                  pl.BlockSpec((1, 1, D_MODEL), lambda b, i: (mrow(b), 0, 1)),
                  pl.BlockSpec((None, 1, D_MODEL), layer)]
        + [pl.BlockSpec((None, D_MODEL, w.shape[2]), layer, **resident) for w in ws]
        + [pl.BlockSpec((None, 1, Q_LORA), layer), pl.BlockSpec((None, 1, KV_LORA), layer),
           pl.BlockSpec((None, Q_LORA, hw), layer, **resident), pl.BlockSpec((None, Q_LORA, hw), layer, **resident),
           pl.BlockSpec((None, KV_LORA, hw), layer, **resident), pl.BlockSpec((None, KV_LORA, hw), layer, **resident),
           table, table, table, table],
        out_specs=[pl.BlockSpec((1, tt, N_PROJ), lambda b, i: (b, i, 0)), heads, heads, heads],
        out_shape=[jax.ShapeDtypeStruct((b_, L, N_PROJ), BF16)]
        + [jax.ShapeDtypeStruct((b_, MLA_HEADS, L, HEAD_PAD), BF16)] * 3,
        compiler_params=_cparams(("arbitrary", "arbitrary")),
        name="inproj",
    )(h, mod, mod, g, *ws, gq, gkv, wq, wqr, wk, wv, cq, sq, ck, sk)


def _seqlocal_kernel(cur_ref, prev_ref, next_ref, pw_ref, ps_ref, cw_ref, cb_ref,
                     a_ref, x0_ref, zvt_ref, *, tt, L):
    i = pl.program_id(1)
    nt = pl.num_programs(1)
    keep_prev = (i > 0).astype(F32)
    keep_next = (i < nt - 1).astype(F32)
    n = tt + 2 * HALO

    def ext(lo, hi):
        return jnp.concatenate([prev_ref[0, :, lo:hi].astype(F32) * keep_prev,
                                cur_ref[0, :, lo:hi].astype(F32),
                                next_ref[0, :, lo:hi].astype(F32) * keep_next], axis=0)

    def roll(x, s):
        return pltpu.roll(x, s % n, 0)

    t = i * tt + lax.broadcasted_iota(jnp.int32, (tt, POOL_GROUP), 0)
    for gi, w in enumerate(POOL_WINDOWS):
        lo = COL_POOL + gi * POOL_GROUP
        xg = ext(lo, lo + POOL_GROUP)
        s = xg + roll(xg, 1)
        if w >= 4:
            s = roll(s, 1) + roll(s, -1)
        if w >= 8:
            s = roll(s, 2) + roll(s, -2)
        if w >= 16:
            s = roll(s, 4) + roll(s, -4)
        cnt = (jnp.minimum(t + w // 2, L) - jnp.maximum(t - w // 2, 0)).astype(F32)
        u = s[HALO:HALO + tt] / cnt - xg[HALO:HALO + tt]
        ug = _mm(u.astype(BF16), pw_ref[gi]) * ps_ref[:, gi * POOL_GROUP:(gi + 1) * POOL_GROUP]
        a_ref[0, :, gi * POOL_GROUP:(gi + 1) * POOL_GROUP] = ug.astype(a_ref.dtype)

    def conv(part, j):
        c0 = part * HY_WIDTH + j * LANE
        xe = ext(COL_HY + c0, COL_HY + c0 + LANE)
        y = (roll(xe, 1) * cw_ref[0:1, c0:c0 + LANE] + xe * cw_ref[1:2, c0:c0 + LANE]
             + roll(xe, -1) * cw_ref[2:3, c0:c0 + LANE] + cb_ref[:, c0:c0 + LANE])
        return y[HALO:HALO + tt]

    for j in range(HY_WIDTH // LANE):
        x0_ref[0, :, j * LANE:(j + 1) * LANE] = conv(0, j).astype(x0_ref.dtype)
        zv = conv(2, j) * conv(1, j)
        zvt_ref[0, j * LANE:(j + 1) * LANE, :] = zv.T


def _seqlocal(proj, pool_w, pool_scale, conv_w, conv_b, l, tt):
    b_, L, _ = proj.shape
    hb = tt // HALO
    nh = L // HALO
    wide = COL_GATE
    return pl.pallas_call(
        functools.partial(_seqlocal_kernel, tt=tt, L=L),
        grid=(b_, L // tt),
        in_specs=[pl.BlockSpec((1, tt, wide), lambda b, i: (b, i, 0)),
                  pl.BlockSpec((1, HALO, wide), lambda b, i: (b, jnp.maximum(i * hb - 1, 0), 0)),
                  pl.BlockSpec((1, HALO, wide), lambda b, i: (b, jnp.minimum((i + 1) * hb, nh - 1), 0)),
                  pl.BlockSpec((None, len(POOL_WINDOWS), POOL_GROUP, POOL_GROUP), lambda b, i: (l, 0, 0, 0)),
                  pl.BlockSpec((None, 1, POOL_WIDTH), lambda b, i: (l, 0, 0)),
                  pl.BlockSpec((None, 3, 3 * HY_WIDTH), lambda b, i: (l, 0, 0)),
                  pl.BlockSpec((None, 1, 3 * HY_WIDTH), lambda b, i: (l, 0, 0))],
        out_specs=[pl.BlockSpec((1, tt, POOL_WIDTH), lambda b, i: (b, i, 0)),
                   pl.BlockSpec((1, tt, HY_WIDTH), lambda b, i: (b, i, 0)),
                   pl.BlockSpec((1, HY_WIDTH, tt), lambda b, i: (b, 0, i))],
        out_shape=[jax.ShapeDtypeStruct((b_, L, POOL_WIDTH), BF16),
                   jax.ShapeDtypeStruct((b_, L, HY_WIDTH), BF16),
                   jax.ShapeDtypeStruct((b_, HY_WIDTH, L), F32)],
        compiler_params=_cparams(("arbitrary", "arbitrary")),
        name="seqlocal",
    )(proj, proj, proj, pool_w, pool_scale, conv_w, conv_b)


def _filter_kernel(z_ref, w1_ref, b1_ref, f1_ref, w2_ref, b2_ref, f2_ref, w3_ref, dec_ref,
                   filt_ref, l1_ref, *, tr, L):
    i = pl.program_id(0)
    half = tr // 2
    z = z_ref[...]
    hid = jnp.sin(f1_ref[...] * (_dot3(z, w1_ref[...]) + b1_ref[...]))
    hid = jnp.sin(f2_ref[...] * (_dot3(hid, w2_ref[...]) + b2_ref[...]))
    decay = jnp.abs(dec_ref[0])
    total = None
    for s in range(2):
        h = _dot3(hid, w3_ref[0, s]) * jnp.exp(-z[:, s * HY_HALF:s * HY_HALF + 1] * decay)
        row = i * tr + s * half + lax.broadcasted_iota(jnp.int32, h.shape, 0)
        h = jnp.where(row == L, 0.0, h)
        filt_ref[:, s * half:(s + 1) * half] = h.T
        part = jnp.sum(jnp.abs(h), axis=0, keepdims=True)
        total = part if total is None else total + part

    @pl.when(i == 0)
    def _():
        l1_ref[...] = jnp.zeros_like(l1_ref)

    l1_ref[...] += total


def _hyena_filter(zpacked, w1, b1, f1, w2, b2, f2, w3, dec, l, L, tr):
    n = 2 * L
    nhalf = L // tr
    full = lambda i: (0, 0)
    mat = pl.BlockSpec((None, LANE, LANE), lambda i: (l, 0, 0))
    vec = pl.BlockSpec((None, 1, LANE), lambda i: (l, 0, 0))
    return pl.pallas_call(
        functools.partial(_filter_kernel, tr=tr, L=L),
        grid=(n // tr,),
        in_specs=[pl.BlockSpec((tr // 2, LANE), lambda i: (i, 0)),
                  mat, vec, vec, mat, vec, vec,
                  pl.BlockSpec((None, 1, 2, LANE, HY_WIDTH), lambda i: (l, i // nhalf, 0, 0, 0)),
                  pl.BlockSpec((None, 1, 1, HY_WIDTH), lambda i: (l, i // nhalf, 0, 0))],
        out_specs=[pl.BlockSpec((HY_WIDTH, tr), lambda i: (0, i)),
                   pl.BlockSpec((1, HY_WIDTH), full)],
        out_shape=[jax.ShapeDtypeStruct((HY_WIDTH, n), F32),
                   jax.ShapeDtypeStruct((1, HY_WIDTH), F32)],
        compiler_params=_cparams(("arbitrary",)),
        name="hyena_filter",
    )(zpacked, w1, b1, f1, w2, b2, f2, w3, dec)


FFT_N1 = 128
FFT_N2 = 128
CONV_CB = 32


def _lconv_kernel(bias_ref, l1_ref, z_ref, f_ref, f1d_h, f1d_l, f1f_h, f1f_l, tr_ref, ti_ref,
                  f2_h, f2_l, g2_h, g2_l, e1_h, e1_l, y_ref, bs_ref, bf_ref, cm_ref, ys_ref, *, layer):
    blk = pl.program_id(0)
    tw_r = tr_ref[...]
    tw_i = ti_ref[...]
    n2 = FFT_N2
    for c in range(CONV_CB):
        rows = slice(c * FFT_N1, (c + 1) * FFT_N1)
        zc = jnp.concatenate([z_ref[0, c], z_ref[1, c]], axis=0)
        a = _cdot3(f1d_h[...], f1d_l[...], zc)
        ar, ai = a[:FFT_N1], a[FFT_N1:]
        bs_ref[rows, :n2] = ar * tw_r - ai * tw_i
        bs_ref[rows, n2:] = ar * tw_i + ai * tw_r
        inv_l1 = 1.0 / jnp.full((1, n2), l1_ref[0, blk * CONV_CB + c], F32)
        a = _cdot3(f1f_h[...], f1f_l[...], f_ref[c] * inv_l1)
        ar, ai = a[:FFT_N1], a[FFT_N1:]
        bf_ref[rows, :n2] = ar * tw_r - ai * tw_i
        bf_ref[rows, n2:] = ar * tw_i + ai * tw_r
    x = _dot3c(bs_ref[...], f2_h[...], f2_l[...])
    hf = _dot3c(bf_ref[...], f2_h[...], f2_l[...])
    xr, xi = x[:, :n2], x[:, n2:]
    hr, hi = hf[:, :n2], hf[:, n2:]
    y = jnp.concatenate([xr * hr - xi * hi, xr * hi + xi * hr], axis=1)
    cm_ref[...] = _dot3c(y, g2_h[...], g2_l[...])
    half = FFT_N1 // 2
    for c in range(CONV_CB):
        rows = slice(c * FFT_N1, (c + 1) * FFT_N1)
        cr = cm_ref[rows, :n2]
        ci = cm_ref[rows, n2:]
        dc = jnp.concatenate([cr * tw_r + ci * tw_i, ci * tw_r - cr * tw_i], axis=0)
        yc = _cdot3(e1_h[...], e1_l[...], dc)
        bias = bias_ref[layer, blk * CONV_CB + c]
        ys_ref[0, c] = yc[:half] + z_ref[0, c] * bias
        ys_ref[1, c] = yc[half:] + z_ref[1, c] * bias
    sub = 8
    for b in range(2):
        for cg in range(CONV_CB // sub):
            for ng in range(half // sub):
                blk8 = ys_ref[b, cg * sub:(cg + 1) * sub, ng * sub:(ng + 1) * sub, :]
                t8 = jnp.swapaxes(blk8, 0, 1)
                for j in range(sub):
                    n1 = ng * sub + j
                    y_ref[b, cg * sub:(cg + 1) * sub, n1 * FFT_N2:(n1 + 1) * FFT_N2] = t8[j]


def _lconv_consts():
    n1, n2 = FFT_N1, FFT_N2
    n = n1 * n2
    idx = np.arange(128)
    ang1 = 2.0 * np.pi * ((idx[:, None] * idx[None, :]) % 128) / 128.0
    fr, fi = np.cos(ang1), -np.sin(ang1)
    angt = 2.0 * np.pi * (idx[:, None] * idx[None, :]) / n
    tw_r, tw_i = np.cos(angt), -np.sin(angt)
    h = n1 // 2
    f1d = np.block([[fr[:, :h], -fi[:, :h]], [fi[:, :h], fr[:, :h]]])
    f1f = np.concatenate([fr, fi], axis=0)
    f2 = np.block([[fr, fi], [-fi, fr]])
    g2 = np.block([[fr, -fi], [fi, fr]])
    er, ei = fr[:h] / n, -fi[:h] / n
    e1 = np.block([[er, -ei], [ei, er]])

    return (*_hl(f1d), *_hl(f1f), np.asarray(tw_r, np.float32), np.asarray(tw_i, np.float32),
            *_hl(f2), *_hl(g2), *_hl(e1))


def _long_conv_lat(zvt, filt_t, l1, bias, l):
    b_, C, L = zvt.shape
    assert b_ == 2 and L == FFT_N1 * FFT_N2 // 2
    z4 = zvt.reshape(b_, C, FFT_N1 // 2, FFT_N2)
    f3 = filt_t.reshape(C, FFT_N1, FFT_N2)
    consts = _lconv_consts()
    cb = CONV_CB

    def cspec(a):
        return pl.BlockSpec(a.shape, lambda i: (0,) * a.ndim)

    return pl.pallas_call(
        functools.partial(_lconv_kernel, layer=l),
        grid=(C // cb,),
        in_specs=[pl.BlockSpec(memory_space=pltpu.SMEM), pl.BlockSpec(memory_space=pltpu.SMEM),
                  pl.BlockSpec((b_, cb, FFT_N1 // 2, FFT_N2), lambda i: (0, i, 0, 0)),
                  pl.BlockSpec((cb, FFT_N1, FFT_N2), lambda i: (i, 0, 0))] + [cspec(a) for a in consts],
        out_specs=pl.BlockSpec((b_, cb, L), lambda i: (0, i, 0)),
        out_shape=jax.ShapeDtypeStruct(zvt.shape, F32),
        scratch_shapes=[pltpu.VMEM((cb * FFT_N1, 2 * FFT_N2), F32),
                        pltpu.VMEM((cb * FFT_N1, 2 * FFT_N2), F32),
                        pltpu.VMEM((cb * FFT_N1, 2 * FFT_N2), F32),
                        pltpu.VMEM((b_, cb, FFT_N1 // 2, FFT_N2), F32)],
        compiler_params=_cparams(("arbitrary",)),
        name="long_conv",
    )(bias, l1, z4, f3, *consts)


def _cconv_kernel(z_ref, f_ref, l1_ref, bias_ref, fd_h, fd_l, ff_h, ff_l, fi_h, fi_l, y_ref, *, L):
    zr, zi = z_ref[0], z_ref[1]
    x = _dot3c(jnp.concatenate([zr, zi], axis=1), fd_h[...], fd_l[...])
    hf = _dot3c(f_ref[...] * (1.0 / l1_ref[...]), ff_h[...], ff_l[...])
    n = 2 * L
    xr, xi = x[:, :n], x[:, n:]
    hr, hi = hf[:, :n], hf[:, n:]
    y = jnp.concatenate([xr * hr - xi * hi, xr * hi + xi * hr], axis=1)
    out = _dot3c(y, fi_h[...], fi_l[...])
    bias = bias_ref[...]
    y_ref[0] = out[:, :L] + zr * bias
    y_ref[1] = out[:, L:] + zi * bias


def _cconv_consts(L):
    n = 2 * L
    idx = np.arange(n)
    ang = 2.0 * np.pi * ((idx[:, None] * idx[None, :]) % n) / n
    fr, fi = np.cos(ang), -np.sin(ang)
    fd = np.block([[fr[:L], fi[:L]], [-fi[:L], fr[:L]]])
    ff = np.concatenate([fr, fi], axis=1)
    er, ei = fr[:, :L] / n, -fi[:, :L] / n
    finv = np.block([[er, ei], [-ei, er]])

    return (*_hl(fd), *_hl(ff), *_hl(finv))


def _long_conv_ctx(zvt, filt_t, l1, bias):
    b_, C, L = zvt.shape
    consts = _cconv_consts(L)
    bias_b = jnp.broadcast_to(bias[:, None], (C, L))
    l1_col = l1.reshape(C, 1)

    def cspec(a):
        return pl.BlockSpec(a.shape, lambda i: (0,) * a.ndim)

    return pl.pallas_call(
        functools.partial(_cconv_kernel, L=L),
        grid=(1,),
        in_specs=[cspec(zvt), cspec(filt_t), cspec(l1_col), cspec(bias_b)] + [cspec(a) for a in consts],
        out_specs=cspec(zvt),
        out_shape=jax.ShapeDtypeStruct(zvt.shape, F32),
        compiler_params=_cparams(("arbitrary",)),
        name="long_conv_ctx",
    )(zvt, filt_t, l1_col, bias_b, *consts)


def _attn_kernel(q_ref, *refs, chunks):
    o_ref = refs[-1]
    tq = q_ref.shape[2]
    q = [q_ref[0, hh] for hh in range(2)]
    m = [jnp.full((tq, 1), -1e30, F32) for _ in range(2)]
    acc = [jnp.zeros((tq, LANE), F32) for _ in range(2)]
    work = [(src, start, size, hh) for (src, start, size) in chunks for hh in range(2)]

    def scores(n):
        src, start, size, hh = work[n]
        k = refs[2 * src][0, hh, start:start + size, :]
        return lax.dot_general(q[hh], k, (((1,), (1,)), ((), ())), preferred_element_type=F32)

    s_next = scores(0)
    for n, (src, start, size, hh) in enumerate(work):
        s = s_next
        if n + 1 < len(work):
            s_next = scores(n + 1)
        v = refs[2 * src + 1][0, hh, start:start + size, :]
        m_new = jnp.maximum(m[hh], jnp.max(s, axis=-1, keepdims=True))
        p = jnp.exp2(s - m_new)
        alpha = jnp.exp2(m[hh] - m_new)
        acc[hh] = alpha * acc[hh] + _mm(p.astype(BF16), v)
        m[hh] = m_new
    lane = lax.broadcasted_iota(jnp.int32, (tq, LANE), 1)
    o0 = acc[0] / acc[0][:, V_HEAD:V_HEAD + 1]
    o1 = acc[1] / acc[1][:, 0:1]
    o_ref[0] = jnp.where(lane < V_HEAD, o0, o1).astype(o_ref.dtype)


def _attention(q, kv_sources, tq, tk):
    b_, H, L, _ = q.shape
    chunks = []
    in_specs = [pl.BlockSpec((1, 2, tq, HEAD_PAD), lambda b, hp, i: (b, hp, i, 0))]
    args = [q]
    for src, (k, v) in enumerate(kv_sources):
        S = k.shape[2]
        step = min(tk, S)
        chunks += [(src, start, step) for start in range(0, S, step)]
        in_specs += [pl.BlockSpec((1, 2, S, HEAD_PAD), lambda b, hp, i: (b, hp, 0, 0))] * 2
        args += [k, v]
    return pl.pallas_call(
        functools.partial(_attn_kernel, chunks=tuple(chunks)),
        grid=(b_, H // 2, L // tq),
        in_specs=in_specs,
        out_specs=pl.BlockSpec((1, tq, LANE), lambda b, hp, i: (b, i, hp)),
        out_shape=jax.ShapeDtypeStruct((b_, L, H * V_HEAD), BF16),
        compiler_params=_cparams(("arbitrary", "arbitrary", "arbitrary")),
        name="attention",
    )(*args)


def _mixffn_kernel(a_ref, x0_ref, yt_ref, o_ref, ga_ref, gb_ref, gc_ref, h_ref, g1_ref, sh2_ref, sc2_ref, g2_ref,
                   gn_ref, wp_ref, wh_ref, wo_ref, wout_ref, w1_ref, w2_ref, fg_ref, out_ref, *, final, nsplit):
    hy = (x0_ref[0].astype(F32) * yt_ref[0].T).astype(BF16)
    a = _mm(a_ref[0], wp_ref[...])
    b = _mm(hy, wh_ref[...])
    cm = _mm(o_ref[0], wo_ref[...])
    m = (jax.nn.sigmoid(ga_ref[0].astype(F32)) * a + jax.nn.sigmoid(gb_ref[0].astype(F32)) * b
         + jax.nn.sigmoid(gc_ref[0].astype(F32)) * cm)
    x = h_ref[0] + g1_ref[0] * _mm(m.astype(BF16), wout_ref[...])
    xn = _mod_norm(x, gn_ref[...], sc2_ref[0], sh2_ref[0]).astype(BF16)
    tf = D_FF // nsplit
    acc = None
    for k in range(nsplit):
        u = jnp.maximum(_mm(xn, w1_ref[:, k * tf:(k + 1) * tf]), 0.0)
        part = _mm((u * u).astype(BF16), w2_ref[k * tf:(k + 1) * tf, :])
        acc = part if acc is None else acc + part
    y = x + g2_ref[0] * acc
    if final:
        y = _rms(y, fg_ref[...])
    out_ref[0] = y


def _mixffn(a_pre, x0c, yt, o, proj, h, mod, mrow, wp, wh, wo, wout, gn, w1, w2, final_g, l, tt, final):
    b_, L, _ = h.shape
    layer = lambda b, i: (l, 0, 0)
    row = lambda b, i: (b, i, 0)
    gcol = COL_GATE // D_MODEL
    resident = dict(pipeline_mode=pl.Buffered(1))
    modrow = lambda k: pl.BlockSpec((1, 1, D_MODEL), lambda b, i: (mrow(b), 0, k))
    return pl.pallas_call(
        functools.partial(_mixffn_kernel, final=final, nsplit=2),
        grid=(b_, L // tt),
        in_specs=[pl.BlockSpec((1, tt, POOL_WIDTH), row),
                  pl.BlockSpec((1, tt, HY_WIDTH), row),
                  pl.BlockSpec((1, HY_WIDTH, tt), lambda b, i: (b, 0, i)),
                  pl.BlockSpec((1, tt, MLA_HEADS * V_HEAD), row),
                  pl.BlockSpec((1, tt, D_MODEL), lambda b, i: (b, i, gcol)),
                  pl.BlockSpec((1, tt, D_MODEL), lambda b, i: (b, i, gcol + 1)),
                  pl.BlockSpec((1, tt, D_MODEL), lambda b, i: (b, i, gcol + 2)),
                  pl.BlockSpec((1, tt, D_MODEL), row),
                  modrow(2), modrow(3), modrow(4), modrow(5),
                  pl.BlockSpec((None, 1, D_MODEL), layer),
                  pl.BlockSpec((None, POOL_WIDTH, D_MODEL), layer, **resident),
                  pl.BlockSpec((None, HY_WIDTH, D_MODEL), layer, **resident),
                  pl.BlockSpec((None, MLA_HEADS * V_HEAD, D_MODEL), layer, **resident),
                  pl.BlockSpec((None, D_MODEL, D_MODEL), layer, **resident),
                  pl.BlockSpec((None, D_MODEL, D_FF), layer, **resident),
                  pl.BlockSpec((None, D_FF, D_MODEL), layer, **resident),
                  pl.BlockSpec((1, D_MODEL), lambda b, i: (0, 0))],
        out_specs=pl.BlockSpec((1, tt, D_MODEL), row),
        out_shape=jax.ShapeDtypeStruct(h.shape, F32),
        compiler_params=pltpu.CompilerParams(dimension_semantics=("arbitrary", "arbitrary"),
                                             vmem_limit_bytes=VMEM_LIMIT_FUSED),
        name="mixffn",
    )(a_pre, x0c, yt, o, proj, proj, proj, h, mod, mod, mod, mod, gn, wp, wh, wo, wout, w1, w2, final_g)


def _rope_tables(n_rows):
    rows = np.repeat(np.arange(n_rows, dtype=np.float64), GRID_W)
    cols = np.tile(np.arange(GRID_W, dtype=np.float64), n_rows)
    half = QK_ROPE // 2
    inv = ROPE_BASE ** (-np.arange(0, half, 2, dtype=np.float64) / half)
    ar = rows[:, None] * inv
    ac = cols[:, None] * inv
    ang = np.concatenate([ar, ar, ac, ac], axis=-1)
    return np.cos(ang), np.sin(ang)


def _head_tables(cos, sin, L):
    ones = np.ones((L, QK_NOPE))
    z64 = np.zeros((L, QK_NOPE))
    z32 = np.zeros((L, HEAD_PAD - QK_DIM))
    if cos is None:
        cos = np.ones((L, QK_ROPE))
        sin = np.zeros((L, QK_ROPE))
    cq = np.concatenate([ones, cos, z32], axis=1) * (MLA_SCALE * LOG2E)
    sq = np.concatenate([z64, sin, z32], axis=1) * (MLA_SCALE * LOG2E)
    ck = np.concatenate([z64, cos, z32], axis=1)
    sk = np.concatenate([z64, sin, z32], axis=1)
    return tuple(np.asarray(t, np.float32) for t in (cq, sq, ck, sk))


def _hy_embed(L, tr):
    t = np.linspace(0.0, 1.0, L)[:, None]
    omega = 2.0 * np.pi * np.arange(L)[:, None] / L
    bands = np.linspace(1e-4, HY_BANDS - 1, HY_BANDS)[None, :]
    z = np.concatenate([t, np.cos(omega * bands), -np.sin(omega * bands)], axis=-1)
    zfull = np.concatenate([z, np.zeros((1, HY_EMB)), z[:0:-1]], axis=0)
    zfull = np.pad(zfull, ((0, 0), (0, HY_HALF - HY_EMB)))
    packed = zfull.reshape(2 * L // tr, 2, tr // 2, HY_HALF).transpose(0, 2, 1, 3).reshape(L, 2 * HY_HALF)
    return np.asarray(packed, np.float32)


def _blockdiag2(a):
    a = _pad2(a, HY_HALF, HY_HALF)
    z = jnp.zeros_like(a)
    return jnp.concatenate([jnp.concatenate([a, z], axis=-1), jnp.concatenate([z, a], axis=-1)], axis=-2)


def _pack_w_in(w):
    z = lambda n: jnp.zeros(w.shape[:-1] + (n,), BF16)
    o_q = 512 + 3 * HY_WIDTH
    o_kv = o_q + Q_LORA
    o_pe = o_kv + KV_LORA
    o_gate = o_pe + QK_ROPE
    kpe = w[..., o_pe:o_gate].astype(BF16)
    mid = jnp.concatenate([
        w[..., o_q:o_pe].astype(BF16),
        z(QK_NOPE), kpe, z(HEAD_PAD - QK_DIM),
        z(QK_NOPE), _rot_half(kpe), z(HEAD_PAD - QK_DIM),
        z(LANE)], axis=-1)
    return w[..., :o_q].astype(BF16), mid, w[..., o_gate:].astype(BF16)


def _pack_heads(w_uq, w_ukv):
    lead = w_uq.shape[:-2]
    lanes = lambda a, lo: jnp.pad(a, [(0, 0)] * (a.ndim - 1) + [(lo, HEAD_PAD - lo - a.shape[-1])])
    wq3 = w_uq.astype(BF16).reshape(*lead, Q_LORA, MLA_HEADS, QK_DIM)
    wq = lanes(wq3, 0).reshape(*lead, Q_LORA, -1)
    wqr = lanes(_rot_half(wq3[..., QK_NOPE:]), QK_NOPE).reshape(*lead, Q_LORA, -1)
    wkv3 = w_ukv.astype(BF16).reshape(*lead, KV_LORA, MLA_HEADS, QK_NOPE + V_HEAD)
    wk = lanes(wkv3[..., :QK_NOPE], 0).reshape(*lead, KV_LORA, -1)
    v5 = wkv3[..., QK_NOPE:].reshape(*lead, KV_LORA, MLA_HEADS // 2, 2, 1, V_HEAD)
    wv = (v5 * jnp.eye(2, dtype=BF16)[:, :, None]).reshape(*lead, KV_LORA, -1)
    return wq, wqr, wk, wv


def _pad2(a, r, c):
    pads = [(0, 0)] * (a.ndim - 2) + [(0, r - a.shape[-2]), (0, c - a.shape[-1])]
    return jnp.pad(a, pads)


TT_LAT = 512
TQ = 1024
TK = 512
TR_LAT = 1024


def kernel(x, c, ctx, c_ctx, w_mod, b_mod, norm1_g, norm2_g, w_in, pool_w, pool_scale, pool_out, hy_conv_w, hy_conv_b, hy_f_w1, hy_f_b1, hy_f_freq1, hy_f_w2, hy_f_b2, hy_f_freq2, hy_f_w3, hy_decay, hy_bias, hy_out, q_norm_g, w_uq, kv_norm_g, w_ukv, w_o, w_out, w_ff1, w_ff2, final_g):
    B, L, _ = x.shape
    Lc = ctx.shape[1]
    cos, sin = _rope_tables(L // GRID_W)
    tabs_lat = _head_tables(cos, sin, L)
    tabs_ctx = _head_tables(None, None, Lc)
    z_lat = _hy_embed(L, TR_LAT)
    z_ctx = _hy_embed(Lc, Lc)

    c_all = jnp.concatenate([c, c_ctx[None], jnp.zeros((MOD_ROWS - B - 1, D_MODEL), F32)], axis=0)
    mod = _modulation(c_all, w_mod, b_mod).reshape(DEPTH * MOD_ROWS, 1, N_MOD * D_MODEL)

    row3 = lambda a: a[:, None, :]
    g1n, g2n, gq, gkv = row3(norm1_g), row3(norm2_g), row3(q_norm_g), row3(kv_norm_g)
    w_in_p = _pack_w_in(w_in)
    wq, wqr, wk, wv = _pack_heads(w_uq, w_ukv)
    pw, ps = pool_w.astype(BF16), row3(pool_scale)
    cw, cb = hy_conv_w, row3(hy_conv_b)
    wp, wh, wo, wout = pool_out.astype(BF16), hy_out.astype(BF16), w_o.astype(BF16), w_out.astype(BF16)
    wf1, wf2 = w_ff1.astype(BF16), w_ff2.astype(BF16)
    twice = lambda v: jnp.tile(_pad2(row3(v), 1, HY_HALF), (1, 1, 2))
    fw1, fw2 = _blockdiag2(hy_f_w1), _blockdiag2(hy_f_w2)
    fb1, ff1, fb2, ff2 = twice(hy_f_b1), twice(hy_f_freq1), twice(hy_f_b2), twice(hy_f_freq2)
    w3d = hy_f_w3.reshape(DEPTH, HY_HALF, 2, HY_WIDTH).transpose(0, 2, 1, 3)
    z3 = jnp.zeros_like(w3d)
    fw3 = jnp.stack([jnp.concatenate([w3d, z3], axis=2), jnp.concatenate([z3, w3d], axis=2)], axis=2)
    dec = hy_decay[:, :, None, :]
    fg = final_g[None]

    h, hc = x, ctx
    for l in range(DEPTH):
        last = l == DEPTH - 1
        row_lat = lambda b, l=l: l * MOD_ROWS + b
        row_ctx = lambda b, l=l: l * MOD_ROWS + B

        def stream_pre(hs, mrow, tabs, tt):
            return _inproj(hs, mod, mrow, g1n, w_in_p, gq, gkv, wq, wqr, wk, wv, *tabs, l, tt)

        def stream_mix(hs, mrow, proj, q, kv_sources, zemb, tt, tq, tk, tr, final):
            Ls = hs.shape[1]
            a_pre, x0c, zvt = _seqlocal(proj, pw, ps, cw, cb, l, tt)
            filt_t, l1 = _hyena_filter(zemb, fw1, fb1, ff1, fw2, fb2, ff2, fw3, dec, l, Ls, tr)
            if Ls == FFT_N1 * FFT_N2 // 2:
                yt = _long_conv_lat(zvt, filt_t, l1, hy_bias, l)
            else:
                yt = _long_conv_ctx(zvt, filt_t, l1, hy_bias[l])
            o = _attention(q, kv_sources, tq, tk)
            return _mixffn(a_pre, x0c, yt, o, proj, hs, mod, mrow, wp, wh, wo, wout, g2n, wf1, wf2, fg, l, tt, final)

        proj_c, q_c, k_c, v_c = stream_pre(hc, row_ctx, tabs_ctx, Lc)
        proj_l, q_l, k_l, v_l = stream_pre(h, row_lat, tabs_lat, TT_LAT)
        h = stream_mix(h, row_lat, proj_l, q_l, [(k_c, v_c), (k_l, v_l)], z_lat, TT_LAT, TQ, TK, TR_LAT, last)
        if not last:
            hc = stream_mix(hc, row_ctx, proj_c, q_c, [(k_c, v_c)], z_ctx, Lc, Lc, Lc, Lc, False)
    return h
```

```python
import functools
import math

import jax
import jax.numpy as jnp
import numpy as np
from jax import lax
from jax.experimental import pallas as pl
from jax.experimental.pallas import tpu as pltpu

F32 = jnp.float32
BF16 = jnp.bfloat16

D_MODEL = 1024
DEPTH = 4
GRID_W = 64
EPS = 1e-6

POOL_WIDTH = 512
POOL_WINDOWS = (2, 4, 8, 16)
POOL_GROUP = 128
HY_WIDTH = 512
HY_EMB = 33
HY_BANDS = 16
HY_HALF = 64
MLA_HEADS = 8
QK_NOPE = 64
QK_ROPE = 32
QK_DIM = 96
V_HEAD = 64
Q_LORA = 384
KV_LORA = 256
MLA_SCALE = QK_DIM ** -0.5
LOG2E = math.log2(math.e)
ROPE_BASE = 10000.0
D_FF = 4 * D_MODEL
N_MOD = 6
MOD_ROWS = 8

LANE = 128
HALO = 16
HEAD_PAD = 128

COL_POOL = 0
COL_HY = 512
COL_GATE = 2048
N_PROJ = 5120

VMEM_LIMIT = 50 * 1024 * 1024
VMEM_LIMIT_FUSED = 58 * 1024 * 1024


def _cparams(sem):
    return pltpu.CompilerParams(dimension_semantics=sem, vmem_limit_bytes=VMEM_LIMIT)


def _mm(a, b):
    return jnp.dot(a, b, preferred_element_type=F32)


def _split(x):
    hi = x.astype(BF16)
    lo = (x - hi.astype(F32)).astype(BF16)
    return hi, lo


def _dot3(a, b):
    ah, al = _split(a)
    bh, bl = _split(b)
    return _mm(ah, bh) + _mm(al, bh) + _mm(ah, bl)


def _dot3c(a, bh, bl):
    ah, al = _split(a)
    return _mm(ah, bh) + _mm(al, bh) + _mm(ah, bl)


def _cdot3(ch, cl, b):
    bh, bl = _split(b)
    n = b.shape[1]
    r = _mm(ch, jnp.concatenate([bh, bl], axis=1))
    return r[:, :n] + r[:, n:] + _mm(cl, bh)


def _hl(m):
    m32 = np.asarray(m, np.float32)
    hi = m32.astype(BF16)
    lo = (m32 - hi.astype(np.float32)).astype(BF16)
    return hi, lo


def _rot_half(x):
    q = QK_ROPE // 4
    return jnp.concatenate([-x[..., q:2 * q], x[..., :q], -x[..., 3 * q:], x[..., 2 * q:3 * q]], axis=-1)


def _mod_kernel(c_ref, w_ref, b_ref, o_ref):
    c = c_ref[...]
    s = c * jax.nn.sigmoid(c)
    o_ref[0] = _dot3(s, w_ref[0]) + b_ref[0]


def _modulation(c_all, w_mod, b_mod):
    tn = 1536
    n = N_MOD * D_MODEL
    return pl.pallas_call(
        _mod_kernel,
        grid=(DEPTH, n // tn),
        in_specs=[pl.BlockSpec((MOD_ROWS, D_MODEL), lambda l, j: (0, 0)),
                  pl.BlockSpec((1, D_MODEL, tn), lambda l, j: (l, 0, j)),
                  pl.BlockSpec((1, 1, tn), lambda l, j: (l, 0, j))],
        out_specs=pl.BlockSpec((1, MOD_ROWS, tn), lambda l, j: (l, 0, j)),
        out_shape=jax.ShapeDtypeStruct((DEPTH, MOD_ROWS, n), F32),
        compiler_params=_cparams(("arbitrary", "arbitrary")),
        name="modulation",
    )(c_all, w_mod, b_mod.reshape(DEPTH, 1, n))


def _mod_norm(x, g, sc, sh):
    ms = jnp.mean(x * x, axis=-1, keepdims=True)
    return (x * lax.rsqrt(ms + EPS) * g) * (1.0 + sc) + sh


def _rms(x, g):
    ms = jnp.mean(x * x, axis=-1, keepdims=True)
    return x * lax.rsqrt(ms + EPS) * g


def _inproj_kernel(x_ref, sh_ref, sc_ref, g_ref, wa_ref, wm_ref, wg_ref, gq_ref, gkv_ref,
                   wq_ref, wqr_ref, wk_ref, wv_ref, cq_ref, sq_ref, ck_ref, sk_ref,
                   proj_ref, q_ref, k_ref, v_ref):
    xn = _mod_norm(x_ref[0], g_ref[...], sc_ref[0], sh_ref[0]).astype(BF16)
    proj_ref[0, :, :COL_GATE] = _mm(xn, wa_ref[...]).astype(proj_ref.dtype)
    proj_ref[0, :, COL_GATE:] = _mm(xn, wg_ref[...]).astype(proj_ref.dtype)
    mid = _mm(xn, wm_ref[...])
    cq = mid[:, :Q_LORA]
    ckv = mid[:, Q_LORA:Q_LORA + KV_LORA]
    o = Q_LORA + KV_LORA
    kpe = mid[:, o:o + LANE]
    kpe_rot = mid[:, o + LANE:o + 2 * LANE]
    qn = _rms(cq, gq_ref[...]).astype(BF16)
    q = _mm(qn, wq_ref[...])
    qr = _mm(qn, wqr_ref[...])
    kvn = _rms(ckv, gkv_ref[...]).astype(BF16)
    k = _mm(kvn, wk_ref[...])
    v = _mm(kvn, wv_ref[...])
    cq_t, sq_t = cq_ref[...], sq_ref[...]
    kpe_r = kpe * ck_ref[...] + kpe_rot * sk_ref[...]
    lane = lax.broadcasted_iota(jnp.int32, (1, HEAD_PAD), 1)
    for h in range(MLA_HEADS):
        sl = slice(h * HEAD_PAD, (h + 1) * HEAD_PAD)
        q_ref[0, h] = (q[:, sl] * cq_t + qr[:, sl] * sq_t).astype(q_ref.dtype)
        k_ref[0, h] = (k[:, sl] + kpe_r).astype(k_ref.dtype)
        pair = v[:, (h // 2) * HEAD_PAD:(h // 2 + 1) * HEAD_PAD]
        if h % 2 == 0:
            vh = jnp.where(lane < V_HEAD, pair, (lane == V_HEAD).astype(F32))
        else:
            vh = jnp.where(lane >= V_HEAD, pair, (lane == 0).astype(F32))
        v_ref[0, h] = vh.astype(v_ref.dtype)


def _inproj(h, mod, mrow, g, ws, gq, gkv, wq, wqr, wk, wv, cq, sq, ck, sk, l, tt):
    b_, L, _ = h.shape
    layer = lambda b, i: (l, 0, 0)
    resident = dict(pipeline_mode=pl.Buffered(1))
    hw = MLA_HEADS * HEAD_PAD
    table = pl.BlockSpec((tt, LANE), lambda b, i: (i, 0))
    heads = pl.BlockSpec((1, MLA_HEADS, tt, HEAD_PAD), lambda b, i: (b, 0, i, 0))
    return pl.pallas_call(
        _inproj_kernel,
        grid=(b_, L // tt),
        in_specs=[pl.BlockSpec((1, tt, D_MODEL), lambda b, i: (b, i, 0)),
                  pl.BlockSpec((1, 1, D_MODEL), lambda b, i: (mrow(b), 0, 0)),
                  pl.BlockSpec((1, 1, D_MODEL), lambda b, i: (mrow(b), 0, 1)),
                  pl.BlockSpec((None, 1, D_MODEL), layer)]
        + [pl.BlockSpec((None, D_MODEL, w.shape[2]), layer, **resident) for w in ws]
        + [pl.BlockSpec((None, 1, Q_LORA), layer), pl.BlockSpec((None, 1, KV_LORA), layer),
           pl.BlockSpec((None, Q_LORA, hw), layer, **resident), pl.BlockSpec((None, Q_LORA, hw), layer, **resident),
           pl.BlockSpec((None, KV_LORA, hw), layer, **resident),
           pl.BlockSpec((None, KV_LORA, MLA_HEADS * V_HEAD), layer, **resident),
           table, table, table, table],
        out_specs=[pl.BlockSpec((1, tt, N_PROJ), lambda b, i: (b, i, 0)), heads, heads, heads],
        out_shape=[jax.ShapeDtypeStruct((b_, L, N_PROJ), BF16)]
        + [jax.ShapeDtypeStruct((b_, MLA_HEADS, L, HEAD_PAD), BF16)] * 3,
        compiler_params=_cparams(("arbitrary", "arbitrary")),
        name="inproj",
    )(h, mod, mod, g, *ws, gq, gkv, wq, wqr, wk, wv, cq, sq, ck, sk)


def _seqlocal_kernel(cur_ref, prev_ref, next_ref, bc_ref, bp_ref, bn_ref, pw_ref, ps_ref, cw_ref, cb_ref,
                     a_ref, x0_ref, zvt_ref, *, tt, L):
    i = pl.program_id(1)
    nt = pl.num_programs(1)
    keep_prev = (i > 0).astype(F32)
    keep_next = (i < nt - 1).astype(F32)
    n = tt + 2 * HALO

    def ext(lo, hi):
        return jnp.concatenate([prev_ref[0, :, lo:hi].astype(F32) * keep_prev,
                                cur_ref[0, :, lo:hi].astype(F32),
                                next_ref[0, :, lo:hi].astype(F32) * keep_next], axis=0)

    def roll(x, s):
        return pltpu.roll(x, s % n, 0)

    t = i * tt + lax.broadcasted_iota(jnp.int32, (tt, POOL_GROUP), 0)
    for gi, w in enumerate(POOL_WINDOWS):
        lo = COL_POOL + gi * POOL_GROUP
        hi = lo + POOL_GROUP
        cur = cur_ref[0, :, lo:hi]
        s = (_mm(bc_ref[gi], cur)
             + _mm(bp_ref[gi], (prev_ref[0, :, lo:hi].astype(F32) * keep_prev).astype(BF16))
             + _mm(bn_ref[gi], (next_ref[0, :, lo:hi].astype(F32) * keep_next).astype(BF16)))
        cnt = (jnp.minimum(t + w // 2, L) - jnp.maximum(t - w // 2, 0)).astype(F32)
        u = s / cnt - cur.astype(F32)
        ug = _mm(u.astype(BF16), pw_ref[gi]) * ps_ref[:, gi * POOL_GROUP:(gi + 1) * POOL_GROUP]
        a_ref[0, :, gi * POOL_GROUP:(gi + 1) * POOL_GROUP] = ug.astype(a_ref.dtype)

    def conv(part, j):
        c0 = part * HY_WIDTH + j * LANE
        xe = ext(COL_HY + c0, COL_HY + c0 + LANE)
        y = (roll(xe, 1) * cw_ref[0:1, c0:c0 + LANE] + xe * cw_ref[1:2, c0:c0 + LANE]
             + roll(xe, -1) * cw_ref[2:3, c0:c0 + LANE] + cb_ref[:, c0:c0 + LANE])
        return y[HALO:HALO + tt]

    for j in range(HY_WIDTH // LANE):
        x0_ref[0, :, j * LANE:(j + 1) * LANE] = conv(0, j).astype(x0_ref.dtype)
        zv = conv(2, j) * conv(1, j)
        zvt_ref[0, j * LANE:(j + 1) * LANE, :] = zv.T


def _pool_bands(tt):
    t = np.arange(tt)[:, None]
    pos = np.arange(-HALO, tt + HALO)[None, :]
    bands = np.stack([(pos - t >= -(w // 2)) & (pos - t < w // 2) for w in POOL_WINDOWS]).astype(np.float32)
    return tuple(b.astype(BF16) for b in (bands[:, :, HALO:HALO + tt], bands[:, :, :HALO], bands[:, :, HALO + tt:]))


def _seqlocal(proj, pool_w, pool_scale, conv_w, conv_b, l, tt):
    b_, L, _ = proj.shape
    hb = tt // HALO
    nh = L // HALO
    wide = COL_GATE
    bands = _pool_bands(tt)
    return pl.pallas_call(
        functools.partial(_seqlocal_kernel, tt=tt, L=L),
        grid=(b_, L // tt),
        in_specs=[pl.BlockSpec((1, tt, wide), lambda b, i: (b, i, 0)),
                  pl.BlockSpec((1, HALO, wide), lambda b, i: (b, jnp.maximum(i * hb - 1, 0), 0)),
                  pl.BlockSpec((1, HALO, wide), lambda b, i: (b, jnp.minimum((i + 1) * hb, nh - 1), 0))]
        + [pl.BlockSpec(a.shape, lambda b, i: (0, 0, 0)) for a in bands] + [
                  pl.BlockSpec((None, len(POOL_WINDOWS), POOL_GROUP, POOL_GROUP), lambda b, i: (l, 0, 0, 0)),
                  pl.BlockSpec((None, 1, POOL_WIDTH), lambda b, i: (l, 0, 0)),
                  pl.BlockSpec((None, 3, 3 * HY_WIDTH), lambda b, i: (l, 0, 0)),
                  pl.BlockSpec((None, 1, 3 * HY_WIDTH), lambda b, i: (l, 0, 0))],
        out_specs=[pl.BlockSpec((1, tt, POOL_WIDTH), lambda b, i: (b, i, 0)),
                   pl.BlockSpec((1, tt, HY_WIDTH), lambda b, i: (b, i, 0)),
                   pl.BlockSpec((1, HY_WIDTH, tt), lambda b, i: (b, 0, i))],
        out_shape=[jax.ShapeDtypeStruct((b_, L, POOL_WIDTH), BF16),
                   jax.ShapeDtypeStruct((b_, L, HY_WIDTH), BF16),
                   jax.ShapeDtypeStruct((b_, HY_WIDTH, L), F32)],
        compiler_params=_cparams(("arbitrary", "arbitrary")),
        name="seqlocal",
    )(proj, proj, proj, *bands, pool_w, pool_scale, conv_w, conv_b)


def _filter_kernel(z_ref, w1_ref, b1_ref, f1_ref, w2_ref, b2_ref, f2_ref, w3_ref, dec_ref,
                   filt_ref, l1_ref, *, tr, L):
    i = pl.program_id(0)
    half = tr // 2
    z = z_ref[...]
    hid = jnp.sin(f1_ref[...] * (_dot3(z, w1_ref[...]) + b1_ref[...]))
    hid = jnp.sin(f2_ref[...] * (_dot3(hid, w2_ref[...]) + b2_ref[...]))
    decay = jnp.abs(dec_ref[0])
    total = None
    for s in range(2):
        h = _dot3(hid, w3_ref[0, s]) * jnp.exp(-z[:, s * HY_HALF:s * HY_HALF + 1] * decay)
        row = i * tr + s * half + lax.broadcasted_iota(jnp.int32, h.shape, 0)
        h = jnp.where(row == L, 0.0, h)
        filt_ref[:, s * half:(s + 1) * half] = h.T
        part = jnp.sum(jnp.abs(h), axis=0, keepdims=True)
        total = part if total is None else total + part

    @pl.when(i == 0)
    def _():
        l1_ref[...] = jnp.zeros_like(l1_ref)

    l1_ref[...] += total


def _hyena_filter(zpacked, w1, b1, f1, w2, b2, f2, w3, dec, l, L, tr):
    n = 2 * L
    nhalf = L // tr
    full = lambda i: (0, 0)
    mat = pl.BlockSpec((None, LANE, LANE), lambda i: (l, 0, 0))
    vec = pl.BlockSpec((None, 1, LANE), lambda i: (l, 0, 0))
    return pl.pallas_call(
        functools.partial(_filter_kernel, tr=tr, L=L),
        grid=(n // tr,),
        in_specs=[pl.BlockSpec((tr // 2, LANE), lambda i: (i, 0)),
                  mat, vec, vec, mat, vec, vec,
                  pl.BlockSpec((None, 1, 2, LANE, HY_WIDTH), lambda i: (l, i // nhalf, 0, 0, 0)),
                  pl.BlockSpec((None, 1, 1, HY_WIDTH), lambda i: (l, i // nhalf, 0, 0))],
        out_specs=[pl.BlockSpec((HY_WIDTH, tr), lambda i: (0, i)),
                   pl.BlockSpec((1, HY_WIDTH), full)],
        out_shape=[jax.ShapeDtypeStruct((HY_WIDTH, n), F32),
                   jax.ShapeDtypeStruct((1, HY_WIDTH), F32)],
        compiler_params=_cparams(("arbitrary",)),
        name="hyena_filter",
    )(zpacked, w1, b1, f1, w2, b2, f2, w3, dec)


FFT_N1 = 128
FFT_N2 = 128
CONV_CB = 32


def _lconv_kernel(bias_ref, l1_ref, z_ref, f_ref, f1d_h, f1d_l, f1f_h, f1f_l, tr_ref, ti_ref,
                  f2_h, f2_l, g2_h, g2_l, e1_h, e1_l, y_ref, bs_ref, bf_ref, cm_ref, ys_ref, *, layer):
    blk = pl.program_id(0)
    tw_r = tr_ref[...]
    tw_i = ti_ref[...]
    n2 = FFT_N2
    for c in range(CONV_CB):
        rows = slice(c * FFT_N1, (c + 1) * FFT_N1)
        zc = jnp.concatenate([z_ref[0, c], z_ref[1, c]], axis=0)
        a = _cdot3(f1d_h[...], f1d_l[...], zc)
        ar, ai = a[:FFT_N1], a[FFT_N1:]
        bs_ref[rows, :n2] = ar * tw_r - ai * tw_i
        bs_ref[rows, n2:] = ar * tw_i + ai * tw_r
        inv_l1 = 1.0 / jnp.full((1, n2), l1_ref[0, blk * CONV_CB + c], F32)
        a = _cdot3(f1f_h[...], f1f_l[...], f_ref[c] * inv_l1)
        ar, ai = a[:FFT_N1], a[FFT_N1:]
        bf_ref[rows, :n2] = ar * tw_r - ai * tw_i
        bf_ref[rows, n2:] = ar * tw_i + ai * tw_r
    x = _dot3c(bs_ref[...], f2_h[...], f2_l[...])
    hf = _dot3c(bf_ref[...], f2_h[...], f2_l[...])
    xr, xi = x[:, :n2], x[:, n2:]
    hr, hi = hf[:, :n2], hf[:, n2:]
    y = jnp.concatenate([xr * hr - xi * hi, xr * hi + xi * hr], axis=1)
    cm_ref[...] = _dot3c(y, g2_h[...], g2_l[...])
    half = FFT_N1 // 2
    for c in range(CONV_CB):
        rows = slice(c * FFT_N1, (c + 1) * FFT_N1)
        cr = cm_ref[rows, :n2]
        ci = cm_ref[rows, n2:]
        dc = jnp.concatenate([cr * tw_r + ci * tw_i, ci * tw_r - cr * tw_i], axis=0)
        yc = _cdot3(e1_h[...], e1_l[...], dc)
        bias = bias_ref[layer, blk * CONV_CB + c]
        ys_ref[0, c] = yc[:half] + z_ref[0, c] * bias
        ys_ref[1, c] = yc[half:] + z_ref[1, c] * bias
    sub = 8
    for b in range(2):
        for cg in range(CONV_CB // sub):
            for ng in range(half // sub):
                blk8 = ys_ref[b, cg * sub:(cg + 1) * sub, ng * sub:(ng + 1) * sub, :]
                t8 = jnp.swapaxes(blk8, 0, 1)
                for j in range(sub):
                    n1 = ng * sub + j
                    y_ref[b, cg * sub:(cg + 1) * sub, n1 * FFT_N2:(n1 + 1) * FFT_N2] = t8[j]


def _lconv_consts():
    n1, n2 = FFT_N1, FFT_N2
    n = n1 * n2
    idx = np.arange(128)
    ang1 = 2.0 * np.pi * ((idx[:, None] * idx[None, :]) % 128) / 128.0
    fr, fi = np.cos(ang1), -np.sin(ang1)
    angt = 2.0 * np.pi * (idx[:, None] * idx[None, :]) / n
    tw_r, tw_i = np.cos(angt), -np.sin(angt)
    h = n1 // 2
    f1d = np.block([[fr[:, :h], -fi[:, :h]], [fi[:, :h], fr[:, :h]]])
    f1f = np.concatenate([fr, fi], axis=0)
    f2 = np.block([[fr, fi], [-fi, fr]])
    g2 = np.block([[fr, -fi], [fi, fr]])
    er, ei = fr[:h] / n, -fi[:h] / n
    e1 = np.block([[er, -ei], [ei, er]])

    return (*_hl(f1d), *_hl(f1f), np.asarray(tw_r, np.float32), np.asarray(tw_i, np.float32),
            *_hl(f2), *_hl(g2), *_hl(e1))


def _long_conv_lat(zvt, filt_t, l1, bias, l):
    b_, C, L = zvt.shape
    assert b_ == 2 and L == FFT_N1 * FFT_N2 // 2
    z4 = zvt.reshape(b_, C, FFT_N1 // 2, FFT_N2)
    f3 = filt_t.reshape(C, FFT_N1, FFT_N2)
    consts = _lconv_consts()
    cb = CONV_CB

    def cspec(a):
        return pl.BlockSpec(a.shape, lambda i: (0,) * a.ndim)

    return pl.pallas_call(
        functools.partial(_lconv_kernel, layer=l),
        grid=(C // cb,),
        in_specs=[pl.BlockSpec(memory_space=pltpu.SMEM), pl.BlockSpec(memory_space=pltpu.SMEM),
                  pl.BlockSpec((b_, cb, FFT_N1 // 2, FFT_N2), lambda i: (0, i, 0, 0)),
                  pl.BlockSpec((cb, FFT_N1, FFT_N2), lambda i: (i, 0, 0))] + [cspec(a) for a in consts],
        out_specs=pl.BlockSpec((b_, cb, L), lambda i: (0, i, 0)),
        out_shape=jax.ShapeDtypeStruct(zvt.shape, F32),
        scratch_shapes=[pltpu.VMEM((cb * FFT_N1, 2 * FFT_N2), F32),
                        pltpu.VMEM((cb * FFT_N1, 2 * FFT_N2), F32),
                        pltpu.VMEM((cb * FFT_N1, 2 * FFT_N2), F32),
                        pltpu.VMEM((b_, cb, FFT_N1 // 2, FFT_N2), F32)],
        compiler_params=_cparams(("arbitrary",)),
        name="long_conv",
    )(bias, l1, z4, f3, *consts)


def _cconv_kernel(z_ref, f_ref, l1_ref, bias_ref, fd_h, fd_l, ff_h, ff_l, fi_h, fi_l, y_ref, *, L):
    zr, zi = z_ref[0], z_ref[1]
    x = _dot3c(jnp.concatenate([zr, zi], axis=1), fd_h[...], fd_l[...])
    hf = _dot3c(f_ref[...] * (1.0 / l1_ref[...]), ff_h[...], ff_l[...])
    n = 2 * L
    xr, xi = x[:, :n], x[:, n:]
    hr, hi = hf[:, :n], hf[:, n:]
    y = jnp.concatenate([xr * hr - xi * hi, xr * hi + xi * hr], axis=1)
    out = _dot3c(y, fi_h[...], fi_l[...])
    bias = bias_ref[...]
    y_ref[0] = out[:, :L] + zr * bias
    y_ref[1] = out[:, L:] + zi * bias


def _cconv_consts(L):
    n = 2 * L
    idx = np.arange(n)
    ang = 2.0 * np.pi * ((idx[:, None] * idx[None, :]) % n) / n
    fr, fi = np.cos(ang), -np.sin(ang)
    fd = np.block([[fr[:L], fi[:L]], [-fi[:L], fr[:L]]])
    ff = np.concatenate([fr, fi], axis=1)
    er, ei = fr[:, :L] / n, -fi[:, :L] / n
    finv = np.block([[er, ei], [-ei, er]])

    return (*_hl(fd), *_hl(ff), *_hl(finv))


def _long_conv_ctx(zvt, filt_t, l1, bias):
    b_, C, L = zvt.shape
    consts = _cconv_consts(L)
    bias_b = jnp.broadcast_to(bias[:, None], (C, L))
    l1_col = l1.reshape(C, 1)

    def cspec(a):
        return pl.BlockSpec(a.shape, lambda i: (0,) * a.ndim)

    return pl.pallas_call(
        functools.partial(_cconv_kernel, L=L),
        grid=(1,),
        in_specs=[cspec(zvt), cspec(filt_t), cspec(l1_col), cspec(bias_b)] + [cspec(a) for a in consts],
        out_specs=cspec(zvt),
        out_shape=jax.ShapeDtypeStruct(zvt.shape, F32),
        compiler_params=_cparams(("arbitrary",)),
        name="long_conv_ctx",
    )(zvt, filt_t, l1_col, bias_b, *consts)


def _attn_kernel(q_ref, *refs, chunks):
    o_ref = refs[-1]
    tq = q_ref.shape[2]
    q = [q_ref[0, hh] for hh in range(2)]
    m = [jnp.full((tq, 1), -1e30, F32) for _ in range(2)]
    acc = [jnp.zeros((tq, LANE), F32) for _ in range(2)]
    work = [(src, start, size, hh) for (src, start, size) in chunks for hh in range(2)]

    def scores(n):
        src, start, size, hh = work[n]
        k = refs[2 * src][0, hh, start:start + size, :]
        return lax.dot_general(q[hh], k, (((1,), (1,)), ((), ())), preferred_element_type=F32)

    s_next = scores(0)
    for n, (src, start, size, hh) in enumerate(work):
        s = s_next
        if n + 1 < len(work):
            s_next = scores(n + 1)
        v = refs[2 * src + 1][0, hh, start:start + size, :]
        m_new = jnp.maximum(m[hh], jnp.max(s, axis=-1, keepdims=True))
        p = jnp.exp2(s - m_new)
        alpha = jnp.exp2(m[hh] - m_new)
        acc[hh] = alpha * acc[hh] + _mm(p.astype(BF16), v)
        m[hh] = m_new
    lane = lax.broadcasted_iota(jnp.int32, (tq, LANE), 1)
    o0 = acc[0] / acc[0][:, V_HEAD:V_HEAD + 1]
    o1 = acc[1] / acc[1][:, 0:1]
    o_ref[0] = jnp.where(lane < V_HEAD, o0, o1).astype(o_ref.dtype)


def _attention(q, kv_sources, tq, tk):
    b_, H, L, _ = q.shape
    chunks = []
    in_specs = [pl.BlockSpec((1, 2, tq, HEAD_PAD), lambda b, hp, i: (b, hp, i, 0))]
    args = [q]
    for src, (k, v) in enumerate(kv_sources):
        S = k.shape[2]
        step = min(tk, S)
        chunks += [(src, start, step) for start in range(0, S, step)]
        in_specs += [pl.BlockSpec((1, 2, S, HEAD_PAD), lambda b, hp, i: (b, hp, 0, 0))] * 2
        args += [k, v]
    return pl.pallas_call(
        functools.partial(_attn_kernel, chunks=tuple(chunks)),
        grid=(b_, H // 2, L // tq),
        in_specs=in_specs,
        out_specs=pl.BlockSpec((1, tq, LANE), lambda b, hp, i: (b, i, hp)),
        out_shape=jax.ShapeDtypeStruct((b_, L, H * V_HEAD), BF16),
        compiler_params=_cparams(("arbitrary", "arbitrary", "arbitrary")),
        name="attention",
    )(*args)


def _mixffn_kernel(a_ref, x0_ref, yt_ref, o_ref, ga_ref, gb_ref, gc_ref, h_ref, g1_ref, sh2_ref, sc2_ref, g2_ref,
                   gn_ref, wp_ref, wh_ref, wo_ref, wout_ref, w1_ref, w2_ref, fg_ref, out_ref, *, final, nsplit):
    hy = (x0_ref[0].astype(F32) * yt_ref[0].T).astype(BF16)
    a = _mm(a_ref[0], wp_ref[...])
    b = _mm(hy, wh_ref[...])
    cm = _mm(o_ref[0], wo_ref[...])
    m = (jax.nn.sigmoid(ga_ref[0].astype(F32)) * a + jax.nn.sigmoid(gb_ref[0].astype(F32)) * b
         + jax.nn.sigmoid(gc_ref[0].astype(F32)) * cm)
    x = h_ref[0] + g1_ref[0] * _mm(m.astype(BF16), wout_ref[...])
    xn = _mod_norm(x, gn_ref[...], sc2_ref[0], sh2_ref[0]).astype(BF16)
    tf = D_FF // nsplit
    acc = None
    for k in range(nsplit):
        u = jnp.maximum(_mm(xn, w1_ref[:, k * tf:(k + 1) * tf]), 0.0)
        part = _mm((u * u).astype(BF16), w2_ref[k * tf:(k + 1) * tf, :])
        acc = part if acc is None else acc + part
    y = x + g2_ref[0] * acc
    if final:
        y = _rms(y, fg_ref[...])
    out_ref[0] = y


def _mixffn(a_pre, x0c, yt, o, proj, h, mod, mrow, wp, wh, wo, wout, gn, w1, w2, final_g, l, tt, final):
    b_, L, _ = h.shape
    layer = lambda b, i: (l, 0, 0)
    row = lambda b, i: (b, i, 0)
    gcol = COL_GATE // D_MODEL
    resident = dict(pipeline_mode=pl.Buffered(1))
    modrow = lambda k: pl.BlockSpec((1, 1, D_MODEL), lambda b, i: (mrow(b), 0, k))
    return pl.pallas_call(
        functools.partial(_mixffn_kernel, final=final, nsplit=2),
        grid=(b_, L // tt),
        in_specs=[pl.BlockSpec((1, tt, POOL_WIDTH), row),
                  pl.BlockSpec((1, tt, HY_WIDTH), row),
                  pl.BlockSpec((1, HY_WIDTH, tt), lambda b, i: (b, 0, i)),
                  pl.BlockSpec((1, tt, MLA_HEADS * V_HEAD), row),
                  pl.BlockSpec((1, tt, D_MODEL), lambda b, i: (b, i, gcol)),
                  pl.BlockSpec((1, tt, D_MODEL), lambda b, i: (b, i, gcol + 1)),
                  pl.BlockSpec((1, tt, D_MODEL), lambda b, i: (b, i, gcol + 2)),
                  pl.BlockSpec((1, tt, D_MODEL), row),
                  modrow(2), modrow(3), modrow(4), modrow(5),
                  pl.BlockSpec((None, 1, D_MODEL), layer),
                  pl.BlockSpec((None, POOL_WIDTH, D_MODEL), layer, **resident),
                  pl.BlockSpec((None, HY_WIDTH, D_MODEL), layer, **resident),
                  pl.BlockSpec((None, MLA_HEADS * V_HEAD, D_MODEL), layer, **resident),
                  pl.BlockSpec((None, D_MODEL, D_MODEL), layer, **resident),
                  pl.BlockSpec((None, D_MODEL, D_FF), layer, **resident),
                  pl.BlockSpec((None, D_FF, D_MODEL), layer, **resident),
                  pl.BlockSpec((1, D_MODEL), lambda b, i: (0, 0))],
        out_specs=pl.BlockSpec((1, tt, D_MODEL), row),
        out_shape=jax.ShapeDtypeStruct(h.shape, F32),
        compiler_params=pltpu.CompilerParams(dimension_semantics=("arbitrary", "arbitrary"),
                                             vmem_limit_bytes=VMEM_LIMIT_FUSED),
        name="mixffn",
    )(a_pre, x0c, yt, o, proj, proj, proj, h, mod, mod, mod, mod, gn, wp, wh, wo, wout, w1, w2, final_g)


def _rope_tables(n_rows):
    rows = np.repeat(np.arange(n_rows, dtype=np.float64), GRID_W)
    cols = np.tile(np.arange(GRID_W, dtype=np.float64), n_rows)
    half = QK_ROPE // 2
    inv = ROPE_BASE ** (-np.arange(0, half, 2, dtype=np.float64) / half)
    ar = rows[:, None] * inv
    ac = cols[:, None] * inv
    ang = np.concatenate([ar, ar, ac, ac], axis=-1)
    return np.cos(ang), np.sin(ang)


def _head_tables(cos, sin, L):
    ones = np.ones((L, QK_NOPE))
    z64 = np.zeros((L, QK_NOPE))
    z32 = np.zeros((L, HEAD_PAD - QK_DIM))
    if cos is None:
        cos = np.ones((L, QK_ROPE))
        sin = np.zeros((L, QK_ROPE))
    cq = np.concatenate([ones, cos, z32], axis=1) * (MLA_SCALE * LOG2E)
    sq = np.concatenate([z64, sin, z32], axis=1) * (MLA_SCALE * LOG2E)
    ck = np.concatenate([z64, cos, z32], axis=1)
    sk = np.concatenate([z64, sin, z32], axis=1)
    return tuple(np.asarray(t, np.float32) for t in (cq, sq, ck, sk))


def _hy_embed(L, tr):
    t = np.linspace(0.0, 1.0, L)[:, None]
    omega = 2.0 * np.pi * np.arange(L)[:, None] / L
    bands = np.linspace(1e-4, HY_BANDS - 1, HY_BANDS)[None, :]
    z = np.concatenate([t, np.cos(omega * bands), -np.sin(omega * bands)], axis=-1)
    zfull = np.concatenate([z, np.zeros((1, HY_EMB)), z[:0:-1]], axis=0)
    zfull = np.pad(zfull, ((0, 0), (0, HY_HALF - HY_EMB)))
    packed = zfull.reshape(2 * L // tr, 2, tr // 2, HY_HALF).transpose(0, 2, 1, 3).reshape(L, 2 * HY_HALF)
    return np.asarray(packed, np.float32)


def _blockdiag2(a):
    a = _pad2(a, HY_HALF, HY_HALF)
    z = jnp.zeros_like(a)
    return jnp.concatenate([jnp.concatenate([a, z], axis=-1), jnp.concatenate([z, a], axis=-1)], axis=-2)


def _pack_w_in(w):
    z = lambda n: jnp.zeros(w.shape[:-1] + (n,), BF16)
    o_q = 512 + 3 * HY_WIDTH
    o_kv = o_q + Q_LORA
    o_pe = o_kv + KV_LORA
    o_gate = o_pe + QK_ROPE
    kpe = w[..., o_pe:o_gate].astype(BF16)
    mid = jnp.concatenate([
        w[..., o_q:o_pe].astype(BF16),
        z(QK_NOPE), kpe, z(HEAD_PAD - QK_DIM),
        z(QK_NOPE), _rot_half(kpe), z(HEAD_PAD - QK_DIM),
        z(LANE)], axis=-1)
    return w[..., :o_q].astype(BF16), mid, w[..., o_gate:].astype(BF16)


def _pack_heads(w_uq, w_ukv):
    lead = w_uq.shape[:-2]
    lanes = lambda a, lo: jnp.pad(a, [(0, 0)] * (a.ndim - 1) + [(lo, HEAD_PAD - lo - a.shape[-1])])
    wq3 = w_uq.astype(BF16).reshape(*lead, Q_LORA, MLA_HEADS, QK_DIM)
    wq = lanes(wq3, 0).reshape(*lead, Q_LORA, -1)
    wqr = lanes(_rot_half(wq3[..., QK_NOPE:]), QK_NOPE).reshape(*lead, Q_LORA, -1)
    wkv3 = w_ukv.astype(BF16).reshape(*lead, KV_LORA, MLA_HEADS, QK_NOPE + V_HEAD)
    wk = lanes(wkv3[..., :QK_NOPE], 0).reshape(*lead, KV_LORA, -1)
    wv = wkv3[..., QK_NOPE:].reshape(*lead, KV_LORA, -1)
    return wq, wqr, wk, wv


def _pad2(a, r, c):
    pads = [(0, 0)] * (a.ndim - 2) + [(0, r - a.shape[-2]), (0, c - a.shape[-1])]
    return jnp.pad(a, pads)


TT_LAT = 512
TQ = 1024
TK = 512
TR_LAT = 1024


def kernel(x, c, ctx, c_ctx, w_mod, b_mod, norm1_g, norm2_g, w_in, pool_w, pool_scale, pool_out, hy_conv_w, hy_conv_b, hy_f_w1, hy_f_b1, hy_f_freq1, hy_f_w2, hy_f_b2, hy_f_freq2, hy_f_w3, hy_decay, hy_bias, hy_out, q_norm_g, w_uq, kv_norm_g, w_ukv, w_o, w_out, w_ff1, w_ff2, final_g):
    B, L, _ = x.shape
    Lc = ctx.shape[1]
    cos, sin = _rope_tables(L // GRID_W)
    tabs_lat = _head_tables(cos, sin, L)
    tabs_ctx = _head_tables(None, None, Lc)
    z_lat = _hy_embed(L, TR_LAT)
    z_ctx = _hy_embed(Lc, Lc)

    c_all = jnp.concatenate([c, c_ctx[None], jnp.zeros((MOD_ROWS - B - 1, D_MODEL), F32)], axis=0)
    mod = _modulation(c_all, w_mod, b_mod).reshape(DEPTH * MOD_ROWS, 1, N_MOD * D_MODEL)

    row3 = lambda a: a[:, None, :]
    g1n, g2n, gq, gkv = row3(norm1_g), row3(norm2_g), row3(q_norm_g), row3(kv_norm_g)
    w_in_p = _pack_w_in(w_in)
    wq, wqr, wk, wv = _pack_heads(w_uq, w_ukv)
    pw, ps = pool_w.astype(BF16), row3(pool_scale)
    cw, cb = hy_conv_w, row3(hy_conv_b)
    wp, wh, wo, wout = pool_out.astype(BF16), hy_out.astype(BF16), w_o.astype(BF16), w_out.astype(BF16)
    wf1, wf2 = w_ff1.astype(BF16), w_ff2.astype(BF16)
    twice = lambda v: jnp.tile(_pad2(row3(v), 1, HY_HALF), (1, 1, 2))
    fw1, fw2 = _blockdiag2(hy_f_w1), _blockdiag2(hy_f_w2)
    fb1, ff1, fb2, ff2 = twice(hy_f_b1), twice(hy_f_freq1), twice(hy_f_b2), twice(hy_f_freq2)
    w3d = hy_f_w3.reshape(DEPTH, HY_HALF, 2, HY_WIDTH).transpose(0, 2, 1, 3)
    z3 = jnp.zeros_like(w3d)
    fw3 = jnp.stack([jnp.concatenate([w3d, z3], axis=2), jnp.concatenate([z3, w3d], axis=2)], axis=2)
    dec = hy_decay[:, :, None, :]
    fg = final_g[None]

    h, hc = x, ctx
    for l in range(DEPTH):
        last = l == DEPTH - 1
        row_lat = lambda b, l=l: l * MOD_ROWS + b
        row_ctx = lambda b, l=l: l * MOD_ROWS + B

        def stream_pre(hs, mrow, tabs, tt):
            return _inproj(hs, mod, mrow, g1n, w_in_p, gq, gkv, wq, wqr, wk, wv, *tabs, l, tt)

        def stream_mix(hs, mrow, proj, q, kv_sources, zemb, tt, tq, tk, tr, final):
            Ls = hs.shape[1]
            a_pre, x0c, zvt = _seqlocal(proj, pw, ps, cw, cb, l, tt)
            filt_t, l1 = _hyena_filter(zemb, fw1, fb1, ff1, fw2, fb2, ff2, fw3, dec, l, Ls, tr)
            if Ls == FFT_N1 * FFT_N2 // 2:
                yt = _long_conv_lat(zvt, filt_t, l1, hy_bias, l)
            else:
                yt = _long_conv_ctx(zvt, filt_t, l1, hy_bias[l])
            o = _attention(q, kv_sources, tq, tk)
            return _mixffn(a_pre, x0c, yt, o, proj, hs, mod, mrow, wp, wh, wo, wout, g2n, wf1, wf2, fg, l, tt, final)

        proj_c, q_c, k_c, v_c = stream_pre(hc, row_ctx, tabs_ctx, Lc)
        proj_l, q_l, k_l, v_l = stream_pre(h, row_lat, tabs_lat, TT_LAT)
        h = stream_mix(h, row_lat, proj_l, q_l, [(k_c, v_c), (k_l, v_l)], z_lat, TT_LAT, TQ, TK, TR_LAT, last)
        if not last:
            hc = stream_mix(hc, row_ctx, proj_c, q_c, [(k_c, v_c)], z_ctx, Lc, Lc, Lc, Lc, False)
    return h
```

```python
import functools
import math

import jax
import jax.numpy as jnp
import numpy as np
from jax import lax
from jax.experimental import pallas as pl
from jax.experimental.pallas import tpu as pltpu

F32 = jnp.float32
BF16 = jnp.bfloat16

D_MODEL = 1024
DEPTH = 4
GRID_W = 64
EPS = 1e-6

POOL_WIDTH = 512
POOL_WINDOWS = (2, 4, 8, 16)
POOL_GROUP = 128
HY_WIDTH = 512
HY_EMB = 33
HY_BANDS = 16
HY_HALF = 64
MLA_HEADS = 8
QK_NOPE = 64
QK_ROPE = 32
QK_DIM = 96
V_HEAD = 64
Q_LORA = 384
KV_LORA = 256
MLA_SCALE = QK_DIM ** -0.5
LOG2E = math.log2(math.e)
ROPE_BASE = 10000.0
D_FF = 4 * D_MODEL
N_MOD = 6
MOD_ROWS = 8

LANE = 128
HALO = 16
HEAD_PAD = 128

COL_POOL = 0
COL_HY = 512
COL_GATE = 2048
N_PROJ = 5120

VMEM_LIMIT = 50 * 1024 * 1024
VMEM_LIMIT_FUSED = 58 * 1024 * 1024


def _cparams(sem):
    return pltpu.CompilerParams(dimension_semantics=sem, vmem_limit_bytes=VMEM_LIMIT)


def _mm(a, b):
    return jnp.dot(a, b, preferred_element_type=F32)


def _split(x):
    hi = x.astype(BF16)
    lo = (x - hi.astype(F32)).astype(BF16)
    return hi, lo


def _dot3(a, b):
    ah, al = _split(a)
    bh, bl = _split(b)
    return _mm(ah, bh) + _mm(al, bh) + _mm(ah, bl)


def _dot3c(a, bh, bl):
    ah, al = _split(a)
    return _mm(ah, bh) + _mm(al, bh) + _mm(ah, bl)


def _cdot3(ch, cl, b):
    bh, bl = _split(b)
    n = b.shape[1]
    r = _mm(ch, jnp.concatenate([bh, bl], axis=1))
    return r[:, :n] + r[:, n:] + _mm(cl, bh)


def _hl(m):
    m32 = np.asarray(m, np.float32)
    hi = m32.astype(BF16)
    lo = (m32 - hi.astype(np.float32)).astype(BF16)
    return hi, lo


def _rot_half(x):
    q = QK_ROPE // 4
    return jnp.concatenate([-x[..., q:2 * q], x[..., :q], -x[..., 3 * q:], x[..., 2 * q:3 * q]], axis=-1)


def _mod_kernel(c_ref, w_ref, b_ref, o_ref):
    c = c_ref[...]
    s = c * jax.nn.sigmoid(c)
    o_ref[0] = _dot3(s, w_ref[0]) + b_ref[0]


def _modulation(c_all, w_mod, b_mod):
    tn = 1536
    n = N_MOD * D_MODEL
    return pl.pallas_call(
        _mod_kernel,
        grid=(DEPTH, n // tn),
        in_specs=[pl.BlockSpec((MOD_ROWS, D_MODEL), lambda l, j: (0, 0)),
                  pl.BlockSpec((1, D_MODEL, tn), lambda l, j: (l, 0, j)),
                  pl.BlockSpec((1, 1, tn), lambda l, j: (l, 0, j))],
        out_specs=pl.BlockSpec((1, MOD_ROWS, tn), lambda l, j: (l, 0, j)),
        out_shape=jax.ShapeDtypeStruct((DEPTH, MOD_ROWS, n), F32),
        compiler_params=_cparams(("arbitrary", "arbitrary")),
        name="modulation",
    )(c_all, w_mod, b_mod.reshape(DEPTH, 1, n))


def _mod_norm(x, g, sc, sh):
    ms = jnp.mean(x * x, axis=-1, keepdims=True)
    return (x * lax.rsqrt(ms + EPS) * g) * (1.0 + sc) + sh


def _rms(x, g):
    ms = jnp.mean(x * x, axis=-1, keepdims=True)
    return x * lax.rsqrt(ms + EPS) * g


def _inproj_kernel(x_ref, sh_ref, sc_ref, g_ref, wa_ref, wm_ref, wg_ref, gq_ref, gkv_ref,
                   wq_ref, wqr_ref, wk_ref, wv_ref, cq_ref, sq_ref, ck_ref, sk_ref,
                   proj_ref, q_ref, k_ref, v_ref):
    xn = _mod_norm(x_ref[0], g_ref[...], sc_ref[0], sh_ref[0]).astype(BF16)
    proj_ref[0, :, :COL_GATE] = _mm(xn, wa_ref[...]).astype(proj_ref.dtype)
    proj_ref[0, :, COL_GATE:] = _mm(xn, wg_ref[...]).astype(proj_ref.dtype)
    mid = _mm(xn, wm_ref[...])
    cq = mid[:, :Q_LORA]
    ckv = mid[:, Q_LORA:Q_LORA + KV_LORA]
    o = Q_LORA + KV_LORA
    kpe = mid[:, o:o + LANE]
    kpe_rot = mid[:, o + LANE:o + 2 * LANE]
    qn = _rms(cq, gq_ref[...]).astype(BF16)
    q = _mm(qn, wq_ref[...])
    qr = _mm(qn, wqr_ref[...])
    kvn = _rms(ckv, gkv_ref[...]).astype(BF16)
    k = _mm(kvn, wk_ref[...])
    v = _mm(kvn, wv_ref[...])
    cq_t, sq_t = cq_ref[...], sq_ref[...]
    kpe_r = kpe * ck_ref[...] + kpe_rot * sk_ref[...]
    lane = lax.broadcasted_iota(jnp.int32, (1, HEAD_PAD), 1)
    for h in range(MLA_HEADS):
        sl = slice(h * HEAD_PAD, (h + 1) * HEAD_PAD)
        q_ref[0, h] = (q[:, sl] * cq_t + qr[:, sl] * sq_t).astype(q_ref.dtype)
        k_ref[0, h] = (k[:, sl] + kpe_r).astype(k_ref.dtype)
        ones_col = (lane == (V_HEAD if h % 2 == 0 else 0)).astype(F32)
        v_ref[0, h] = (v[:, sl] + ones_col).astype(v_ref.dtype)


def _inproj(h, mod, mrow, g, ws, gq, gkv, wq, wqr, wk, wv, cq, sq, ck, sk, l, tt):
    b_, L, _ = h.shape
    layer = lambda b, i: (l, 0, 0)
    resident = dict(pipeline_mode=pl.Buffered(1))
    hw = MLA_HEADS * HEAD_PAD
    table = pl.BlockSpec((tt, LANE), lambda b, i: (i, 0))
    heads = pl.BlockSpec((1, MLA_HEADS, tt, HEAD_PAD), lambda b, i: (b, 0, i, 0))
    return pl.pallas_call(
        _inproj_kernel,
        grid=(b_, L // tt),
        in_specs=[pl.BlockSpec((1, tt, D_MODEL), lambda b, i: (b, i, 0)),
                  pl.BlockSpec((1, 1, D_MODEL), lambda b, i: (mrow(b), 0, 0)),
                  pl.BlockSpec((1, 1, D_MODEL), lambda b, i: (mrow(b), 0, 1)),
                  pl.BlockSpec((None, 1, D_MODEL), layer)]
        + [pl.BlockSpec((None, D_MODEL, w.shape[2]), layer, **resident) for w in ws]
        + [pl.BlockSpec((None, 1, Q_LORA), layer), pl.BlockSpec((None, 1, KV_LORA), layer),
           pl.BlockSpec((None, Q_LORA, hw), layer, **resident), pl.BlockSpec((None, Q_LORA, hw), layer, **resident),
           pl.BlockSpec((None, KV_LORA, hw), layer, **resident), pl.BlockSpec((None, KV_LORA, hw), layer, **resident),
           table, table, table, table],
        out_specs=[pl.BlockSpec((1, tt, N_PROJ), lambda b, i: (b, i, 0)), heads, heads, heads],
        out_shape=[jax.ShapeDtypeStruct((b_, L, N_PROJ), BF16)]
        + [jax.ShapeDtypeStruct((b_, MLA_HEADS, L, HEAD_PAD), BF16)] * 3,
        compiler_params=_cparams(("arbitrary", "arbitrary")),
        name="inproj",
    )(h, mod, mod, g, *ws, gq, gkv, wq, wqr, wk, wv, cq, sq, ck, sk)


def _seqlocal_kernel(cur_ref, prev_ref, next_ref, pw_ref, ps_ref, cw_ref, cb_ref,
                     a_ref, x0_ref, zvt_ref, *, tt, L):
    i = pl.program_id(1)
    nt = pl.num_programs(1)
    keep_prev = (i > 0).astype(F32)
    keep_next = (i < nt - 1).astype(F32)
    n = tt + 2 * HALO

    def ext(lo, hi):
        return jnp.concatenate([prev_ref[0, :, lo:hi].astype(F32) * keep_prev,
                                cur_ref[0, :, lo:hi].astype(F32),
                                next_ref[0, :, lo:hi].astype(F32) * keep_next], axis=0)

    def roll(x, s):
        return pltpu.roll(x, s % n, 0)

    t = i * tt + lax.broadcasted_iota(jnp.int32, (tt, POOL_GROUP), 0)
    for gi, w in enumerate(POOL_WINDOWS):
        lo = COL_POOL + gi * POOL_GROUP
        xg = ext(lo, lo + POOL_GROUP)
        s = xg + roll(xg, 1)
        if w >= 4:
            s = roll(s, 1) + roll(s, -1)
        if w >= 8:
            s = roll(s, 2) + roll(s, -2)
        if w >= 16:
            s = roll(s, 4) + roll(s, -4)
        cnt = (jnp.minimum(t + w // 2, L) - jnp.maximum(t - w // 2, 0)).astype(F32)
        u = s[HALO:HALO + tt] / cnt - xg[HALO:HALO + tt]
        ug = _mm(u.astype(BF16), pw_ref[gi]) * ps_ref[:, gi * POOL_GROUP:(gi + 1) * POOL_GROUP]
        a_ref[0, :, gi * POOL_GROUP:(gi + 1) * POOL_GROUP] = ug.astype(a_ref.dtype)

    def conv(part, j):
        c0 = part * HY_WIDTH + j * LANE
        xe = ext(COL_HY + c0, COL_HY + c0 + LANE)
        y = (roll(xe, 1) * cw_ref[0:1, c0:c0 + LANE] + xe * cw_ref[1:2, c0:c0 + LANE]
             + roll(xe, -1) * cw_ref[2:3, c0:c0 + LANE] + cb_ref[:, c0:c0 + LANE])
        return y[HALO:HALO + tt]

    for j in range(HY_WIDTH // LANE):
        x0_ref[0, :, j * LANE:(j + 1) * LANE] = conv(0, j).astype(x0_ref.dtype)
        zv = conv(2, j) * conv(1, j)
        zvt_ref[0, j * LANE:(j + 1) * LANE, :] = zv.T


def _seqlocal(proj, pool_w, pool_scale, conv_w, conv_b, l, tt):
    b_, L, _ = proj.shape
    hb = tt // HALO
    nh = L // HALO
    wide = COL_GATE
    return pl.pallas_call(
        functools.partial(_seqlocal_kernel, tt=tt, L=L),
        grid=(b_, L // tt),
        in_specs=[pl.BlockSpec((1, tt, wide), lambda b, i: (b, i, 0)),
                  pl.BlockSpec((1, HALO, wide), lambda b, i: (b, jnp.maximum(i * hb - 1, 0), 0)),
                  pl.BlockSpec((1, HALO, wide), lambda b, i: (b, jnp.minimum((i + 1) * hb, nh - 1), 0)),
                  pl.BlockSpec((None, len(POOL_WINDOWS), POOL_GROUP, POOL_GROUP), lambda b, i: (l, 0, 0, 0)),
                  pl.BlockSpec((None, 1, POOL_WIDTH), lambda b, i: (l, 0, 0)),
                  pl.BlockSpec((None, 3, 3 * HY_WIDTH), lambda b, i: (l, 0, 0)),
                  pl.BlockSpec((None, 1, 3 * HY_WIDTH), lambda b, i: (l, 0, 0))],
        out_specs=[pl.BlockSpec((1, tt, POOL_WIDTH), lambda b, i: (b, i, 0)),
                   pl.BlockSpec((1, tt, HY_WIDTH), lambda b, i: (b, i, 0)),
                   pl.BlockSpec((1, HY_WIDTH, tt), lambda b, i: (b, 0, i))],
        out_shape=[jax.ShapeDtypeStruct((b_, L, POOL_WIDTH), BF16),
                   jax.ShapeDtypeStruct((b_, L, HY_WIDTH), BF16),
                   jax.ShapeDtypeStruct((b_, HY_WIDTH, L), F32)],
        compiler_params=_cparams(("arbitrary", "arbitrary")),
        name="seqlocal",
    )(proj, proj, proj, pool_w, pool_scale, conv_w, conv_b)


def _filter_kernel(z_ref, w1_ref, b1_ref, f1_ref, w2_ref, b2_ref, f2_ref, w3_ref, dec_ref,
                   filt_ref, l1_ref, *, tr, L):
    i = pl.program_id(0)
    half = tr // 2
    z = z_ref[...]
    hid = jnp.sin(f1_ref[...] * (_dot3(z, w1_ref[...]) + b1_ref[...]))
    hid = jnp.sin(f2_ref[...] * (_dot3(hid, w2_ref[...]) + b2_ref[...]))
    decay = jnp.abs(dec_ref[0])
    total = None
    for s in range(2):
        h = _dot3(hid, w3_ref[0, s]) * jnp.exp(-z[:, s * HY_HALF:s * HY_HALF + 1] * decay)
        row = i * tr + s * half + lax.broadcasted_iota(jnp.int32, h.shape, 0)
        h = jnp.where(row == L, 0.0, h)
        filt_ref[:, s * half:(s + 1) * half] = h.T
        part = jnp.sum(jnp.abs(h), axis=0, keepdims=True)
        total = part if total is None else total + part

    @pl.when(i == 0)
    def _():
        l1_ref[...] = jnp.zeros_like(l1_ref)

    l1_ref[...] += total


def _hyena_filter(zpacked, w1, b1, f1, w2, b2, f2, w3, dec, l, L, tr):
    n = 2 * L
    nhalf = L // tr
    full = lambda i: (0, 0)
    mat = pl.BlockSpec((None, LANE, LANE), lambda i: (l, 0, 0))
    vec = pl.BlockSpec((None, 1, LANE), lambda i: (l, 0, 0))
    return pl.pallas_call(
        functools.partial(_filter_kernel, tr=tr, L=L),
        grid=(n // tr,),
        in_specs=[pl.BlockSpec((tr // 2, LANE), lambda i: (i, 0)),
                  mat, vec, vec, mat, vec, vec,
                  pl.BlockSpec((None, 1, 2, LANE, HY_WIDTH), lambda i: (l, i // nhalf, 0, 0, 0)),
                  pl.BlockSpec((None, 1, 1, HY_WIDTH), lambda i: (l, i // nhalf, 0, 0))],
        out_specs=[pl.BlockSpec((HY_WIDTH, tr), lambda i: (0, i)),
                   pl.BlockSpec((1, HY_WIDTH), full)],
        out_shape=[jax.ShapeDtypeStruct((HY_WIDTH, n), F32),
                   jax.ShapeDtypeStruct((1, HY_WIDTH), F32)],
        compiler_params=_cparams(("arbitrary",)),
        name="hyena_filter",
    )(zpacked, w1, b1, f1, w2, b2, f2, w3, dec)


FFT_N1 = 128
FFT_N2 = 128
CONV_CB = 32


def _lconv_kernel(bias_ref, l1_ref, z_ref, f_ref, f1d_h, f1d_l, f1f_h, f1f_l, tr_ref, ti_ref,
                  f2_h, f2_l, g2_h, g2_l, e1_h, e1_l, y_ref, bs_ref, bf_ref, cm_ref, ys_ref, *, layer):
    blk = pl.program_id(0)
    tw_r = tr_ref[...]
    tw_i = ti_ref[...]
    n2 = FFT_N2
    for c in range(CONV_CB):
        rows = slice(c * FFT_N1, (c + 1) * FFT_N1)
        zc = jnp.concatenate([z_ref[0, c], z_ref[1, c]], axis=0)
        a = _cdot3(f1d_h[...], f1d_l[...], zc)
        ar, ai = a[:FFT_N1], a[FFT_N1:]
        bs_ref[rows, :n2] = ar * tw_r - ai * tw_i
        bs_ref[rows, n2:] = ar * tw_i + ai * tw_r
        inv_l1 = 1.0 / jnp.full((1, n2), l1_ref[0, blk * CONV_CB + c], F32)
        a = _cdot3(f1f_h[...], f1f_l[...], f_ref[c] * inv_l1)
        ar, ai = a[:FFT_N1], a[FFT_N1:]
        bf_ref[rows, :n2] = ar * tw_r - ai * tw_i
        bf_ref[rows, n2:] = ar * tw_i + ai * tw_r
    x = _dot3c(bs_ref[...], f2_h[...], f2_l[...])
    hf = _dot3c(bf_ref[...], f2_h[...], f2_l[...])
    xr, xi = x[:, :n2], x[:, n2:]
    hr, hi = hf[:, :n2], hf[:, n2:]
    y = jnp.concatenate([xr * hr - xi * hi, xr * hi + xi * hr], axis=1)
    cm_ref[...] = _dot3c(y, g2_h[...], g2_l[...])
    half = FFT_N1 // 2
    for c in range(CONV_CB):
        rows = slice(c * FFT_N1, (c + 1) * FFT_N1)
        cr = cm_ref[rows, :n2]
        ci = cm_ref[rows, n2:]
        dc = jnp.concatenate([cr * tw_r + ci * tw_i, ci * tw_r - cr * tw_i], axis=0)
        yc = _cdot3(e1_h[...], e1_l[...], dc)
        bias = bias_ref[layer, blk * CONV_CB + c]
        ys_ref[0, c] = yc[:half] + z_ref[0, c] * bias
        ys_ref[1, c] = yc[half:] + z_ref[1, c] * bias
    sub = 8
    for b in range(2):
        for cg in range(CONV_CB // sub):
            for ng in range(half // sub):
                blk8 = ys_ref[b, cg * sub:(cg + 1) * sub, ng * sub:(ng + 1) * sub, :]
                t8 = jnp.swapaxes(blk8, 0, 1)
                for j in range(sub):
                    n1 = ng * sub + j
                    y_ref[b, cg * sub:(cg + 1) * sub, n1 * FFT_N2:(n1 + 1) * FFT_N2] = t8[j]


def _lconv_consts():
    n1, n2 = FFT_N1, FFT_N2
    n = n1 * n2
    idx = np.arange(128)
    ang1 = 2.0 * np.pi * ((idx[:, None] * idx[None, :]) % 128) / 128.0
    fr, fi = np.cos(ang1), -np.sin(ang1)
    angt = 2.0 * np.pi * (idx[:, None] * idx[None, :]) / n
    tw_r, tw_i = np.cos(angt), -np.sin(angt)
    h = n1 // 2
    f1d = np.block([[fr[:, :h], -fi[:, :h]], [fi[:, :h], fr[:, :h]]])
    f1f = np.concatenate([fr, fi], axis=0)
    f2 = np.block([[fr, fi], [-fi, fr]])
    g2 = np.block([[fr, -fi], [fi, fr]])
    er, ei = fr[:h] / n, -fi[:h] / n
    e1 = np.block([[er, -ei], [ei, er]])

    return (*_hl(f1d), *_hl(f1f), np.asarray(tw_r, np.float32), np.asarray(tw_i, np.float32),
            *_hl(f2), *_hl(g2), *_hl(e1))


def _long_conv_lat(zvt, filt_t, l1, bias, l):
    b_, C, L = zvt.shape
    assert b_ == 2 and L == FFT_N1 * FFT_N2 // 2
    z4 = zvt.reshape(b_, C, FFT_N1 // 2, FFT_N2)
    f3 = filt_t.reshape(C, FFT_N1, FFT_N2)
    consts = _lconv_consts()
    cb = CONV_CB

    def cspec(a):
        return pl.BlockSpec(a.shape, lambda i: (0,) * a.ndim)

    return pl.pallas_call(
        functools.partial(_lconv_kernel, layer=l),
        grid=(C // cb,),
        in_specs=[pl.BlockSpec(memory_space=pltpu.SMEM), pl.BlockSpec(memory_space=pltpu.SMEM),
                  pl.BlockSpec((b_, cb, FFT_N1 // 2, FFT_N2), lambda i: (0, i, 0, 0)),
                  pl.BlockSpec((cb, FFT_N1, FFT_N2), lambda i: (i, 0, 0))] + [cspec(a) for a in consts],
        out_specs=pl.BlockSpec((b_, cb, L), lambda i: (0, i, 0)),
        out_shape=jax.ShapeDtypeStruct(zvt.shape, F32),
        scratch_shapes=[pltpu.VMEM((cb * FFT_N1, 2 * FFT_N2), F32),
                        pltpu.VMEM((cb * FFT_N1, 2 * FFT_N2), F32),
                        pltpu.VMEM((cb * FFT_N1, 2 * FFT_N2), F32),
                        pltpu.VMEM((b_, cb, FFT_N1 // 2, FFT_N2), F32)],
        compiler_params=_cparams(("arbitrary",)),
        name="long_conv",
    )(bias, l1, z4, f3, *consts)


def _cconv_kernel(z_ref, f_ref, l1_ref, bias_ref, fd_h, fd_l, ff_h, ff_l, fi_h, fi_l, y_ref, *, L):
    zr, zi = z_ref[0], z_ref[1]
    x = _dot3c(jnp.concatenate([zr, zi], axis=1), fd_h[...], fd_l[...])
    hf = _dot3c(f_ref[...] * (1.0 / l1_ref[...]), ff_h[...], ff_l[...])
    n = 2 * L
    xr, xi = x[:, :n], x[:, n:]
    hr, hi = hf[:, :n], hf[:, n:]
    y = jnp.concatenate([xr * hr - xi * hi, xr * hi + xi * hr], axis=1)
    out = _dot3c(y, fi_h[...], fi_l[...])
    bias = bias_ref[...]
    y_ref[0] = out[:, :L] + zr * bias
    y_ref[1] = out[:, L:] + zi * bias


def _cconv_consts(L):
    n = 2 * L
    idx = np.arange(n)
    ang = 2.0 * np.pi * ((idx[:, None] * idx[None, :]) % n) / n
    fr, fi = np.cos(ang), -np.sin(ang)
    fd = np.block([[fr[:L], fi[:L]], [-fi[:L], fr[:L]]])
    ff = np.concatenate([fr, fi], axis=1)
    er, ei = fr[:, :L] / n, -fi[:, :L] / n
    finv = np.block([[er, ei], [-ei, er]])

    return (*_hl(fd), *_hl(ff), *_hl(finv))


def _long_conv_ctx(zvt, filt_t, l1, bias):
    b_, C, L = zvt.shape
    consts = _cconv_consts(L)
    bias_b = jnp.broadcast_to(bias[:, None], (C, L))
    l1_col = l1.reshape(C, 1)

    def cspec(a):
        return pl.BlockSpec(a.shape, lambda i: (0,) * a.ndim)

    return pl.pallas_call(
        functools.partial(_cconv_kernel, L=L),
        grid=(1,),
        in_specs=[cspec(zvt), cspec(filt_t), cspec(l1_col), cspec(bias_b)] + [cspec(a) for a in consts],
        out_specs=cspec(zvt),
        out_shape=jax.ShapeDtypeStruct(zvt.shape, F32),
        compiler_params=_cparams(("arbitrary",)),
        name="long_conv_ctx",
    )(zvt, filt_t, l1_col, bias_b, *consts)


def _attn_kernel(q_ref, *refs, chunks):
    o_ref = refs[-1]
    tq = q_ref.shape[2]
    q = [q_ref[0, hh] for hh in range(2)]
    m = [jnp.full((tq, 1), -1e30, F32) for _ in range(2)]
    acc = [jnp.zeros((tq, LANE), F32) for _ in range(2)]
    work = [(src, start, size, hh) for (src, start, size) in chunks for hh in range(2)]

    def scores(n):
        src, start, size, hh = work[n]
        k = refs[2 * src][0, hh, start:start + size, :]
        return lax.dot_general(q[hh], k, (((1,), (1,)), ((), ())), preferred_element_type=F32)

    s_next = scores(0)
    for n, (src, start, size, hh) in enumerate(work):
        s = s_next
        if n + 1 < len(work):
            s_next = scores(n + 1)
        v = refs[2 * src + 1][0, hh, start:start + size, :]
        m_new = jnp.maximum(m[hh], jnp.max(s, axis=-1, keepdims=True))
        p = jnp.exp2(s - m_new)
        alpha = jnp.exp2(m[hh] - m_new)
        acc[hh] = alpha * acc[hh] + _mm(p.astype(BF16), v)
        m[hh] = m_new
    lane = lax.broadcasted_iota(jnp.int32, (tq, LANE), 1)
    o0 = acc[0] / acc[0][:, V_HEAD:V_HEAD + 1]
    o1 = acc[1] / acc[1][:, 0:1]
    o_ref[0] = jnp.where(lane < V_HEAD, o0, o1).astype(o_ref.dtype)


def _attention(q, kv_sources, tq, tk):
    b_, H, L, _ = q.shape
    chunks = []
    in_specs = [pl.BlockSpec((1, 2, tq, HEAD_PAD), lambda b, hp, i: (b, hp, i, 0))]
    args = [q]
    for src, (k, v) in enumerate(kv_sources):
        S = k.shape[2]
        step = min(tk, S)
        chunks += [(src, start, step) for start in range(0, S, step)]
        in_specs += [pl.BlockSpec((1, 2, S, HEAD_PAD), lambda b, hp, i: (b, hp, 0, 0))] * 2
        args += [k, v]
    return pl.pallas_call(
        functools.partial(_attn_kernel, chunks=tuple(chunks)),
        grid=(b_, H // 2, L // tq),
        in_specs=in_specs,
        out_specs=pl.BlockSpec((1, tq, LANE), lambda b, hp, i: (b, i, hp)),
        out_shape=jax.ShapeDtypeStruct((b_, L, H * V_HEAD), BF16),
        compiler_params=_cparams(("arbitrary", "arbitrary", "arbitrary")),
        name="attention",
    )(*args)


def _mixffn_kernel(a_ref, x0_ref, yt_ref, o_ref, ga_ref, gb_ref, gc_ref, h_ref, g1_ref, sh2_ref, sc2_ref, g2_ref,
                   gn_ref, wp_ref, wh_ref, wo_ref, wout_ref, w1_ref, w2_ref, fg_ref, out_ref, *, final, nsplit):
    hy = (x0_ref[0].astype(F32) * yt_ref[0].T).astype(BF16)
    a = _mm(a_ref[0], wp_ref[...])
    b = _mm(hy, wh_ref[...])
    cm = _mm(o_ref[0], wo_ref[...])
    m = (jax.nn.sigmoid(ga_ref[0].astype(F32)) * a + jax.nn.sigmoid(gb_ref[0].astype(F32)) * b
         + jax.nn.sigmoid(gc_ref[0].astype(F32)) * cm)
    x = h_ref[0] + g1_ref[0] * _mm(m.astype(BF16), wout_ref[...])
    xn = _mod_norm(x, gn_ref[...], sc2_ref[0], sh2_ref[0]).astype(BF16)
    tf = D_FF // nsplit
    acc = None
    for k in range(nsplit):
        u = jnp.maximum(_mm(xn, w1_ref[:, k * tf:(k + 1) * tf]), 0.0)
        part = _mm((u * u).astype(BF16), w2_ref[k * tf:(k + 1) * tf, :])
        acc = part if acc is None else acc + part
    y = x + g2_ref[0] * acc
    if final:
        y = _rms(y, fg_ref[...])
    out_ref[0] = y


def _mixffn(a_pre, x0c, yt, o, proj, h, mod, mrow, wp, wh, wo, wout, gn, w1, w2, final_g, l, tt, final):
    b_, L, _ = h.shape
    layer = lambda b, i: (l, 0, 0)
    row = lambda b, i: (b, i, 0)
    gcol = COL_GATE // D_MODEL
    resident = dict(pipeline_mode=pl.Buffered(1))
    modrow = lambda k: pl.BlockSpec((1, 1, D_MODEL), lambda b, i: (mrow(b), 0, k))
    return pl.pallas_call(
        functools.partial(_mixffn_kernel, final=final, nsplit=2),
        grid=(b_, L // tt),
        in_specs=[pl.BlockSpec((1, tt, POOL_WIDTH), row),
                  pl.BlockSpec((1, tt, HY_WIDTH), row),
                  pl.BlockSpec((1, HY_WIDTH, tt), lambda b, i: (b, 0, i)),
                  pl.BlockSpec((1, tt, MLA_HEADS * V_HEAD), row),
                  pl.BlockSpec((1, tt, D_MODEL), lambda b, i: (b, i, gcol)),
                  pl.BlockSpec((1, tt, D_MODEL), lambda b, i: (b, i, gcol + 1)),
                  pl.BlockSpec((1, tt, D_MODEL), lambda b, i: (b, i, gcol + 2)),
                  pl.BlockSpec((1, tt, D_MODEL), row),
                  modrow(2), modrow(3), modrow(4), modrow(5),
                  pl.BlockSpec((None, 1, D_MODEL), layer),
                  pl.BlockSpec((None, POOL_WIDTH, D_MODEL), layer, **resident),
                  pl.BlockSpec((None, HY_WIDTH, D_MODEL), layer, **resident),
                  pl.BlockSpec((None, MLA_HEADS * V_HEAD, D_MODEL), layer, **resident),
                  pl.BlockSpec((None, D_MODEL, D_MODEL), layer, **resident),
                  pl.BlockSpec((None, D_MODEL, D_FF), layer, **resident),
                  pl.BlockSpec((None, D_FF, D_MODEL), layer, **resident),
                  pl.BlockSpec((1, D_MODEL), lambda b, i: (0, 0))],
        out_specs=pl.BlockSpec((1, tt, D_MODEL), row),
        out_shape=jax.ShapeDtypeStruct(h.shape, F32),
        compiler_params=pltpu.CompilerParams(dimension_semantics=("arbitrary", "arbitrary"),
                                             vmem_limit_bytes=VMEM_LIMIT_FUSED),
        name="mixffn",
    )(a_pre, x0c, yt, o, proj, proj, proj, h, mod, mod, mod, mod, gn, wp, wh, wo, wout, w1, w2, final_g)


def _rope_tables(n_rows):
    rows = np.repeat(np.arange(n_rows, dtype=np.float64), GRID_W)
    cols = np.tile(np.arange(GRID_W, dtype=np.float64), n_rows)
    half = QK_ROPE // 2
    inv = ROPE_BASE ** (-np.arange(0, half, 2, dtype=np.float64) / half)
    ar = rows[:, None] * inv
    ac = cols[:, None] * inv
    ang = np.concatenate([ar, ar, ac, ac], axis=-1)
    return np.cos(ang), np.sin(ang)


def _head_tables(cos, sin, L):
    ones = np.ones((L, QK_NOPE))
    z64 = np.zeros((L, QK_NOPE))
    z32 = np.zeros((L, HEAD_PAD - QK_DIM))
    if cos is None:
        cos = np.ones((L, QK_ROPE))
        sin = np.zeros((L, QK_ROPE))
    cq = np.concatenate([ones, cos, z32], axis=1) * (MLA_SCALE * LOG2E)
    sq = np.concatenate([z64, sin, z32], axis=1) * (MLA_SCALE * LOG2E)
    ck = np.concatenate([z64, cos, z32], axis=1)
    sk = np.concatenate([z64, sin, z32], axis=1)
    return tuple(np.asarray(t, np.float32) for t in (cq, sq, ck, sk))


def _hy_embed(L, tr):
    t = np.linspace(0.0, 1.0, L)[:, None]
    omega = 2.0 * np.pi * np.arange(L)[:, None] / L
    bands = np.linspace(1e-4, HY_BANDS - 1, HY_BANDS)[None, :]
    z = np.concatenate([t, np.cos(omega * bands), -np.sin(omega * bands)], axis=-1)
    zfull = np.concatenate([z, np.zeros((1, HY_EMB)), z[:0:-1]], axis=0)
    zfull = np.pad(zfull, ((0, 0), (0, HY_HALF - HY_EMB)))
    packed = zfull.reshape(2 * L // tr, 2, tr // 2, HY_HALF).transpose(0, 2, 1, 3).reshape(L, 2 * HY_HALF)
    return np.asarray(packed, np.float32)


def _blockdiag2(a):
    a = _pad2(a, HY_HALF, HY_HALF)
    z = jnp.zeros_like(a)
    return jnp.concatenate([jnp.concatenate([a, z], axis=-1), jnp.concatenate([z, a], axis=-1)], axis=-2)


def _pack_w_in(w):
    z = lambda n: jnp.zeros(w.shape[:-1] + (n,), BF16)
    o_q = 512 + 3 * HY_WIDTH
    o_kv = o_q + Q_LORA
    o_pe = o_kv + KV_LORA
    o_gate = o_pe + QK_ROPE
    kpe = w[..., o_pe:o_gate].astype(BF16)
    mid = jnp.concatenate([
        w[..., o_q:o_pe].astype(BF16),
        z(QK_NOPE), kpe, z(HEAD_PAD - QK_DIM),
        z(QK_NOPE), _rot_half(kpe), z(HEAD_PAD - QK_DIM),
        z(LANE)], axis=-1)
    return w[..., :o_q].astype(BF16), mid, w[..., o_gate:].astype(BF16)


def _pack_heads(w_uq, w_ukv):
    lead = w_uq.shape[:-2]
    lanes = lambda a, lo: jnp.pad(a, [(0, 0)] * (a.ndim - 1) + [(lo, HEAD_PAD - lo - a.shape[-1])])
    wq3 = w_uq.astype(BF16).reshape(*lead, Q_LORA, MLA_HEADS, QK_DIM)
    wq = lanes(wq3, 0).reshape(*lead, Q_LORA, -1)
    wqr = lanes(_rot_half(wq3[..., QK_NOPE:]), QK_NOPE).reshape(*lead, Q_LORA, -1)
    wkv3 = w_ukv.astype(BF16).reshape(*lead, KV_LORA, MLA_HEADS, QK_NOPE + V_HEAD)
    wk = lanes(wkv3[..., :QK_NOPE], 0).reshape(*lead, KV_LORA, -1)
    v5 = wkv3[..., QK_NOPE:].reshape(*lead, KV_LORA, MLA_HEADS // 2, 2, 1, V_HEAD)
    wv = (v5 * jnp.eye(2, dtype=BF16)[:, :, None]).reshape(*lead, KV_LORA, -1)
    return wq, wqr, wk, wv


def _pad2(a, r, c):
    pads = [(0, 0)] * (a.ndim - 2) + [(0, r - a.shape[-2]), (0, c - a.shape[-1])]
    return jnp.pad(a, pads)


TT_LAT = 512
TQ = 1024
TK = 1024
TR_LAT = 1024


def kernel(x, c, ctx, c_ctx, w_mod, b_mod, norm1_g, norm2_g, w_in, pool_w, pool_scale, pool_out, hy_conv_w, hy_conv_b, hy_f_w1, hy_f_b1, hy_f_freq1, hy_f_w2, hy_f_b2, hy_f_freq2, hy_f_w3, hy_decay, hy_bias, hy_out, q_norm_g, w_uq, kv_norm_g, w_ukv, w_o, w_out, w_ff1, w_ff2, final_g):
    B, L, _ = x.shape
    Lc = ctx.shape[1]
    cos, sin = _rope_tables(L // GRID_W)
    tabs_lat = _head_tables(cos, sin, L)
    tabs_ctx = _head_tables(None, None, Lc)
    z_lat = _hy_embed(L, TR_LAT)
    z_ctx = _hy_embed(Lc, Lc)

    c_all = jnp.concatenate([c, c_ctx[None], jnp.zeros((MOD_ROWS - B - 1, D_MODEL), F32)], axis=0)
    mod = _modulation(c_all, w_mod, b_mod).reshape(DEPTH * MOD_ROWS, 1, N_MOD * D_MODEL)

    row3 = lambda a: a[:, None, :]
    g1n, g2n, gq, gkv = row3(norm1_g), row3(norm2_g), row3(q_norm_g), row3(kv_norm_g)
    w_in_p = _pack_w_in(w_in)
    wq, wqr, wk, wv = _pack_heads(w_uq, w_ukv)
    pw, ps = pool_w.astype(BF16), row3(pool_scale)
    cw, cb = hy_conv_w, row3(hy_conv_b)
    wp, wh, wo, wout = pool_out.astype(BF16), hy_out.astype(BF16), w_o.astype(BF16), w_out.astype(BF16)
    wf1, wf2 = w_ff1.astype(BF16), w_ff2.astype(BF16)
    twice = lambda v: jnp.tile(_pad2(row3(v), 1, HY_HALF), (1, 1, 2))
    fw1, fw2 = _blockdiag2(hy_f_w1), _blockdiag2(hy_f_w2)
    fb1, ff1, fb2, ff2 = twice(hy_f_b1), twice(hy_f_freq1), twice(hy_f_b2), twice(hy_f_freq2)
    w3d = hy_f_w3.reshape(DEPTH, HY_HALF, 2, HY_WIDTH).transpose(0, 2, 1, 3)
    z3 = jnp.zeros_like(w3d)
    fw3 = jnp.stack([jnp.concatenate([w3d, z3], axis=2), jnp.concatenate([z3, w3d], axis=2)], axis=2)
    dec = hy_decay[:, :, None, :]
    fg = final_g[None]

    h, hc = x, ctx
    for l in range(DEPTH):
        last = l == DEPTH - 1
        row_lat = lambda b, l=l: l * MOD_ROWS + b
        row_ctx = lambda b, l=l: l * MOD_ROWS + B

        def stream_pre(hs, mrow, tabs, tt):
            return _inproj(hs, mod, mrow, g1n, w_in_p, gq, gkv, wq, wqr, wk, wv, *tabs, l, tt)

        def stream_mix(hs, mrow, proj, q, kv_sources, zemb, tt, tq, tk, tr, final):
            Ls = hs.shape[1]
            a_pre, x0c, zvt = _seqlocal(proj, pw, ps, cw, cb, l, tt)
            filt_t, l1 = _hyena_filter(zemb, fw1, fb1, ff1, fw2, fb2, ff2, fw3, dec, l, Ls, tr)
            if Ls == FFT_N1 * FFT_N2 // 2:
                yt = _long_conv_lat(zvt, filt_t, l1, hy_bias, l)
            else:
                yt = _long_conv_ctx(zvt, filt_t, l1, hy_bias[l])
            o = _attention(q, kv_sources, tq, tk)
            return _mixffn(a_pre, x0c, yt, o, proj, hs, mod, mrow, wp, wh, wo, wout, g2n, wf1, wf2, fg, l, tt, final)

        proj_c, q_c, k_c, v_c = stream_pre(hc, row_ctx, tabs_ctx, Lc)
        proj_l, q_l, k_l, v_l = stream_pre(h, row_lat, tabs_lat, TT_LAT)
        h = stream_mix(h, row_lat, proj_l, q_l, [(k_c, v_c), (k_l, v_l)], z_lat, TT_LAT, TQ, TK, TR_LAT, last)
        if not last:
            hc = stream_mix(hc, row_ctx, proj_c, q_c, [(k_c, v_c)], z_ctx, Lc, Lc, Lc, Lc, False)
    return h
```

```python
import functools
import math

import jax
import jax.numpy as jnp
import numpy as np
from jax import lax
from jax.experimental import pallas as pl
from jax.experimental.pallas import tpu as pltpu

F32 = jnp.float32
BF16 = jnp.bfloat16

D_MODEL = 1024
DEPTH = 4
GRID_W = 64
EPS = 1e-6

POOL_WIDTH = 512
POOL_WINDOWS = (2, 4, 8, 16)
POOL_GROUP = 128
HY_WIDTH = 512
HY_EMB = 33
HY_BANDS = 16
HY_HALF = 64
MLA_HEADS = 8
QK_NOPE = 64
QK_ROPE = 32
QK_DIM = 96
V_HEAD = 64
Q_LORA = 384
KV_LORA = 256
MLA_SCALE = QK_DIM ** -0.5
LOG2E = math.log2(math.e)
ROPE_BASE = 10000.0
D_FF = 4 * D_MODEL
N_MOD = 6
MOD_ROWS = 8

LANE = 128
HALO = 16
HEAD_PAD = 128

COL_POOL = 0
COL_HY = 512
COL_GATE = 2048
N_PROJ = 5120

VMEM_LIMIT = 50 * 1024 * 1024
VMEM_LIMIT_FUSED = 58 * 1024 * 1024


def _cparams(sem):
    return pltpu.CompilerParams(dimension_semantics=sem, vmem_limit_bytes=VMEM_LIMIT)


def _mm(a, b):
    return jnp.dot(a, b, preferred_element_type=F32)


def _split(x):
    hi = x.astype(BF16)
    lo = (x - hi.astype(F32)).astype(BF16)
    return hi, lo


def _dot3(a, b):
    ah, al = _split(a)
    bh, bl = _split(b)
    return _mm(ah, bh) + _mm(al, bh) + _mm(ah, bl)


def _dot3c(a, bh, bl):
    ah, al = _split(a)
    return _mm(ah, bh) + _mm(al, bh) + _mm(ah, bl)


def _cdot3(ch, cl, b):
    bh, bl = _split(b)
    n = b.shape[1]
    r = _mm(ch, jnp.concatenate([bh, bl], axis=1))
    return r[:, :n] + r[:, n:] + _mm(cl, bh)


def _hl(m):
    m32 = np.asarray(m, np.float32)
    hi = m32.astype(BF16)
    lo = (m32 - hi.astype(np.float32)).astype(BF16)
    return hi, lo


def _rot_half(x):
    q = QK_ROPE // 4
    return jnp.concatenate([-x[..., q:2 * q], x[..., :q], -x[..., 3 * q:], x[..., 2 * q:3 * q]], axis=-1)


def _mod_kernel(c_ref, w_ref, b_ref, o_ref):
    c = c_ref[...]
    s = c * jax.nn.sigmoid(c)
    o_ref[0] = _dot3(s, w_ref[0]) + b_ref[0]


def _modulation(c_all, w_mod, b_mod):
    tn = 1536
    n = N_MOD * D_MODEL
    return pl.pallas_call(
        _mod_kernel,
        grid=(DEPTH, n // tn),
        in_specs=[pl.BlockSpec((MOD_ROWS, D_MODEL), lambda l, j: (0, 0)),
                  pl.BlockSpec((1, D_MODEL, tn), lambda l, j: (l, 0, j)),
                  pl.BlockSpec((1, 1, tn), lambda l, j: (l, 0, j))],
        out_specs=pl.BlockSpec((1, MOD_ROWS, tn), lambda l, j: (l, 0, j)),
        out_shape=jax.ShapeDtypeStruct((DEPTH, MOD_ROWS, n), F32),
        compiler_params=_cparams(("arbitrary", "arbitrary")),
        name="modulation",
    )(c_all, w_mod, b_mod.reshape(DEPTH, 1, n))


def _mod_norm(x, g, sc, sh):
    ms = jnp.mean(x * x, axis=-1, keepdims=True)
    return (x * lax.rsqrt(ms + EPS) * g) * (1.0 + sc) + sh


def _rms(x, g):
    ms = jnp.mean(x * x, axis=-1, keepdims=True)
    return x * lax.rsqrt(ms + EPS) * g


def _inproj_kernel(x_ref, sh_ref, sc_ref, g_ref, wa_ref, wm_ref, wg_ref, gq_ref, gkv_ref,
                   wq_ref, wqr_ref, wk_ref, wv_ref, cq_ref, sq_ref, ck_ref, sk_ref,
                   proj_ref, q_ref, k_ref, v_ref):
    xn = _mod_norm(x_ref[0], g_ref[...], sc_ref[0], sh_ref[0]).astype(BF16)
    proj_ref[0, :, :COL_GATE] = _mm(xn, wa_ref[...]).astype(proj_ref.dtype)
    proj_ref[0, :, COL_GATE:] = _mm(xn, wg_ref[...]).astype(proj_ref.dtype)
    mid = _mm(xn, wm_ref[...])
    cq = mid[:, :Q_LORA]
    ckv = mid[:, Q_LORA:Q_LORA + KV_LORA]
    o = Q_LORA + KV_LORA
    kpe = mid[:, o:o + LANE]
    kpe_rot = mid[:, o + LANE:o + 2 * LANE]
    qn = _rms(cq, gq_ref[...]).astype(BF16)
    q = _mm(qn, wq_ref[...])
    qr = _mm(qn, wqr_ref[...])
    kvn = _rms(ckv, gkv_ref[...]).astype(BF16)
    k = _mm(kvn, wk_ref[...])
    v = _mm(kvn, wv_ref[...])
    cq_t, sq_t = cq_ref[...], sq_ref[...]
    kpe_r = kpe * ck_ref[...] + kpe_rot * sk_ref[...]
    lane = lax.broadcasted_iota(jnp.int32, (1, HEAD_PAD), 1)
    for h in range(MLA_HEADS):
        sl = slice(h * HEAD_PAD, (h + 1) * HEAD_PAD)
        q_ref[0, h] = (q[:, sl] * cq_t + qr[:, sl] * sq_t).astype(q_ref.dtype)
        k_ref[0, h] = (k[:, sl] + kpe_r).astype(k_ref.dtype)
        ones_col = (lane == (V_HEAD if h % 2 == 0 else 0)).astype(F32)
        v_ref[0, h] = (v[:, sl] + ones_col).astype(v_ref.dtype)


def _inproj(h, mod, mrow, g, ws, gq, gkv, wq, wqr, wk, wv, cq, sq, ck, sk, l, tt):
    b_, L, _ = h.shape
    layer = lambda b, i: (l, 0, 0)
    resident = dict(pipeline_mode=pl.Buffered(1))
    hw = MLA_HEADS * HEAD_PAD
    table = pl.BlockSpec((tt, LANE), lambda b, i: (i, 0))
    heads = pl.BlockSpec((1, MLA_HEADS, tt, HEAD_PAD), lambda b, i: (b, 0, i, 0))
    return pl.pallas_call(
        _inproj_kernel,
        grid=(b_, L // tt),
        in_specs=[pl.BlockSpec((1, tt, D_MODEL), lambda b, i: (b, i, 0)),
                  pl.BlockSpec((1, 1, D_MODEL), lambda b, i: (mrow(b), 0, 0)),
                  pl.BlockSpec((1, 1, D_MODEL), lambda b, i: (mrow(b), 0, 1)),
                  pl.BlockSpec((None, 1, D_MODEL), layer)]
        + [pl.BlockSpec((None, D_MODEL, w.shape[2]), layer, **resident) for w in ws]
        + [pl.BlockSpec((None, 1, Q_LORA), layer), pl.BlockSpec((None, 1, KV_LORA), layer),
           pl.BlockSpec((None, Q_LORA, hw), layer, **resident), pl.BlockSpec((None, Q_LORA, hw), layer, **resident),
           pl.BlockSpec((None, KV_LORA, hw), layer, **resident), pl.BlockSpec((None, KV_LORA, hw), layer, **resident),
           table, table, table, table],
        out_specs=[pl.BlockSpec((1, tt, N_PROJ), lambda b, i: (b, i, 0)), heads, heads, heads],
        out_shape=[jax.ShapeDtypeStruct((b_, L, N_PROJ), BF16)]
        + [jax.ShapeDtypeStruct((b_, MLA_HEADS, L, HEAD_PAD), BF16)] * 3,
        compiler_params=_cparams(("arbitrary", "arbitrary")),
        name="inproj",
    )(h, mod, mod, g, *ws, gq, gkv, wq, wqr, wk, wv, cq, sq, ck, sk)


def _seqlocal_kernel(cur_ref, prev_ref, next_ref, pw_ref, ps_ref, cw_ref, cb_ref,
                     a_ref, x0_ref, zvt_ref, *, tt, L):
    i = pl.program_id(1)
    nt = pl.num_programs(1)
    keep_prev = (i > 0).astype(F32)
    keep_next = (i < nt - 1).astype(F32)
    n = tt + 2 * HALO

    def ext(lo, hi):
        return jnp.concatenate([prev_ref[0, :, lo:hi].astype(F32) * keep_prev,
                                cur_ref[0, :, lo:hi].astype(F32),
                                next_ref[0, :, lo:hi].astype(F32) * keep_next], axis=0)

    def roll(x, s):
        return pltpu.roll(x, s % n, 0)

    t = i * tt + lax.broadcasted_iota(jnp.int32, (tt, POOL_GROUP), 0)
    for gi, w in enumerate(POOL_WINDOWS):
        lo = COL_POOL + gi * POOL_GROUP
        xg = ext(lo, lo + POOL_GROUP)
        s = xg + roll(xg, 1)
        if w >= 4:
            s = roll(s, 1) + roll(s, -1)
        if w >= 8:
            s = roll(s, 2) + roll(s, -2)
        if w >= 16:
            s = roll(s, 4) + roll(s, -4)
        cnt = (jnp.minimum(t + w // 2, L) - jnp.maximum(t - w // 2, 0)).astype(F32)
        u = s[HALO:HALO + tt] / cnt - xg[HALO:HALO + tt]
        ug = _mm(u.astype(BF16), pw_ref[gi]) * ps_ref[:, gi * POOL_GROUP:(gi + 1) * POOL_GROUP]
        a_ref[0, :, gi * POOL_GROUP:(gi + 1) * POOL_GROUP] = ug.astype(a_ref.dtype)

    def conv(part, j):
        c0 = part * HY_WIDTH + j * LANE
        xe = ext(COL_HY + c0, COL_HY + c0 + LANE)
        y = (roll(xe, 1) * cw_ref[0:1, c0:c0 + LANE] + xe * cw_ref[1:2, c0:c0 + LANE]
             + roll(xe, -1) * cw_ref[2:3, c0:c0 + LANE] + cb_ref[:, c0:c0 + LANE])
        return y[HALO:HALO + tt]

    for j in range(HY_WIDTH // LANE):
        x0_ref[0, :, j * LANE:(j + 1) * LANE] = conv(0, j).astype(x0_ref.dtype)
        zv = conv(2, j) * conv(1, j)
        zvt_ref[0, j * LANE:(j + 1) * LANE, :] = zv.T


def _seqlocal(proj, pool_w, pool_scale, conv_w, conv_b, l, tt):
    b_, L, _ = proj.shape
    hb = tt // HALO
    nh = L // HALO
    wide = COL_GATE
    return pl.pallas_call(
        functools.partial(_seqlocal_kernel, tt=tt, L=L),
        grid=(b_, L // tt),
        in_specs=[pl.BlockSpec((1, tt, wide), lambda b, i: (b, i, 0)),
                  pl.BlockSpec((1, HALO, wide), lambda b, i: (b, jnp.maximum(i * hb - 1, 0), 0)),
                  pl.BlockSpec((1, HALO, wide), lambda b, i: (b, jnp.minimum((i + 1) * hb, nh - 1), 0)),
                  pl.BlockSpec((None, len(POOL_WINDOWS), POOL_GROUP, POOL_GROUP), lambda b, i: (l, 0, 0, 0)),
                  pl.BlockSpec((None, 1, POOL_WIDTH), lambda b, i: (l, 0, 0)),
                  pl.BlockSpec((None, 3, 3 * HY_WIDTH), lambda b, i: (l, 0, 0)),
                  pl.BlockSpec((None, 1, 3 * HY_WIDTH), lambda b, i: (l, 0, 0))],
        out_specs=[pl.BlockSpec((1, tt, POOL_WIDTH), lambda b, i: (b, i, 0)),
                   pl.BlockSpec((1, tt, HY_WIDTH), lambda b, i: (b, i, 0)),
                   pl.BlockSpec((1, HY_WIDTH, tt), lambda b, i: (b, 0, i))],
        out_shape=[jax.ShapeDtypeStruct((b_, L, POOL_WIDTH), BF16),
                   jax.ShapeDtypeStruct((b_, L, HY_WIDTH), BF16),
                   jax.ShapeDtypeStruct((b_, HY_WIDTH, L), F32)],
        compiler_params=_cparams(("arbitrary", "arbitrary")),
        name="seqlocal",
    )(proj, proj, proj, pool_w, pool_scale, conv_w, conv_b)


def _filter_kernel(z_ref, w1_ref, b1_ref, f1_ref, w2_ref, b2_ref, f2_ref, w3_ref, dec_ref,
                   filt_ref, l1_ref, *, tr, L):
    i = pl.program_id(0)
    half = tr // 2
    z = z_ref[...]
    hid = jnp.sin(f1_ref[...] * (_dot3(z, w1_ref[...]) + b1_ref[...]))
    hid = jnp.sin(f2_ref[...] * (_dot3(hid, w2_ref[...]) + b2_ref[...]))
    decay = jnp.abs(dec_ref[0])
    total = None
    for s in range(2):
        h = _dot3(hid, w3_ref[0, s]) * jnp.exp(-z[:, s * HY_HALF:s * HY_HALF + 1] * decay)
        row = i * tr + s * half + lax.broadcasted_iota(jnp.int32, h.shape, 0)
        h = jnp.where(row == L, 0.0, h)
        filt_ref[:, s * half:(s + 1) * half] = h.T
        part = jnp.sum(jnp.abs(h), axis=0, keepdims=True)
        total = part if total is None else total + part

    @pl.when(i == 0)
    def _():
        l1_ref[...] = jnp.zeros_like(l1_ref)

    l1_ref[...] += total


def _hyena_filter(zpacked, w1, b1, f1, w2, b2, f2, w3, dec, l, L, tr):
    n = 2 * L
    nhalf = L // tr
    full = lambda i: (0, 0)
    mat = pl.BlockSpec((None, LANE, LANE), lambda i: (l, 0, 0))
    vec = pl.BlockSpec((None, 1, LANE), lambda i: (l, 0, 0))
    return pl.pallas_call(
        functools.partial(_filter_kernel, tr=tr, L=L),
        grid=(n // tr,),
        in_specs=[pl.BlockSpec((tr // 2, LANE), lambda i: (i, 0)),
                  mat, vec, vec, mat, vec, vec,
                  pl.BlockSpec((None, 1, 2, LANE, HY_WIDTH), lambda i: (l, i // nhalf, 0, 0, 0)),
                  pl.BlockSpec((None, 1, 1, HY_WIDTH), lambda i: (l, i // nhalf, 0, 0))],
        out_specs=[pl.BlockSpec((HY_WIDTH, tr), lambda i: (0, i)),
                   pl.BlockSpec((1, HY_WIDTH), full)],
        out_shape=[jax.ShapeDtypeStruct((HY_WIDTH, n), F32),
                   jax.ShapeDtypeStruct((1, HY_WIDTH), F32)],
        compiler_params=_cparams(("arbitrary",)),
        name="hyena_filter",
    )(zpacked, w1, b1, f1, w2, b2, f2, w3, dec)


FFT_N1 = 128
FFT_N2 = 128
CONV_CB = 32


def _lconv_kernel(bias_ref, l1_ref, z_ref, f_ref, f1d_h, f1d_l, f1f_h, f1f_l, tr_ref, ti_ref,
                  f2_h, f2_l, g2_h, g2_l, e1_h, e1_l, y_ref, bs_ref, bf_ref, cm_ref, ys_ref, *, layer):
    blk = pl.program_id(0)
    tw_r = tr_ref[...]
    tw_i = ti_ref[...]
    n2 = FFT_N2
    for c in range(CONV_CB):
        rows = slice(c * FFT_N1, (c + 1) * FFT_N1)
        zc = jnp.concatenate([z_ref[0, c], z_ref[1, c]], axis=0)
        a = _cdot3(f1d_h[...], f1d_l[...], zc)
        ar, ai = a[:FFT_N1], a[FFT_N1:]
        bs_ref[rows, :n2] = ar * tw_r - ai * tw_i
        bs_ref[rows, n2:] = ar * tw_i + ai * tw_r
        inv_l1 = 1.0 / jnp.full((1, n2), l1_ref[0, blk * CONV_CB + c], F32)
        a = _cdot3(f1f_h[...], f1f_l[...], f_ref[c] * inv_l1)
        ar, ai = a[:FFT_N1], a[FFT_N1:]
        bf_ref[rows, :n2] = ar * tw_r - ai * tw_i
        bf_ref[rows, n2:] = ar * tw_i + ai * tw_r
    x = _dot3c(bs_ref[...], f2_h[...], f2_l[...])
    hf = _dot3c(bf_ref[...], f2_h[...], f2_l[...])
    xr, xi = x[:, :n2], x[:, n2:]
    hr, hi = hf[:, :n2], hf[:, n2:]
    y = jnp.concatenate([xr * hr - xi * hi, xr * hi + xi * hr], axis=1)
    cm_ref[...] = _dot3c(y, g2_h[...], g2_l[...])
    half = FFT_N1 // 2
    for c in range(CONV_CB):
        rows = slice(c * FFT_N1, (c + 1) * FFT_N1)
        cr = cm_ref[rows, :n2]
        ci = cm_ref[rows, n2:]
        dc = jnp.concatenate([cr * tw_r + ci * tw_i, ci * tw_r - cr * tw_i], axis=0)
        yc = _cdot3(e1_h[...], e1_l[...], dc)
        bias = bias_ref[layer, blk * CONV_CB + c]
        ys_ref[0, c] = yc[:half] + z_ref[0, c] * bias
        ys_ref[1, c] = yc[half:] + z_ref[1, c] * bias
    sub = 8
    for b in range(2):
        for cg in range(CONV_CB // sub):
            for ng in range(half // sub):
                blk8 = ys_ref[b, cg * sub:(cg + 1) * sub, ng * sub:(ng + 1) * sub, :]
                t8 = jnp.swapaxes(blk8, 0, 1)
                for j in range(sub):
                    n1 = ng * sub + j
                    y_ref[b, cg * sub:(cg + 1) * sub, n1 * FFT_N2:(n1 + 1) * FFT_N2] = t8[j]


def _lconv_consts():
    n1, n2 = FFT_N1, FFT_N2
    n = n1 * n2
    idx = np.arange(128)
    ang1 = 2.0 * np.pi * ((idx[:, None] * idx[None, :]) % 128) / 128.0
    fr, fi = np.cos(ang1), -np.sin(ang1)
    angt = 2.0 * np.pi * (idx[:, None] * idx[None, :]) / n
    tw_r, tw_i = np.cos(angt), -np.sin(angt)
    h = n1 // 2
    f1d = np.block([[fr[:, :h], -fi[:, :h]], [fi[:, :h], fr[:, :h]]])
    f1f = np.concatenate([fr, fi], axis=0)
    f2 = np.block([[fr, fi], [-fi, fr]])
    g2 = np.block([[fr, -fi], [fi, fr]])
    er, ei = fr[:h] / n, -fi[:h] / n
    e1 = np.block([[er, -ei], [ei, er]])

    return (*_hl(f1d), *_hl(f1f), np.asarray(tw_r, np.float32), np.asarray(tw_i, np.float32),
            *_hl(f2), *_hl(g2), *_hl(e1))


def _long_conv_lat(zvt, filt_t, l1, bias, l):
    b_, C, L = zvt.shape
    assert b_ == 2 and L == FFT_N1 * FFT_N2 // 2
    z4 = zvt.reshape(b_, C, FFT_N1 // 2, FFT_N2)
    f3 = filt_t.reshape(C, FFT_N1, FFT_N2)
    consts = _lconv_consts()
    cb = CONV_CB

    def cspec(a):
        return pl.BlockSpec(a.shape, lambda i: (0,) * a.ndim)

    return pl.pallas_call(
        functools.partial(_lconv_kernel, layer=l),
        grid=(C // cb,),
        in_specs=[pl.BlockSpec(memory_space=pltpu.SMEM), pl.BlockSpec(memory_space=pltpu.SMEM),
                  pl.BlockSpec((b_, cb, FFT_N1 // 2, FFT_N2), lambda i: (0, i, 0, 0)),
                  pl.BlockSpec((cb, FFT_N1, FFT_N2), lambda i: (i, 0, 0))] + [cspec(a) for a in consts],
        out_specs=pl.BlockSpec((b_, cb, L), lambda i: (0, i, 0)),
        out_shape=jax.ShapeDtypeStruct(zvt.shape, F32),
        scratch_shapes=[pltpu.VMEM((cb * FFT_N1, 2 * FFT_N2), F32),
                        pltpu.VMEM((cb * FFT_N1, 2 * FFT_N2), F32),
                        pltpu.VMEM((cb * FFT_N1, 2 * FFT_N2), F32),
                        pltpu.VMEM((b_, cb, FFT_N1 // 2, FFT_N2), F32)],
        compiler_params=_cparams(("arbitrary",)),
        name="long_conv",
    )(bias, l1, z4, f3, *consts)


def _cconv_kernel(z_ref, f_ref, l1_ref, bias_ref, fd_h, fd_l, ff_h, ff_l, fi_h, fi_l, y_ref, *, L):
    zr, zi = z_ref[0], z_ref[1]
    x = _dot3c(jnp.concatenate([zr, zi], axis=1), fd_h[...], fd_l[...])
    hf = _dot3c(f_ref[...] * (1.0 / l1_ref[...]), ff_h[...], ff_l[...])
    n = 2 * L
    xr, xi = x[:, :n], x[:, n:]
    hr, hi = hf[:, :n], hf[:, n:]
    y = jnp.concatenate([xr * hr - xi * hi, xr * hi + xi * hr], axis=1)
    out = _dot3c(y, fi_h[...], fi_l[...])
    bias = bias_ref[...]
    y_ref[0] = out[:, :L] + zr * bias
    y_ref[1] = out[:, L:] + zi * bias


def _cconv_consts(L):
    n = 2 * L
    idx = np.arange(n)
    ang = 2.0 * np.pi * ((idx[:, None] * idx[None, :]) % n) / n
    fr, fi = np.cos(ang), -np.sin(ang)
    fd = np.block([[fr[:L], fi[:L]], [-fi[:L], fr[:L]]])
    ff = np.concatenate([fr, fi], axis=1)
    er, ei = fr[:, :L] / n, -fi[:, :L] / n
    finv = np.block([[er, ei], [-ei, er]])

    return (*_hl(fd), *_hl(ff), *_hl(finv))


def _long_conv_ctx(zvt, filt_t, l1, bias):
    b_, C, L = zvt.shape
    consts = _cconv_consts(L)
    bias_b = jnp.broadcast_to(bias[:, None], (C, L))
    l1_col = l1.reshape(C, 1)

    def cspec(a):
        return pl.BlockSpec(a.shape, lambda i: (0,) * a.ndim)

    return pl.pallas_call(
        functools.partial(_cconv_kernel, L=L),
        grid=(1,),
        in_specs=[cspec(zvt), cspec(filt_t), cspec(l1_col), cspec(bias_b)] + [cspec(a) for a in consts],
        out_specs=cspec(zvt),
        out_shape=jax.ShapeDtypeStruct(zvt.shape, F32),
        compiler_params=_cparams(("arbitrary",)),
        name="long_conv_ctx",
    )(zvt, filt_t, l1_col, bias_b, *consts)


def _attn_kernel(q_ref, *refs, chunks):
    o_ref = refs[-1]
    tq = q_ref.shape[2]
    q = [q_ref[0, hh] for hh in range(2)]
    m = [jnp.full((tq, 1), -1e30, F32) for _ in range(2)]
    acc = [jnp.zeros((tq, LANE), F32) for _ in range(2)]
    work = [(src, start, size, hh) for (src, start, size) in chunks for hh in range(2)]

    def scores(n):
        src, start, size, hh = work[n]
        k = refs[2 * src][0, hh, start:start + size, :]
        return lax.dot_general(q[hh], k, (((1,), (1,)), ((), ())), preferred_element_type=F32)

    s_next = scores(0)
    for n, (src, start, size, hh) in enumerate(work):
        s = s_next
        if n + 1 < len(work):
            s_next = scores(n + 1)
        v = refs[2 * src + 1][0, hh, start:start + size, :]
        m_new = jnp.maximum(m[hh], jnp.max(s, axis=-1, keepdims=True))
        p = jnp.exp2(s - m_new)
        alpha = jnp.exp2(m[hh] - m_new)
        acc[hh] = alpha * acc[hh] + _mm(p.astype(BF16), v)
        m[hh] = m_new
    lane = lax.broadcasted_iota(jnp.int32, (tq, LANE), 1)
    o0 = acc[0] / acc[0][:, V_HEAD:V_HEAD + 1]
    o1 = acc[1] / acc[1][:, 0:1]
    o_ref[0] = jnp.where(lane < V_HEAD, o0, o1).astype(o_ref.dtype)


def _attention(q, kv_sources, tq, tk):
    b_, H, L, _ = q.shape
    chunks = []
    in_specs = [pl.BlockSpec((1, 2, tq, HEAD_PAD), lambda b, hp, i: (b, hp, i, 0))]
    args = [q]
    for src, (k, v) in enumerate(kv_sources):
        S = k.shape[2]
        step = min(tk, S)
        chunks += [(src, start, step) for start in range(0, S, step)]
        in_specs += [pl.BlockSpec((1, 2, S, HEAD_PAD), lambda b, hp, i: (b, hp, 0, 0))] * 2
        args += [k, v]
    return pl.pallas_call(
        functools.partial(_attn_kernel, chunks=tuple(chunks)),
        grid=(b_, H // 2, L // tq),
        in_specs=in_specs,
        out_specs=pl.BlockSpec((1, tq, LANE), lambda b, hp, i: (b, i, hp)),
        out_shape=jax.ShapeDtypeStruct((b_, L, H * V_HEAD), BF16),
        compiler_params=_cparams(("arbitrary", "arbitrary", "arbitrary")),
        name="attention",
    )(*args)


def _mixffn_kernel(a_ref, x0_ref, yt_ref, o_ref, ga_ref, gb_ref, gc_ref, h_ref, g1_ref, sh2_ref, sc2_ref, g2_ref,
                   gn_ref, wp_ref, wh_ref, wo_ref, wout_ref, w1_ref, w2_ref, fg_ref, out_ref, *, final, nsplit):
    hy = (x0_ref[0].astype(F32) * yt_ref[0].T).astype(BF16)
    a = _mm(a_ref[0], wp_ref[...])
    b = _mm(hy, wh_ref[...])
    cm = _mm(o_ref[0], wo_ref[...])
    m = (jax.nn.sigmoid(ga_ref[0].astype(F32)) * a + jax.nn.sigmoid(gb_ref[0].astype(F32)) * b
         + jax.nn.sigmoid(gc_ref[0].astype(F32)) * cm)
    x = h_ref[0] + g1_ref[0] * _mm(m.astype(BF16), wout_ref[...])
    xn = _mod_norm(x, gn_ref[...], sc2_ref[0], sh2_ref[0]).astype(BF16)
    tf = D_FF // nsplit
    acc = None
    for k in range(nsplit):
        u = jnp.maximum(_mm(xn, w1_ref[:, k * tf:(k + 1) * tf]), 0.0)
        part = _mm((u * u).astype(BF16), w2_ref[k * tf:(k + 1) * tf, :])
        acc = part if acc is None else acc + part
    y = x + g2_ref[0] * acc
    if final:
        y = _rms(y, fg_ref[...])
    out_ref[0] = y


def _mixffn(a_pre, x0c, yt, o, proj, h, mod, mrow, wp, wh, wo, wout, gn, w1, w2, final_g, l, tt, final):
    b_, L, _ = h.shape
    layer = lambda b, i: (l, 0, 0)
    row = lambda b, i: (b, i, 0)
    gcol = COL_GATE // D_MODEL
    resident = dict(pipeline_mode=pl.Buffered(1))
    modrow = lambda k: pl.BlockSpec((1, 1, D_MODEL), lambda b, i: (mrow(b), 0, k))
    return pl.pallas_call(
        functools.partial(_mixffn_kernel, final=final, nsplit=2),
        grid=(b_, L // tt),
        in_specs=[pl.BlockSpec((1, tt, POOL_WIDTH), row),
                  pl.BlockSpec((1, tt, HY_WIDTH), row),
                  pl.BlockSpec((1, HY_WIDTH, tt), lambda b, i: (b, 0, i)),
                  pl.BlockSpec((1, tt, MLA_HEADS * V_HEAD), row),
                  pl.BlockSpec((1, tt, D_MODEL), lambda b, i: (b, i, gcol)),
                  pl.BlockSpec((1, tt, D_MODEL), lambda b, i: (b, i, gcol + 1)),
                  pl.BlockSpec((1, tt, D_MODEL), lambda b, i: (b, i, gcol + 2)),
                  pl.BlockSpec((1, tt, D_MODEL), row),
                  modrow(2), modrow(3), modrow(4), modrow(5),
                  pl.BlockSpec((None, 1, D_MODEL), layer),
                  pl.BlockSpec((None, POOL_WIDTH, D_MODEL), layer, **resident),
                  pl.BlockSpec((None, HY_WIDTH, D_MODEL), layer, **resident),
                  pl.BlockSpec((None, MLA_HEADS * V_HEAD, D_MODEL), layer, **resident),
                  pl.BlockSpec((None, D_MODEL, D_MODEL), layer, **resident),
                  pl.BlockSpec((None, D_MODEL, D_FF), layer, **resident),
                  pl.BlockSpec((None, D_FF, D_MODEL), layer, **resident),
                  pl.BlockSpec((1, D_MODEL), lambda b, i: (0, 0))],
        out_specs=pl.BlockSpec((1, tt, D_MODEL), row),
        out_shape=jax.ShapeDtypeStruct(h.shape, F32),
        compiler_params=pltpu.CompilerParams(dimension_semantics=("arbitrary", "arbitrary"),
                                             vmem_limit_bytes=VMEM_LIMIT_FUSED),
        name="mixffn",
    )(a_pre, x0c, yt, o, proj, proj, proj, h, mod, mod, mod, mod, gn, wp, wh, wo, wout, w1, w2, final_g)


def _rope_tables(n_rows):
    rows = np.repeat(np.arange(n_rows, dtype=np.float64), GRID_W)
    cols = np.tile(np.arange(GRID_W, dtype=np.float64), n_rows)
    half = QK_ROPE // 2
    inv = ROPE_BASE ** (-np.arange(0, half, 2, dtype=np.float64) / half)
    ar = rows[:, None] * inv
    ac = cols[:, None] * inv
    ang = np.concatenate([ar, ar, ac, ac], axis=-1)
    return np.cos(ang), np.sin(ang)


def _head_tables(cos, sin, L):
    ones = np.ones((L, QK_NOPE))
    z64 = np.zeros((L, QK_NOPE))
    z32 = np.zeros((L, HEAD_PAD - QK_DIM))
    if cos is None:
        cos = np.ones((L, QK_ROPE))
        sin = np.zeros((L, QK_ROPE))
    cq = np.concatenate([ones, cos, z32], axis=1) * (MLA_SCALE * LOG2E)
    sq = np.concatenate([z64, sin, z32], axis=1) * (MLA_SCALE * LOG2E)
    ck = np.concatenate([z64, cos, z32], axis=1)
    sk = np.concatenate([z64, sin, z32], axis=1)
    return tuple(np.asarray(t, np.float32) for t in (cq, sq, ck, sk))


def _hy_embed(L, tr):
    t = np.linspace(0.0, 1.0, L)[:, None]
    omega = 2.0 * np.pi * np.arange(L)[:, None] / L
    bands = np.linspace(1e-4, HY_BANDS - 1, HY_BANDS)[None, :]
    z = np.concatenate([t, np.cos(omega * bands), -np.sin(omega * bands)], axis=-1)
    zfull = np.concatenate([z, np.zeros((1, HY_EMB)), z[:0:-1]], axis=0)
    zfull = np.pad(zfull, ((0, 0), (0, HY_HALF - HY_EMB)))
    packed = zfull.reshape(2 * L // tr, 2, tr // 2, HY_HALF).transpose(0, 2, 1, 3).reshape(L, 2 * HY_HALF)
    return np.asarray(packed, np.float32)


def _blockdiag2(a):
    a = _pad2(a, HY_HALF, HY_HALF)
    z = jnp.zeros_like(a)
    return jnp.concatenate([jnp.concatenate([a, z], axis=-1), jnp.concatenate([z, a], axis=-1)], axis=-2)


def _pack_w_in(w):
    z = lambda n: jnp.zeros(w.shape[:-1] + (n,), BF16)
    o_q = 512 + 3 * HY_WIDTH
    o_kv = o_q + Q_LORA
    o_pe = o_kv + KV_LORA
    o_gate = o_pe + QK_ROPE
    kpe = w[..., o_pe:o_gate].astype(BF16)
    mid = jnp.concatenate([
        w[..., o_q:o_pe].astype(BF16),
        z(QK_NOPE), kpe, z(HEAD_PAD - QK_DIM),
        z(QK_NOPE), _rot_half(kpe), z(HEAD_PAD - QK_DIM),
        z(LANE)], axis=-1)
    return w[..., :o_q].astype(BF16), mid, w[..., o_gate:].astype(BF16)


def _pack_heads(w_uq, w_ukv):
    lead = w_uq.shape[:-2]
    lanes = lambda a, lo: jnp.pad(a, [(0, 0)] * (a.ndim - 1) + [(lo, HEAD_PAD - lo - a.shape[-1])])
    wq3 = w_uq.astype(BF16).reshape(*lead, Q_LORA, MLA_HEADS, QK_DIM)
    wq = lanes(wq3, 0).reshape(*lead, Q_LORA, -1)
    wqr = lanes(_rot_half(wq3[..., QK_NOPE:]), QK_NOPE).reshape(*lead, Q_LORA, -1)
    wkv3 = w_ukv.astype(BF16).reshape(*lead, KV_LORA, MLA_HEADS, QK_NOPE + V_HEAD)
    wk = lanes(wkv3[..., :QK_NOPE], 0).reshape(*lead, KV_LORA, -1)
    v5 = wkv3[..., QK_NOPE:].reshape(*lead, KV_LORA, MLA_HEADS // 2, 2, 1, V_HEAD)
    wv = (v5 * jnp.eye(2, dtype=BF16)[:, :, None]).reshape(*lead, KV_LORA, -1)
    return wq, wqr, wk, wv


def _pad2(a, r, c):
    pads = [(0, 0)] * (a.ndim - 2) + [(0, r - a.shape[-2]), (0, c - a.shape[-1])]
    return jnp.pad(a, pads)


TT_LAT = 512
TQ = 1024
TK = 2048
TR_LAT = 1024


def kernel(x, c, ctx, c_ctx, w_mod, b_mod, norm1_g, norm2_g, w_in, pool_w, pool_scale, pool_out, hy_conv_w, hy_conv_b, hy_f_w1, hy_f_b1, hy_f_freq1, hy_f_w2, hy_f_b2, hy_f_freq2, hy_f_w3, hy_decay, hy_bias, hy_out, q_norm_g, w_uq, kv_norm_g, w_ukv, w_o, w_out, w_ff1, w_ff2, final_g):
    B, L, _ = x.shape
    Lc = ctx.shape[1]
    cos, sin = _rope_tables(L // GRID_W)
    tabs_lat = _head_tables(cos, sin, L)
    tabs_ctx = _head_tables(None, None, Lc)
    z_lat = _hy_embed(L, TR_LAT)
    z_ctx = _hy_embed(Lc, Lc)

    c_all = jnp.concatenate([c, c_ctx[None], jnp.zeros((MOD_ROWS - B - 1, D_MODEL), F32)], axis=0)
    mod = _modulation(c_all, w_mod, b_mod).reshape(DEPTH * MOD_ROWS, 1, N_MOD * D_MODEL)

    row3 = lambda a: a[:, None, :]
    g1n, g2n, gq, gkv = row3(norm1_g), row3(norm2_g), row3(q_norm_g), row3(kv_norm_g)
    w_in_p = _pack_w_in(w_in)
    wq, wqr, wk, wv = _pack_heads(w_uq, w_ukv)
    pw, ps = pool_w.astype(BF16), row3(pool_scale)
    cw, cb = hy_conv_w, row3(hy_conv_b)
    wp, wh, wo, wout = pool_out.astype(BF16), hy_out.astype(BF16), w_o.astype(BF16), w_out.astype(BF16)
    wf1, wf2 = w_ff1.astype(BF16), w_ff2.astype(BF16)
    twice = lambda v: jnp.tile(_pad2(row3(v), 1, HY_HALF), (1, 1, 2))
    fw1, fw2 = _blockdiag2(hy_f_w1), _blockdiag2(hy_f_w2)
    fb1, ff1, fb2, ff2 = twice(hy_f_b1), twice(hy_f_freq1), twice(hy_f_b2), twice(hy_f_freq2)
    w3d = hy_f_w3.reshape(DEPTH, HY_HALF, 2, HY_WIDTH).transpose(0, 2, 1, 3)
    z3 = jnp.zeros_like(w3d)
    fw3 = jnp.stack([jnp.concatenate([w3d, z3], axis=2), jnp.concatenate([z3, w3d], axis=2)], axis=2)
    dec = hy_decay[:, :, None, :]
    fg = final_g[None]

    h, hc = x, ctx
    for l in range(DEPTH):
        last = l == DEPTH - 1
        row_lat = lambda b, l=l: l * MOD_ROWS + b
        row_ctx = lambda b, l=l: l * MOD_ROWS + B

        def stream_pre(hs, mrow, tabs, tt):
            return _inproj(hs, mod, mrow, g1n, w_in_p, gq, gkv, wq, wqr, wk, wv, *tabs, l, tt)

        def stream_mix(hs, mrow, proj, q, kv_sources, zemb, tt, tq, tk, tr, final):
            Ls = hs.shape[1]
            a_pre, x0c, zvt = _seqlocal(proj, pw, ps, cw, cb, l, tt)
            filt_t, l1 = _hyena_filter(zemb, fw1, fb1, ff1, fw2, fb2, ff2, fw3, dec, l, Ls, tr)
            if Ls == FFT_N1 * FFT_N2 // 2:
                yt = _long_conv_lat(zvt, filt_t, l1, hy_bias, l)
            else:
                yt = _long_conv_ctx(zvt, filt_t, l1, hy_bias[l])
            o = _attention(q, kv_sources, tq, tk)
            return _mixffn(a_pre, x0c, yt, o, proj, hs, mod, mrow, wp, wh, wo, wout, g2n, wf1, wf2, fg, l, tt, final)

        proj_c, q_c, k_c, v_c = stream_pre(hc, row_ctx, tabs_ctx, Lc)
        proj_l, q_l, k_l, v_l = stream_pre(h, row_lat, tabs_lat, TT_LAT)
        h = stream_mix(h, row_lat, proj_l, q_l, [(k_c, v_c), (k_l, v_l)], z_lat, TT_LAT, TQ, TK, TR_LAT, last)
        if not last:
            hc = stream_mix(hc, row_ctx, proj_c, q_c, [(k_c, v_c)], z_ctx, Lc, Lc, Lc, Lc, False)
    return h
```
